```python
import math
import jax, jax.numpy as jnp
from jax import lax
import numpy as np

D_MODEL = 1024
BATCH = 4
SEQ = 4096
DEPTH = 2
DEC_BATCH = 128
DEC_SEQ = 1
PAST_LEN = 2048
PAGE_SIZE = 128

N_BRANCH = 4
N_HEADS = 4
HEAD_DIM = 64
BR_WIDTH = N_HEADS * HEAD_DIM
DIFF_DIM = HEAD_DIM // 2
ROPE_THETA = 500000.0
Q_BLOCK = 128
MOBA_BLOCK = 256
MOBA_TOPK = 3
NSA_CMP_LEN = 32
NSA_CMP_STRIDE = 16
NSA_CMP_HID = 256
NSA_SLC_BLOCK = 64
NSA_TOPK = 4
NSA_WINDOW = 512
D_FF = 2816
N_EXPERTS = 8
TOP_K = 2
D_EXPERT = 1792
EPS = 1e-6
NEG = -1e30
IN_SPLITS = (BR_WIDTH,) * 3 + (N_HEADS,) + (BR_WIDTH,) * 6 + (BR_WIDTH,) * 7 + (3 * N_HEADS,)
IN_COLS = 16 * BR_WIDTH + 4 * N_HEADS

kernel_name = 'hybrid_fox_diff_moba_nsa_decode_step'


def rms_norm(x, g):
    xf = x.astype(jnp.float32)
    y = xf * lax.rsqrt(jnp.mean(xf * xf, axis=-1, keepdims=True) + EPS)
    return (y * g.astype(jnp.float32)).astype(x.dtype)


def rope(x, pos):
    r = x.shape[-1] // 4
    half = r // 2
    inv = ROPE_THETA ** (-(jnp.arange(half, dtype=jnp.float32) / half))
    ang = pos.astype(jnp.float32)[:, None] * inv[None, :]
    shape = (1, ang.shape[0]) + (1,) * (x.ndim - 3) + (half,)
    cos = jnp.cos(ang).reshape(shape)
    sin = jnp.sin(ang).reshape(shape)
    xr = x[..., :r].astype(jnp.float32)
    x1, x2 = xr[..., :half], xr[..., half:]
    rot = jnp.concatenate([x1 * cos - x2 * sin, x2 * cos + x1 * sin], axis=-1).astype(x.dtype)
    return jnp.concatenate([rot, x[..., r:]], axis=-1)


def masked_softmax(s, mask):
    s = jnp.where(mask, s.astype(jnp.float32), NEG)
    e = jnp.where(mask, jnp.exp(s - jnp.max(s, axis=-1, keepdims=True)), 0.0)
    return e / jnp.maximum(jnp.sum(e, axis=-1, keepdims=True), 1e-30)


def to_blocks(a):
    b, s = a.shape[:2]
    return jnp.moveaxis(a.reshape((b, s // Q_BLOCK, Q_BLOCK) + a.shape[2:]), 1, 0)


def from_blocks(a):
    a = jnp.moveaxis(a, 0, 1)
    return a.reshape((a.shape[0], a.shape[1] * a.shape[2]) + a.shape[3:])


def to_key_blocks(a, blk):
    b, l, h, d = a.shape
    nb = -(-l // blk)
    a = jnp.pad(a, ((0, 0), (0, nb * blk - l), (0, 0), (0, 0)))
    return jnp.transpose(a.reshape(b, nb, blk, h, d), (0, 3, 1, 2, 4))


def paged_rows(pool, page_table):
    g = pool[page_table]
    return g.reshape((page_table.shape[0], page_table.shape[1] * pool.shape[1]) + pool.shape[2:])


def fox_attend(q, fq, pos_q, k, v, fk, pos_k):
    s = jnp.einsum('bthd,blhd->bhtl', q, k).astype(jnp.float32) * (HEAD_DIM ** -0.5)
    s = s + jnp.transpose(fq, (0, 2, 1))[:, :, :, None] - jnp.transpose(fk, (0, 2, 1))[:, :, None, :]
    p = masked_softmax(s, pos_k[None, :] <= pos_q[:, None])
    return jnp.einsum('bhtl,blhd->bthd', p.astype(v.dtype), v)


def diff_attend(q, k, v, pos_q, pos_k, lam, lam_init, g_sub):
    s = jnp.einsum('bthmd,blhmd->bmhtl', q, k).astype(jnp.float32) * (DIFF_DIM ** -0.5)
    p = masked_softmax(s, pos_k[None, :] <= pos_q[:, None])
    pd = p[:, 0] - lam * p[:, 1]
    o = jnp.einsum('bhtl,blhd->bthd', pd.astype(v.dtype), v)
    return rms_norm(o, g_sub) * (1.0 - lam_init)


def select_blocks_attend(q, pos_q, kb, vb, rel, blk, topk):
    b, h, nb = kb.shape[:3]
    t = q.shape[1]
    own = pos_q // blk
    past = jnp.arange(nb)[None, :] < own[:, None]
    _, idx = lax.top_k(jnp.where(past, rel, NEG), min(topk, nb))
    own_b = jnp.broadcast_to(own[:, None], idx.shape[:-1] + (1,))
    sel = jnp.concatenate([idx, own_b], axis=-1)
    ok = jnp.concatenate([idx < own[:, None], jnp.ones(own_b.shape, bool)], axis=-1)
    bi = jnp.arange(b)[:, None, None, None]
    hi = jnp.arange(h)[None, :, None, None]
    kg = kb[bi, hi, sel]
    vg = vb[bi, hi, sel]
    kpos = sel[..., None] * blk + jnp.arange(blk)
    mask = ok[..., None] & (kpos <= pos_q[:, None, None])
    j = sel.shape[-1]
    s = jnp.einsum('bthd,bhtjld->bhtjl', q, kg).astype(jnp.float32) * (HEAD_DIM ** -0.5)
    p = masked_softmax(s.reshape(b, h, t, j * blk), mask.reshape(b, h, t, j * blk))
    return jnp.einsum('bhtjl,bhtjld->bthd', p.reshape(b, h, t, j, blk).astype(vg.dtype), vg)


def nsa_compress(rows, pos_emb, w1, b1, w2, b2):
    b, l, h, d = rows.shape
    n = (l - NSA_CMP_LEN) // NSA_CMP_STRIDE + 1
    gidx = np.arange(n)[:, None] * NSA_CMP_STRIDE + np.arange(NSA_CMP_LEN)[None, :]
    blocks = rows[:, gidx] + pos_emb[:, None, :]
    flat = jnp.transpose(blocks, (0, 1, 3, 2, 4)).reshape(b, n, h, NSA_CMP_LEN * d)
    return jax.nn.gelu(flat @ w1 + b1) @ w2 + b2


def cmp_to_slc_map(n_cmp, n_slc):
    start = np.arange(n_cmp)[:, None] * NSA_CMP_STRIDE
    blk = np.arange(n_slc)[None, :] * NSA_SLC_BLOCK
    ov = np.minimum(start + NSA_CMP_LEN, blk + NSA_SLC_BLOCK) - np.maximum(start, blk)
    return jnp.asarray(np.maximum(ov, 0).astype(np.float32) / NSA_CMP_LEN)


def nsa_attend(q, qn, gates, pos_q, k_cmp, v_cmp, kb_s, vb_s, k_w, v_w, pos_w):
    scale = HEAD_DIM ** -0.5
    n_cmp = k_cmp.shape[1]
    cmp_end = jnp.arange(n_cmp, dtype=jnp.int32) * NSA_CMP_STRIDE + (NSA_CMP_LEN - 1)
    s_c = jnp.einsum('bthd,bnhd->bhtn', qn, k_cmp).astype(jnp.float32) * scale
    p_c = masked_softmax(s_c, cmp_end[None, :] <= pos_q[:, None])
    o_c = jnp.einsum('bhtn,bnhd->bthd', p_c.astype(v_cmp.dtype), v_cmp)
    rel = jnp.einsum('bhtn,ns->bhts', p_c, cmp_to_slc_map(n_cmp, kb_s.shape[2]))
    o_s = select_blocks_attend(q, pos_q, kb_s, vb_s, rel, NSA_SLC_BLOCK, NSA_TOPK)
    dist = pos_q[:, None] - pos_w[None, :]
    s_w = jnp.einsum('bthd,blhd->bhtl', q, k_w).astype(jnp.float32) * scale
    p_w = masked_softmax(s_w, (dist >= 0) & (dist < NSA_WINDOW) & (pos_w[None, :] >= 0))
    o_w = jnp.einsum('bhtl,blhd->bthd', p_w.astype(v_w.dtype), v_w)
    return gates[..., 0:1] * o_c + gates[..., 1:2] * o_s + gates[..., 2:3] * o_w


def swiglu(h, wg, wu, wd):
    return (jax.nn.silu(h @ wg) * (h @ wu)) @ wd


def moe_ffn(h, router, router_b, wg, wu, wd):
    logits = (h @ router).astype(jnp.float32) + router_b.astype(jnp.float32)
    top_p, top_i = lax.top_k(jax.nn.softmax(logits, axis=-1), TOP_K)
    top_p = top_p / jnp.sum(top_p, axis=-1, keepdims=True)
    comb = jnp.sum(jax.nn.one_hot(top_i, N_EXPERTS, dtype=jnp.float32) * top_p[..., None], axis=-2)
    out = jnp.zeros_like(h)
    for e in range(N_EXPERTS):
        out = out + comb[..., e:e + 1].astype(h.dtype) * swiglu(h, wg[e], wu[e], wd[e])
    return out


def mixer_rows(h, pos, lw):
    b, t = h.shape[:2]
    split_at = np.cumsum(IN_SPLITS)[:-1].tolist()
    (aq, ak, av, af, bq, bk, bv, cq, ck, cv,
     dq, dkc, dvc, dks, dvs, dkw, dvw, dg) = jnp.split(h @ lw['w_in'], split_at, axis=-1)
    heads = lambda a: a.reshape(b, t, N_HEADS, HEAD_DIM)
    pairs = lambda a: a.reshape(b, t, N_HEADS, 2, DIFF_DIM)
    qn_d = rms_norm(heads(dq), lw['nsa_qnorm'])
    return {
        'a_q': rms_norm(heads(aq), lw['fox_qnorm']),
        'a_k': rms_norm(heads(ak), lw['fox_knorm']),
        'a_v': heads(av),
        'a_logf': jax.nn.log_sigmoid(af.astype(jnp.float32) + lw['fox_fbias'].astype(jnp.float32)),
        'b_q': rope(rms_norm(pairs(bq), lw['diff_qnorm']), pos),
        'b_k': rope(rms_norm(pairs(bk), lw['diff_knorm']), pos),
        'b_v': heads(bv),
        'c_q': rope(rms_norm(heads(cq), lw['moba_qnorm']), pos),
        'c_k': rope(rms_norm(heads(ck), lw['moba_knorm']), pos),
        'c_v': heads(cv),
        'd_qn': qn_d,
        'd_q': rope(qn_d, pos),
        'd_kc': heads(dkc),
        'd_vc': heads(dvc),
        'd_ks': rope(rms_norm(heads(dks), lw['nsa_knorm'][1]), pos),
        'd_vs': heads(dvs),
        'd_kw': rope(rms_norm(heads(dkw), lw['nsa_knorm'][2]), pos),
        'd_vw': heads(dvw),
        'd_g': jax.nn.sigmoid(dg).reshape(b, t, N_HEADS, 3),
    }


def trunk_layer(x, c, pos_q, past, lw, li, ffn):
    b, t, _ = x.shape
    mod = (jax.nn.silu(c) @ lw['ada_w'] + lw['ada_b']).reshape(b, 6, 1, D_MODEL)
    sh1, sc1, g1, sh2, sc2, g2 = [mod[:, i] for i in range(6)]
    h = rms_norm(x, lw['norm_attn']) * (1.0 + sc1) + sh1
    r = mixer_rows(h, pos_q, lw)
    new = {
        'a_kv': jnp.stack([r['a_k'], r['a_v']], axis=2),
        'a_logf': r['a_logf'],
        'b_kv': jnp.stack([r['b_k'].reshape(b, t, N_HEADS, HEAD_DIM), r['b_v']], axis=2),
        'c_kv': jnp.stack([r['c_k'], r['c_v']], axis=2),
        'd_cmp_kv': jnp.stack([r['d_kc'], r['d_vc']], axis=2),
        'd_slc_kv': jnp.stack([r['d_ks'], r['d_vs']], axis=2),
    }
    win_new = jnp.stack([r['d_kw'], r['d_vw']], axis=2)
    if past is None:
        full = new
        new_win = win_new[:, t - min(NSA_WINDOW, t):]
        win_src = jnp.pad(win_new, ((0, 0), (NSA_WINDOW, 0), (0, 0), (0, 0), (0, 0)))

        def window_fn(pos_blk):
            st = pos_blk[0]
            rows = lax.dynamic_slice_in_dim(win_src, st, NSA_WINDOW + Q_BLOCK, axis=1)
            return rows, st - NSA_WINDOW + jnp.arange(NSA_WINDOW + Q_BLOCK, dtype=jnp.int32)
    else:
        full = {k: jnp.concatenate([past[k], v], axis=1) for k, v in new.items()}
        buf_len = past['d_win_kv'].shape[1]
        win_all = jnp.concatenate([past['d_win_kv'], win_new], axis=1)
        new_win = win_all[:, t:]
        w_pos = pos_q[0] - buf_len + jnp.arange(buf_len + t, dtype=jnp.int32)

        def window_fn(pos_blk):
            return win_all, w_pos

    L = full['a_kv'].shape[1]
    pos_k = jnp.arange(L, dtype=jnp.int32)
    k_a, v_a = full['a_kv'][:, :, 0], full['a_kv'][:, :, 1]
    f_k = jnp.cumsum(full['a_logf'].astype(jnp.float32), axis=1)
    f_q = f_k[:, L - t:]
    k_b = full['b_kv'][:, :, 0].reshape(b, L, N_HEADS, 2, DIFF_DIM)
    v_b = full['b_kv'][:, :, 1]
    kb_c = to_key_blocks(full['c_kv'][:, :, 0], MOBA_BLOCK)
    vb_c = to_key_blocks(full['c_kv'][:, :, 1], MOBA_BLOCK)
    kmean_c = jnp.mean(kb_c.astype(jnp.float32), axis=3)
    k_cmp = rms_norm(nsa_compress(full['d_cmp_kv'][:, :, 0], *lw['nsa_cmp_k']), lw['nsa_knorm'][0])
    v_cmp = nsa_compress(full['d_cmp_kv'][:, :, 1], *lw['nsa_cmp_v'])
    kb_s = to_key_blocks(full['d_slc_kv'][:, :, 0], NSA_SLC_BLOCK)
    vb_s = to_key_blocks(full['d_slc_kv'][:, :, 1], NSA_SLC_BLOCK)
    lam_init = 0.8 - 0.6 * math.exp(-0.3 * li)
    lq = lw['diff_lambda'].astype(jnp.float32)
    lam = jnp.exp(jnp.sum(lq[0] * lq[1])) - jnp.exp(jnp.sum(lq[2] * lq[3])) + lam_init

    def attend(qs, pos_blk):
        q_a, fq_b, q_b, q_c, qn_d, q_d, g_d = qs
        o_a = fox_attend(q_a, fq_b, pos_blk, k_a, v_a, f_k, pos_k)
        o_b = diff_attend(q_b, k_b, v_b, pos_blk, pos_k, lam, lam_init, lw['diff_subnorm'])
        rel_c = jnp.einsum('bthd,bhnd->bhtn', q_c.astype(jnp.float32), kmean_c)
        o_c = select_blocks_attend(q_c, pos_blk, kb_c, vb_c, rel_c, MOBA_BLOCK, MOBA_TOPK)
        kv_w, pos_w = window_fn(pos_blk)
        o_d = nsa_attend(q_d, qn_d, g_d, pos_blk, k_cmp, v_cmp, kb_s, vb_s,
                         kv_w[:, :, 0], kv_w[:, :, 1], pos_w)
        return jnp.stack([o.reshape(o.shape[0], o.shape[1], BR_WIDTH) for o in (o_a, o_b, o_c, o_d)], axis=2)

    qs = (r['a_q'], f_q, r['b_q'], r['c_q'], r['d_qn'], r['d_q'], r['d_g'])
    if past is None:
        outs = lax.map(lambda xs: attend(xs[0], xs[1]),
                       (tuple(to_blocks(a) for a in qs), pos_q.reshape(-1, Q_BLOCK)))
        br_out = from_blocks(outs)
    else:
        br_out = attend(qs, pos_q)
    br = jnp.einsum('btgw,gwd->btgd', br_out, lw['w_branch'])
    gates = jax.nn.sigmoid(h @ lw['w_gate'] + lw['b_gate']).reshape(b, t, N_BRANCH, D_MODEL)
    x = x + g1 * (jnp.sum(gates * br, axis=2) @ lw['w_out'])
    h2 = rms_norm(x, lw['norm_ffn']) * (1.0 + sc2) + sh2
    x = x + g2 * ffn(h2)
    new['d_win_kv'] = new_win
    return x, new


def setup_inputs(seed: int = 0) -> dict:
    key = jax.random.key(seed)
    keys = iter(jax.random.split(key, 64))
    f32 = jnp.float32

    def normal(shape, scale):
        return scale * jax.random.normal(next(keys), shape, f32)

    def gain(shape):
        return 1.0 + normal(shape, 0.05)

    D = D_MODEL
    n_pages = PAST_LEN // PAGE_SIZE
    n_pool = (DEC_BATCH * n_pages * 5) // 4
    win_rows = min(NSA_WINDOW, PAST_LEN)
    n_dense = (DEPTH + 1) // 2
    n_moe = DEPTH // 2
    kv_pool = (DEPTH, n_pool, PAGE_SIZE, 2, N_HEADS, HEAD_DIM)
    page_table = jax.random.permutation(next(keys), n_pool)[: DEC_BATCH * n_pages]
    page_table = page_table.reshape(DEC_BATCH, n_pages).astype(jnp.int32)
    cl_d = NSA_CMP_LEN * HEAD_DIM
    return {
        'x_prompt': normal((BATCH, SEQ, D), 1.0),
        'x_sample': normal((DEC_BATCH, DEC_SEQ, D), 1.0),
        'c_prompt': normal((BATCH, D), 1.0),
        'c_sample': normal((DEC_BATCH, D), 1.0),
        'cache_a_kv': normal(kv_pool, 1.0),
        'cache_a_logf': jax.nn.log_sigmoid(1.0 + normal((DEPTH, n_pool, PAGE_SIZE, N_HEADS), 1.0)),
        'cache_b_kv': normal(kv_pool, 1.0),
        'cache_c_kv': normal(kv_pool, 1.0),
        'cache_d_cmp_kv': normal(kv_pool, 1.0),
        'cache_d_slc_kv': normal(kv_pool, 1.0),
        'state_d_win_kv': normal((DEPTH, DEC_BATCH, win_rows, 2, N_HEADS, HEAD_DIM), 1.0),
        'page_table': page_table,
        'ada_w': normal((DEPTH, D, 6 * D), 0.5 * D ** -0.5),
        'ada_b': normal((DEPTH, 6 * D), 0.02),
        'norm_attn': gain((DEPTH, D)),
        'norm_ffn': gain((DEPTH, D)),
        'w_in': normal((DEPTH, D, IN_COLS), D ** -0.5),
        'fox_fbias': 1.0 + normal((DEPTH, N_HEADS), 0.1),
        'fox_qnorm': gain((DEPTH, HEAD_DIM)),
        'fox_knorm': gain((DEPTH, HEAD_DIM)),
        'diff_qnorm': gain((DEPTH, DIFF_DIM)),
        'diff_knorm': gain((DEPTH, DIFF_DIM)),
        'diff_lambda': normal((DEPTH, 4, DIFF_DIM), 0.1),
        'diff_subnorm': gain((DEPTH, HEAD_DIM)),
        'moba_qnorm': gain((DEPTH, HEAD_DIM)),
        'moba_knorm': gain((DEPTH, HEAD_DIM)),
        'nsa_qnorm': gain((DEPTH, HEAD_DIM)),
        'nsa_knorm': gain((DEPTH, 3, HEAD_DIM)),
        'nsa_cmp_pos': normal((DEPTH, 2, NSA_CMP_LEN, HEAD_DIM), 0.1),
        'nsa_cmp_w1': normal((DEPTH, 2, cl_d, NSA_CMP_HID), cl_d ** -0.5),
        'nsa_cmp_b1': normal((DEPTH, 2, NSA_CMP_HID), 0.02),
        'nsa_cmp_w2': normal((DEPTH, 2, NSA_CMP_HID, HEAD_DIM), NSA_CMP_HID ** -0.5),
        'nsa_cmp_b2': normal((DEPTH, 2, HEAD_DIM), 0.02),
        'w_branch': normal((DEPTH, N_BRANCH, BR_WIDTH, D), BR_WIDTH ** -0.5),
        'w_gate': normal((DEPTH, D, N_BRANCH * D), D ** -0.5),
        'b_gate': normal((DEPTH, N_BRANCH * D), 0.02),
        'w_out': normal((DEPTH, D, D), D ** -0.5),
        'ffn_w_gate': normal((n_dense, D, D_FF), D ** -0.5),
        'ffn_w_up': normal((n_dense, D, D_FF), D ** -0.5),
        'ffn_w_down': normal((n_dense, D_FF, D), D_FF ** -0.5),
        'moe_router': normal((n_moe, D, N_EXPERTS), D ** -0.5),
        'moe_router_b': normal((n_moe, N_EXPERTS), 0.01),
        'moe_w_gate': normal((n_moe, N_EXPERTS, D, D_EXPERT), D ** -0.5),
        'moe_w_up': normal((n_moe, N_EXPERTS, D, D_EXPERT), D ** -0.5),
        'moe_w_down': normal((n_moe, N_EXPERTS, D_EXPERT, D), D_EXPERT ** -0.5),
    }


def reference(x_prompt, x_sample, c_prompt, c_sample, cache_a_kv, cache_a_logf, cache_b_kv,
              cache_c_kv, cache_d_cmp_kv, cache_d_slc_kv, state_d_win_kv, page_table,
              ada_w, ada_b, norm_attn, norm_ffn, w_in, fox_fbias, fox_qnorm, fox_knorm,
              diff_qnorm, diff_knorm, diff_lambda, diff_subnorm, moba_qnorm, moba_knorm,
              nsa_qnorm, nsa_knorm, nsa_cmp_pos, nsa_cmp_w1, nsa_cmp_b1, nsa_cmp_w2, nsa_cmp_b2,
              w_branch, w_gate, b_gate, w_out, ffn_w_gate, ffn_w_up, ffn_w_down,
              moe_router, moe_router_b, moe_w_gate, moe_w_up, moe_w_down):
    past_len = page_table.shape[1] * cache_a_kv.shape[2]
    pos_p = jnp.arange(x_prompt.shape[1], dtype=jnp.int32)
    pos_s = past_len + jnp.arange(x_sample.shape[1], dtype=jnp.int32)
    names = ('a_kv', 'a_logf', 'b_kv', 'c_kv', 'd_cmp_kv', 'd_slc_kv', 'd_win_kv')
    st_p = {n: [] for n in names}
    st_s = {n: [] for n in names}
    xp, xs = x_prompt, x_sample
    for li in range(DEPTH):
        lw = {
            'ada_w': ada_w[li], 'ada_b': ada_b[li], 'norm_attn': norm_attn[li], 'norm_ffn': norm_ffn[li],
            'w_in': w_in[li], 'fox_fbias': fox_fbias[li], 'fox_qnorm': fox_qnorm[li], 'fox_knorm': fox_knorm[li],
            'diff_qnorm': diff_qnorm[li], 'diff_knorm': diff_knorm[li], 'diff_lambda': diff_lambda[li],
            'diff_subnorm': diff_subnorm[li], 'moba_qnorm': moba_qnorm[li], 'moba_knorm': moba_knorm[li],
            'nsa_qnorm': nsa_qnorm[li], 'nsa_knorm': nsa_knorm[li],
            'nsa_cmp_k': (nsa_cmp_pos[li][0], nsa_cmp_w1[li][0], nsa_cmp_b1[li][0], nsa_cmp_w2[li][0], nsa_cmp_b2[li][0]),
            'nsa_cmp_v': (nsa_cmp_pos[li][1], nsa_cmp_w1[li][1], nsa_cmp_b1[li][1], nsa_cmp_w2[li][1], nsa_cmp_b2[li][1]),
            'w_branch': w_branch[li], 'w_gate': w_gate[li], 'b_gate': b_gate[li], 'w_out': w_out[li],
        }
        j = li // 2
        if li % 2 == 0:
            ffn = lambda hh, j=j: swiglu(hh, ffn_w_gate[j], ffn_w_up[j], ffn_w_down[j])
        else:
            ffn = lambda hh, j=j: moe_ffn(hh, moe_router[j], moe_router_b[j], moe_w_gate[j],
                                          moe_w_up[j], moe_w_down[j])
        past = {
            'a_kv': paged_rows(cache_a_kv[li], page_table),
            'a_logf': paged_rows(cache_a_logf[li], page_table),
            'b_kv': paged_rows(cache_b_kv[li], page_table),
            'c_kv': paged_rows(cache_c_kv[li], page_table),
            'd_cmp_kv': paged_rows(cache_d_cmp_kv[li], page_table),
            'd_slc_kv': paged_rows(cache_d_slc_kv[li], page_table),
            'd_win_kv': state_d_win_kv[li],
        }
        xp, new_p = trunk_layer(xp, c_prompt, pos_p, None, lw, li, ffn)
        xs, new_s = trunk_layer(xs, c_sample, pos_s, past, lw, li, ffn)
        for n in names:
            st_p[n].append(new_p[n])
            st_s[n].append(new_s[n])
    stk = lambda lst: jnp.stack(lst, axis=0)
    return (xp, xs,
            stk(st_p['a_kv']), stk(st_p['a_logf']), stk(st_p['b_kv']), stk(st_p['c_kv']),
            stk(st_p['d_cmp_kv']), stk(st_p['d_slc_kv']), stk(st_p['d_win_kv']),
            stk(st_s['a_kv']), stk(st_s['a_logf']), stk(st_s['b_kv']), stk(st_s['c_kv']),
            stk(st_s['d_cmp_kv']), stk(st_s['d_slc_kv']), stk(st_s['d_win_kv']))
```

```python
import functools
import math

import jax
import jax.numpy as jnp
import numpy as np
from jax import lax
from jax.experimental import pallas as pl
from jax.experimental.pallas import tpu as pltpu

F32 = jnp.float32
BF16 = jnp.bfloat16

N_HEADS = 4
HEAD_DIM = 64
BR_WIDTH = N_HEADS * HEAD_DIM
PAIR_LANES = 2 * HEAD_DIM
DIFF_DIM = HEAD_DIM // 2
ROPE_THETA = 500000.0
MOBA_BLOCK = 256
MOBA_TOPK = 3
NSA_CMP_LEN = 32
NSA_CMP_STRIDE = 16
NSA_SLC_BLOCK = 64
NSA_TOPK = 4
NSA_WINDOW = 512
N_EXPERTS = 8
EPS = 1e-6
NEG = -1e30
V7X_VMEM_BYTES = 64 * 1024 * 1024
VMEM_LIMIT = V7X_VMEM_BYTES - 16 * 1024 * 1024

(G_QA, G_KA, G_VA, G_QB, G_KB, G_VB, G_QC, G_KC, G_VC,
 G_QDN, G_QD, G_KDS, G_VDS, G_KDW, G_VDW) = range(15)
N_ATT_GROUPS = 15


def _cparams(sem):
    return pltpu.CompilerParams(dimension_semantics=sem, vmem_limit_bytes=VMEM_LIMIT)


def _const_spec(shape):
    nd = len(shape)
    return pl.BlockSpec(shape, lambda *_: (0,) * nd)


def _rowmm_kernel(x_ref, w_ref, b_ref, o_ref, *, silu_in):
    x = x_ref[...]
    if silu_in:
        x = x * jax.nn.sigmoid(x)
    o_ref[...] = jnp.dot(x.astype(BF16), w_ref[...].astype(BF16),
                         preferred_element_type=F32) + b_ref[...]


def _rowmm(x, w, b, *, silu_in=False, tn=1024):
    m, k = x.shape
    n = w.shape[1]
    tn = math.gcd(tn, n)
    return pl.pallas_call(
        functools.partial(_rowmm_kernel, silu_in=silu_in),
        grid=(n // tn,),
        in_specs=[pl.BlockSpec((m, k), lambda j: (0, 0)),
                  pl.BlockSpec((k, tn), lambda j: (0, j)),
                  pl.BlockSpec((1, tn), lambda j: (0, j))],
        out_specs=pl.BlockSpec((m, tn), lambda j: (0, j)),
        out_shape=jax.ShapeDtypeStruct((m, n), F32),
        compiler_params=_cparams(("parallel",)),
        name="rowmm",
    )(x, w, b.reshape(1, n))


def _rope_tables(pos, group):
    r = group // 4
    half = r // 2
    inv = ROPE_THETA ** (-(np.arange(half, dtype=np.float32) / half))
    lane = np.arange(BR_WIDTH)
    j = lane % group
    ang = pos.astype(F32)[:, None] * jnp.asarray(inv[j % half], F32)[None, :]
    cos = jnp.where(j[None, :] < r, jnp.cos(ang), 1.0)
    sin = jnp.sin(ang)
    sin_a = jnp.where(j[None, :] < half, -sin, 0.0)
    sin_b = jnp.where((j[None, :] >= half) & (j[None, :] < r), sin, 0.0)
    return cos.astype(F32), sin_a.astype(F32), sin_b.astype(F32)


def _group_mean_matrix(width, group):
    i = np.arange(width)
    return jnp.asarray((i[:, None] // group == i[None, :] // group).astype(np.float32) / group, BF16)


def _group_rms(a, bd, gain):
    ms = jnp.dot((a * a).astype(BF16), bd, preferred_element_type=F32)
    return a * lax.rsqrt(ms + EPS) * gain


def _rope(a, cos, sin_a, sin_b, half):
    w = a.shape[-1]
    return a * cos + pltpu.roll(a, w - half, 1) * sin_a + pltpu.roll(a, half, 1) * sin_b


def _log_sigmoid(x):
    return jnp.minimum(x, 0.0) - jnp.log1p(jnp.exp(-jnp.abs(x)))


def _inproj_kernel(x_ref, sc_ref, sh_ref, gn_ref, w_ref, ws_ref, bs_ref, gains_ref,
                   bd64_ref, bd32_ref, c64_ref, sa64_ref, sb64_ref, c32_ref, sa32_ref, sb32_ref,
                   h_ref, att_ref, kva_ref, kvb_ref, kvc_ref, kvdc_ref, kvds_ref, kvdw_ref, small_ref):
    x = x_ref[...]
    ms = jnp.mean(x * x, axis=-1, keepdims=True)
    h = (x * lax.rsqrt(ms + EPS) * gn_ref[...]) * (1.0 + sc_ref[...]) + sh_ref[...]
    hb = h.astype(BF16)
    h_ref[...] = hb
    bd64 = bd64_ref[...]
    bd32 = bd32_ref[...]

    def proj(g):
        return jnp.dot(hb, w_ref[:, g * BR_WIDTH:(g + 1) * BR_WIDTH], preferred_element_type=F32)

    def gain(i):
        return gains_ref[i:i + 1, :]

    def rope64(a):
        return _rope(a, c64_ref[...], sa64_ref[...], sb64_ref[...], 8)

    def rope32(a):
        return _rope(a, c32_ref[...], sa32_ref[...], sb32_ref[...], 4)

    def att(g, a):
        att_ref[:, g * BR_WIDTH:(g + 1) * BR_WIDTH] = a.astype(BF16)

    sm_scale = HEAD_DIM ** -0.5
    att(G_QA, _group_rms(proj(0), bd64, gain(0)) * sm_scale)
    k = _group_rms(proj(1), bd64, gain(1))
    v = proj(2)
    kva_ref[:, :BR_WIDTH] = k
    kva_ref[:, BR_WIDTH:] = v
    att(G_KA, k)
    att(G_VA, v)
    att(G_QB, rope32(_group_rms(proj(3), bd32, gain(2))) * (DIFF_DIM ** -0.5))
    k = rope32(_group_rms(proj(4), bd32, gain(3)))
    v = proj(5)
    kvb_ref[:, :BR_WIDTH] = k
    kvb_ref[:, BR_WIDTH:] = v
    att(G_KB, k)
    att(G_VB, v)
    att(G_QC, rope64(_group_rms(proj(6), bd64, gain(4))) * sm_scale)
    k = rope64(_group_rms(proj(7), bd64, gain(5)))
    v = proj(8)
    kvc_ref[:, :BR_WIDTH] = k
    kvc_ref[:, BR_WIDTH:] = v
    att(G_KC, k)
    att(G_VC, v)
    qn = _group_rms(proj(9), bd64, gain(6))
    att(G_QDN, qn * sm_scale)
    att(G_QD, rope64(qn) * sm_scale)
    kvdc_ref[:, :BR_WIDTH] = proj(10)
    kvdc_ref[:, BR_WIDTH:] = proj(11)
    k = rope64(_group_rms(proj(12), bd64, gain(7)))
    v = proj(13)
    kvds_ref[:, :BR_WIDTH] = k
    kvds_ref[:, BR_WIDTH:] = v
    att(G_KDS, k)
    att(G_VDS, v)
    k = rope64(_group_rms(proj(14), bd64, gain(8)))
    v = proj(15)
    kvdw_ref[:, :BR_WIDTH] = k
    kvdw_ref[:, BR_WIDTH:] = v
    att(G_KDW, k)
    att(G_VDW, v)
    z = jnp.dot(hb, ws_ref[...], preferred_element_type=F32) + bs_ref[...]
    lane = lax.broadcasted_iota(jnp.int32, z.shape, 1)
    small_ref[...] = jnp.where(lane < N_HEADS, _log_sigmoid(z), jax.nn.sigmoid(z))


def _split_w_in(w_in, fox_fbias):
    d = w_in.shape[0]
    c0 = 3 * BR_WIDTH
    main = jnp.concatenate([w_in[:, :c0], w_in[:, c0 + N_HEADS:c0 + N_HEADS + 13 * BR_WIDTH]], axis=1)
    small = jnp.concatenate([w_in[:, c0:c0 + N_HEADS], w_in[:, c0 + N_HEADS + 13 * BR_WIDTH:],
                             jnp.zeros((d, 128 - 4 * N_HEADS), w_in.dtype)], axis=1)
    bias = jnp.concatenate([fox_fbias.astype(F32), jnp.zeros((128 - N_HEADS,), F32)]).reshape(1, 128)
    return main.astype(BF16), small.astype(BF16), bias


def _inproj(x2d, sc, sh, gn, w_main, w_small, b_small, gains, pos, *, rows_per_mod, tm):
    n, d = x2d.shape
    tm = min(tm, n)
    assert n % tm == 0
    per_tok = rows_per_mod == 1
    if not per_tok:
        assert rows_per_mod % tm == 0
    tabs = _rope_tables(pos, HEAD_DIM) + _rope_tables(pos, DIFF_DIM)
    single_pos = pos.shape[0] == 1
    tiles_per_seq = 1 if single_pos else pos.shape[0] // tm

    if per_tok:
        mod_spec = pl.BlockSpec((tm, d), lambda i: (i, 0))
    else:
        tpm = rows_per_mod // tm
        mod_spec = pl.BlockSpec((None, 1, d), lambda i: (i // tpm, 0, 0))
        sc, sh = sc.reshape(-1, 1, d), sh.reshape(-1, 1, d)
    if single_pos:
        tab_spec = pl.BlockSpec((1, BR_WIDTH), lambda i: (0, 0))
    else:
        tab_spec = pl.BlockSpec((tm, BR_WIDTH), lambda i: (i % tiles_per_seq, 0))

    row = lambda w: pl.BlockSpec((tm, w), lambda i: (i, 0))
    kv_shape = jax.ShapeDtypeStruct((n, 2 * BR_WIDTH), F32)
    out_shape = (jax.ShapeDtypeStruct((n, d), BF16),
                 jax.ShapeDtypeStruct((n, N_ATT_GROUPS * BR_WIDTH), BF16),
                 kv_shape, kv_shape, kv_shape, kv_shape, kv_shape, kv_shape,
                 jax.ShapeDtypeStruct((n, 128), F32))
    out_specs = (row(d), row(N_ATT_GROUPS * BR_WIDTH)) + (row(2 * BR_WIDTH),) * 6 + (row(128),)
    return pl.pallas_call(
        _inproj_kernel,
        grid=(n // tm,),
        in_specs=[row(d), mod_spec, mod_spec, _const_spec((1, d)),
                  _const_spec(w_main.shape), _const_spec(w_small.shape), _const_spec((1, 128)),
                  _const_spec(gains.shape),
                  _const_spec((BR_WIDTH, BR_WIDTH)), _const_spec((BR_WIDTH, BR_WIDTH))] + [tab_spec] * 6,
        out_specs=out_specs,
        out_shape=out_shape,
        compiler_params=_cparams(("parallel",)),
        name="inproj",
    )(x2d, sc, sh, gn.reshape(1, d), w_main, w_small, b_small, gains,
      _group_mean_matrix(BR_WIDTH, HEAD_DIM), _group_mean_matrix(BR_WIDTH, DIFF_DIM), *tabs)


def _tile_gain(g, group):
    return jnp.tile(g.astype(F32), BR_WIDTH // group)


def _cumsum_kernel(x_ref, tri_ref, ft_ref, carry_ref):
    @pl.when(pl.program_id(1) == 0)
    def _():
        carry_ref[...] = jnp.zeros_like(carry_ref)

    c = jnp.dot(tri_ref[...], x_ref[...], precision=lax.Precision.HIGHEST,
                preferred_element_type=F32) + carry_ref[...]
    carry_ref[...] = c[-1:, :]
    ft_ref[...] = jnp.transpose(c)[:8, :]


def _cumsum_t(small, batch, seq, *, tc=256):
    tc = min(tc, seq)
    nt = seq // tc
    tri = jnp.asarray(np.tril(np.ones((tc, tc), np.float32)))
    return pl.pallas_call(
        _cumsum_kernel,
        grid=(batch, nt),
        in_specs=[pl.BlockSpec((tc, 128), lambda b, t: (b * nt + t, 0)), _const_spec((tc, tc))],
        out_specs=pl.BlockSpec((None, 8, tc), lambda b, t: (b, 0, t)),
        out_shape=jax.ShapeDtypeStruct((batch, 8, seq), F32),
        scratch_shapes=[pltpu.VMEM((1, 128), F32)],
        compiler_params=_cparams(("parallel", "arbitrary")),
        name="cumsum_t",
    )(small, tri)


def _flash_steps(seq, tq, tk, window):
    assert tk % tq == 0 and seq % tk == 0
    qi_l, kj_l, fl_l = [], [], []
    for qi in range(seq // tq):
        t0, t1 = qi * tq, (qi + 1) * tq - 1
        js = []
        for j in range(t0 // tk, -1, -1):
            s0, s1 = j * tk, (j + 1) * tk - 1
            if window is not None and s1 <= t0 - window:
                break
            partial = s1 > t0 or (window is not None and s0 <= t1 - window)
            js.append((j, partial))
        for n, (j, partial) in enumerate(js):
            qi_l.append(qi)
            kj_l.append(j)
            fl_l.append((1 if n == 0 else 0) | (2 if n == len(js) - 1 else 0) | (4 if partial else 0))
    return (jnp.asarray(qi_l, jnp.int32), jnp.asarray(kj_l, jnp.int32), jnp.asarray(fl_l, jnp.int32))


def _dot_nt(a, b):
    return lax.dot_general(a, b, (((1,), (1,)), ((), ())), preferred_element_type=F32)


def _flash_kernel(qi_ref, kj_ref, fl_ref, *refs, aug, groups, tq, tk, fox, window, combine, lam_scale):
    refs = list(refs)
    q_ref, k_ref, v_ref = refs[:3]
    refs = refs[3:]
    mask_ref = None if aug else refs.pop(0)
    ft_ref = refs.pop(0) if fox else None
    if combine == "diff":
        lam_ref, bd_ref, gsub_ref = refs[:3]
        refs = refs[3:]
    o_ref, m_ref, l_ref, acc_ref, qs_ref = refs

    step = pl.program_id(2)
    fl = fl_ref[step]
    qi = qi_ref[step]
    kj = kj_ref[step]

    @pl.when((fl & 1) != 0)
    def _():
        m_ref[...] = jnp.full(m_ref.shape, NEG, F32)
        l_ref[...] = jnp.zeros(l_ref.shape, F32)
        acc_ref[...] = jnp.zeros(acc_ref.shape, F32)
        if not aug:
            q = q_ref[...]
            for r in range(groups):
                qs_ref[r] = q * mask_ref[r:r + 1, :]

    def body(apply_mask):
        v = v_ref[...]
        if apply_mask:
            row = qi * tq + lax.broadcasted_iota(jnp.int32, (tq, tk), 0)
            col = kj * tk + lax.broadcasted_iota(jnp.int32, (tq, tk), 1)
            valid = col <= row
            if window is not None:
                valid = jnp.logical_and(valid, col > row - window)
        for r in range(groups):
            if aug:
                s = _dot_nt(q_ref[r], k_ref[r])
            else:
                s = _dot_nt(qs_ref[r], k_ref[...])
            if fox:
                s = s - ft_ref[r:r + 1, :]
            if apply_mask:
                s = jnp.where(valid, s, NEG)
            m_prev = m_ref[r]
            m_new = jnp.maximum(m_prev, jnp.max(s, axis=-1, keepdims=True))
            alpha = jnp.exp(m_prev - m_new)
            p = jnp.exp(s - m_new)
            l_ref[r] = alpha * l_ref[r] + jnp.sum(p, axis=-1, keepdims=True)
            acc_ref[r] = alpha * acc_ref[r] + jnp.dot(p.astype(BF16), v, preferred_element_type=F32)
            m_ref[r] = m_new

    masked = (fl & 4) != 0
    pl.when(masked)(lambda: body(True))
    pl.when(jnp.logical_not(masked))(lambda: body(False))

    @pl.when((fl & 2) != 0)
    def _():
        outs = [acc_ref[r] / l_ref[r] for r in range(groups)]
        lane = lax.broadcasted_iota(jnp.int32, (tq, PAIR_LANES), 1)
        if combine == "diff":
            lam = lam_ref[0]
            o = jnp.where(lane < HEAD_DIM, outs[0] - lam * outs[1], outs[2] - lam * outs[3])
            o = _group_rms(o, bd_ref[...], gsub_ref[...]) * lam_scale
        else:
            o = jnp.where(lane < HEAD_DIM, outs[0], outs[1])
        o_ref[...] = o.astype(o_ref.dtype)


def _lane_masks(bounds):
    lane = np.arange(PAIR_LANES)
    return jnp.asarray(np.stack([(lane >= a) & (lane < b) for a, b in bounds]).astype(np.float32), BF16)


PAIR_MASKS = ((0, 64), (64, 128))
DIFF_MASKS = ((0, 32), (32, 64), (64, 96), (96, 128))


def _flash(q_src, k_src, v_src, *, batch, seq, tq, tk, q_col=None, k_col=None, v_col, aug=False,
           fox_t=None, window=None, diff=None, out_dtype=BF16):
    tq, tk = min(tq, seq), min(tk, seq)
    nq, nk = seq // tq, seq // tk
    qi_t, kj_t, fl_t = _flash_steps(seq, tq, tk, window)
    n_steps = int(qi_t.shape[0]) // 1
    groups = 4 if diff is not None else 2
    combine = "diff" if diff is not None else "pair"

    def tok(tile, col):
        n_t = seq // tile
        tab = 0 if tile == tq else 1
        return pl.BlockSpec((tile, PAIR_LANES),
                            lambda b, p, s, qi, kj, fl: (b * n_t + (qi, kj)[tab][s], col * 2 + p))

    in_specs, args = [], []
    if aug:
        in_specs += [pl.BlockSpec((None, 2, tq, PAIR_LANES), lambda b, p, s, qi, kj, fl: (b, p, qi[s], 0)),
                     pl.BlockSpec((None, 2, tk, PAIR_LANES), lambda b, p, s, qi, kj, fl: (b, p, kj[s], 0))]
    else:
        in_specs += [tok(tq, q_col), tok(tk, k_col)]
    in_specs.append(tok(tk, v_col))
    args += [q_src, k_src, v_src]
    if not aug:
        in_specs.append(_const_spec((groups, PAIR_LANES)))
        args.append(_lane_masks(DIFF_MASKS if diff is not None else PAIR_MASKS))
    if fox_t is not None:
        in_specs.append(pl.BlockSpec((None, None, 8, tk), lambda b, p, s, qi, kj, fl: (b, p, 0, kj[s])))
        args.append(fox_t)
    lam_scale = 1.0
    if diff is not None:
        lam, lam_scale, gsub = diff
        in_specs += [pl.BlockSpec(memory_space=pltpu.SMEM), _const_spec((PAIR_LANES, PAIR_LANES)),
                     _const_spec((1, PAIR_LANES))]
        args += [lam.reshape(1).astype(F32), _group_mean_matrix(PAIR_LANES, HEAD_DIM), gsub.reshape(1, PAIR_LANES)]

    kern = functools.partial(_flash_kernel, aug=aug, groups=groups, tq=tq, tk=tk, fox=fox_t is not None,
                             window=window, combine=combine, lam_scale=lam_scale)
    return pl.pallas_call(
        kern,
        grid_spec=pltpu.PrefetchScalarGridSpec(
            num_scalar_prefetch=3,
            grid=(batch, 2, n_steps),
            in_specs=in_specs,
            out_specs=pl.BlockSpec((tq, PAIR_LANES), lambda b, p, s, qi, kj, fl: (b * nq + qi[s], p)),
            scratch_shapes=[pltpu.VMEM((groups, tq, 1), F32), pltpu.VMEM((groups, tq, 1), F32),
                            pltpu.VMEM((groups, tq, PAIR_LANES), F32),
                            pltpu.VMEM((groups, tq, PAIR_LANES), BF16)]),
        out_shape=jax.ShapeDtypeStruct((batch * seq, BR_WIDTH), out_dtype),
        compiler_params=_cparams(("parallel", "parallel", "arbitrary")),
        name="flash_" + ("aug" if aug else combine) + ("_fox" if fox_t is not None else "")
             + ("_win" if window is not None else ""),
    )(qi_t, kj_t, fl_t, *args)


def _select_bias(rel, j, own, topk):
    big = jnp.int32(1 << 20)
    avail = jnp.logical_and(j >= 0, j < own)
    sel = j == own
    for _ in range(topk):
        val = jnp.where(avail, rel, NEG)
        mx = jnp.max(val, axis=-1, keepdims=True)
        cand = jnp.logical_and(avail, val == mx)
        idx = jnp.min(jnp.where(cand, j, big), axis=-1, keepdims=True)
        pick = j == idx
        sel = jnp.logical_or(sel, pick)
        avail = jnp.logical_and(avail, jnp.logical_not(pick))
    return jnp.where(sel, 0.0, NEG)


def _blockmean_kernel(x_ref, o_ref):
    j = pl.program_id(1)
    o_ref[pl.ds(j, 1), :] = jnp.mean(x_ref[...], axis=0, keepdims=True)


def _blockmean(kv, batch, seq, blk, nb_pad):
    nb = seq // blk
    out = pl.pallas_call(
        _blockmean_kernel,
        grid=(batch, nb),
        in_specs=[pl.BlockSpec((blk, BR_WIDTH), lambda b, j: (b * nb + j, 0))],
        out_specs=pl.BlockSpec((None, nb, BR_WIDTH), lambda b, j: (b, 0, 0)),
        out_shape=jax.ShapeDtypeStruct((batch, nb, BR_WIDTH), F32),
        compiler_params=_cparams(("parallel", "arbitrary")),
        name="blockmean",
    )(kv)
    return jnp.pad(out, ((0, 0), (0, nb_pad - nb), (0, 0)))


def _augment(x, bias_or_onehot, r):
    lane = lax.broadcasted_iota(jnp.int32, x.shape, 1)
    mine = (lane < HEAD_DIM) if r == 0 else (lane >= HEAD_DIM)
    return jnp.where(mine, x.astype(F32), bias_or_onehot).astype(BF16)


def _payload_block_id(shape, r):
    lane = lax.broadcasted_iota(jnp.int32, shape, 1)
    return (lane - HEAD_DIM) if r == 0 else jnp.where(lane < HEAD_DIM, lane, -1)


def _moba_router_kernel(q_ref, k_ref, km_ref, mask_ref, qa_ref, ka_ref, *, tq, blk, nb_pad):
    qi = pl.program_id(2)
    q = q_ref[...]
    k = k_ref[...]
    row = qi * tq + lax.broadcasted_iota(jnp.int32, (tq, 1), 0)
    own = row // blk
    for r in range(2):
        km = (km_ref[...] * mask_ref[r:r + 1, :].astype(F32)).astype(BF16)
        lo = HEAD_DIM if r == 0 else 0
        pieces = [km, jnp.zeros((PAIR_LANES - nb_pad, PAIR_LANES), BF16)]
        if lo:
            pieces = [jnp.zeros((lo, PAIR_LANES), BF16), km, jnp.zeros((PAIR_LANES - lo - nb_pad, PAIR_LANES), BF16)]
        rel = _dot_nt(q * mask_ref[r:r + 1, :], jnp.concatenate(pieces, axis=0))
        j = _payload_block_id((tq, PAIR_LANES), r)
        bias = _select_bias(rel, j, own, MOBA_TOPK)
        qa_ref[r] = _augment(q, bias, r)
        ka_ref[r] = _augment(k, (j == own).astype(F32), r)


def _moba_router(att, kmean, batch, seq, *, tq):
    tq = min(tq, seq)
    nq = seq // tq
    nb_pad = kmean.shape[1]
    out = jax.ShapeDtypeStruct((batch, N_HEADS, seq, PAIR_LANES), BF16)
    aug_spec = pl.BlockSpec((None, 2, tq, PAIR_LANES), lambda b, p, i: (b, p, i, 0))
    return pl.pallas_call(
        functools.partial(_moba_router_kernel, tq=tq, blk=MOBA_BLOCK, nb_pad=nb_pad),
        grid=(batch, 2, nq),
        in_specs=[pl.BlockSpec((tq, PAIR_LANES), lambda b, p, i: (b * nq + i, G_QC * 2 + p)),
                  pl.BlockSpec((tq, PAIR_LANES), lambda b, p, i: (b * nq + i, G_KC * 2 + p)),
                  pl.BlockSpec((None, nb_pad, PAIR_LANES), lambda b, p, i: (b, 0, p)),
                  _const_spec((2, PAIR_LANES))],
        out_specs=(aug_spec, aug_spec),
        out_shape=(out, out),
        compiler_params=_cparams(("parallel", "parallel", "parallel")),
        name="moba_router",
    )(att, att, kmean, _lane_masks(PAIR_MASKS))


def _compress_kernel(c_ref, w1_ref, b1_ref, w2_ref, b2_ref, gain_ref, o_ref, *, rows):
    kv = pl.program_id(0)
    uv = jnp.dot(c_ref[...].astype(BF16), w1_ref[...], preferred_element_type=F32)
    hid = uv.shape[1] // 2
    pre = uv[:, :hid] + pltpu.roll(uv[:, hid:], rows - 1, 0) + b1_ref[...]
    y = jnp.dot(jax.nn.gelu(pre, approximate=True).astype(BF16), w2_ref[...],
                preferred_element_type=F32) + b2_ref[...]
    ms = jnp.mean(y * y, axis=-1, keepdims=True)
    yn = y * lax.rsqrt(ms + EPS) * gain_ref[...]
    o_ref[...] = jnp.where(kv == 0, yn, y)


def _compress(chunks, pos_emb, w1, b1, w2, b2, k_gain, *, chunks_per_seq, seqs_per_tile):
    _, r, cw = chunks.shape
    hid = w1.shape[-1]
    rows = chunks_per_seq * seqs_per_tile
    assert r % rows == 0
    w1ab = jnp.concatenate([w1[:, :cw], w1[:, cw:]], axis=-1).astype(BF16)
    bias1 = (jnp.einsum('kf,kfh->kh', pos_emb.reshape(2, -1).astype(F32), w1.astype(F32),
                        precision=lax.Precision.HIGHEST) + b1.astype(F32)).reshape(2, 1, hid)
    gain = k_gain.astype(F32).reshape(1, HEAD_DIM)
    return pl.pallas_call(
        functools.partial(_compress_kernel, rows=rows),
        grid=(2, r // rows),
        in_specs=[pl.BlockSpec((None, rows, cw), lambda kv, i: (kv, i, 0)),
                  pl.BlockSpec((None, cw, 2 * hid), lambda kv, i: (kv, 0, 0)),
                  pl.BlockSpec((None, 1, hid), lambda kv, i: (kv, 0, 0)),
                  pl.BlockSpec((None, hid, HEAD_DIM), lambda kv, i: (kv, 0, 0)),
                  pl.BlockSpec((None, 1, HEAD_DIM), lambda kv, i: (kv, 0, 0)),
                  _const_spec((1, HEAD_DIM))],
        out_specs=pl.BlockSpec((None, rows, HEAD_DIM), lambda kv, i: (kv, i, 0)),
        out_shape=jax.ShapeDtypeStruct((2, r, HEAD_DIM), F32),
        compiler_params=_cparams(("parallel", "parallel")),
        name="nsa_compress",
    )(chunks, w1ab, bias1, w2.astype(BF16), b2.astype(F32).reshape(2, 1, HEAD_DIM), gain)


def _cmp_to_slc_map(n_cmp_pad, n_cmp):
    start = np.arange(n_cmp_pad)[:, None] * NSA_CMP_STRIDE
    blk = np.arange(HEAD_DIM)[None, :] * NSA_SLC_BLOCK
    ov = np.minimum(start + NSA_CMP_LEN, blk + NSA_SLC_BLOCK) - np.maximum(start, blk)
    m = np.maximum(ov, 0).astype(np.float32) / NSA_CMP_LEN
    m[n_cmp:] = 0.0
    z = np.zeros_like(m)
    return jnp.asarray(np.stack([np.concatenate([z, m], 1), np.concatenate([m, z], 1)]), BF16)


def _nsa_cmp_kernel(qn_ref, q_ref, k_ref, kc_ref, vc_ref, map_ref, mask_ref,
                    oc_ref, qa_ref, ka_ref, *, tq, n_pad):
    qi = pl.program_id(2)
    qn = qn_ref[...]
    q = q_ref[...]
    k = k_ref[...]
    kc = kc_ref[...].astype(BF16)
    vc = vc_ref[...].astype(BF16)
    row = qi * tq + lax.broadcasted_iota(jnp.int32, (tq, 1), 0)
    own = row // NSA_SLC_BLOCK
    cmp_end = lax.broadcasted_iota(jnp.int32, (tq, n_pad), 1) * NSA_CMP_STRIDE + (NSA_CMP_LEN - 1)
    visible = cmp_end <= row
    outs = []
    for r in range(2):
        s = jnp.where(visible, _dot_nt(qn * mask_ref[r:r + 1, :], kc), NEG)
        e = jnp.where(visible, jnp.exp(s - jnp.max(s, axis=-1, keepdims=True)), 0.0)
        p = e / jnp.maximum(jnp.sum(e, axis=-1, keepdims=True), 1e-30)
        p_hi = p.astype(BF16)
        p_lo = (p - p_hi.astype(F32)).astype(BF16)
        outs.append(jnp.dot(p_hi, vc, preferred_element_type=F32))
        rel = (jnp.dot(p_hi, map_ref[r], preferred_element_type=F32)
               + jnp.dot(p_lo, map_ref[r], preferred_element_type=F32))
        j = _payload_block_id((tq, PAIR_LANES), r)
        bias = _select_bias(rel, j, own, NSA_TOPK)
        qa_ref[r] = _augment(q, bias, r)
        ka_ref[r] = _augment(k, (j == own).astype(F32), r)
    lane = lax.broadcasted_iota(jnp.int32, (tq, PAIR_LANES), 1)
    oc_ref[...] = jnp.where(lane < HEAD_DIM, outs[0], outs[1])


def _nsa_cmp(att, cmp_kv, batch, seq, n_cmp, *, tq):
    tq = min(tq, seq)
    nq = seq // tq
    n_pad = cmp_kv.shape[1]
    aug = jax.ShapeDtypeStruct((batch, N_HEADS, seq, PAIR_LANES), BF16)
    aug_spec = pl.BlockSpec((None, 2, tq, PAIR_LANES), lambda b, p, i: (b, p, i, 0))
    tok = lambda g: pl.BlockSpec((tq, PAIR_LANES), lambda b, p, i: (b * nq + i, g * 2 + p))
    return pl.pallas_call(
        functools.partial(_nsa_cmp_kernel, tq=tq, n_pad=n_pad),
        grid=(batch, 2, nq),
        in_specs=[tok(G_QDN), tok(G_QD), tok(G_KDS),
                  pl.BlockSpec((None, n_pad, PAIR_LANES), lambda b, p, i: (b, 0, p)),
                  pl.BlockSpec((None, n_pad, PAIR_LANES), lambda b, p, i: (b, 0, 2 + p)),
                  _const_spec((2, n_pad, PAIR_LANES)), _const_spec((2, PAIR_LANES))],
        out_specs=(pl.BlockSpec((tq, PAIR_LANES), lambda b, p, i: (b * nq + i, p)), aug_spec, aug_spec),
        out_shape=(jax.ShapeDtypeStruct((batch * seq, BR_WIDTH), F32), aug, aug),
        compiler_params=_cparams(("parallel", "parallel", "parallel")),
        name="nsa_cmp",
    )(att, att, att, cmp_kv, cmp_kv, _cmp_to_slc_map(n_pad, n_cmp), _lane_masks(PAIR_MASKS))


def _resident_spec(shape):
    nd = len(shape)
    return pl.BlockSpec(shape, lambda *_: (0,) * nd, pipeline_mode=pl.Buffered(1))


def _gate_expand_matrix():
    m = np.zeros((3, 128, BR_WIDTH), np.float32)
    for c in range(3):
        for h in range(N_HEADS):
            m[c, N_HEADS + 3 * h + c, h * HEAD_DIM:(h + 1) * HEAD_DIM] = 1.0
    return jnp.asarray(m, BF16)


def _merge_kernel(x_ref, hb_ref, oa_ref, ob_ref, oc_ref, odc_ref, ods_ref, odw_ref, small_ref,
                  g1_ref, sc2_ref, sh2_ref, gn2_ref, ex_ref, wbr_ref, wg_ref, bg_ref, wo_ref,
                  xo_ref, h2_ref):
    d = x_ref.shape[1]
    sm = small_ref[...]
    sm_hi = sm.astype(BF16)
    sm_lo = (sm - sm_hi.astype(F32)).astype(BF16)
    o_d = jnp.zeros(odc_ref.shape, F32)
    for c, ref in enumerate((odc_ref, ods_ref, odw_ref)):
        gate = (jnp.dot(sm_hi, ex_ref[c], preferred_element_type=F32)
                + jnp.dot(sm_lo, ex_ref[c], preferred_element_type=F32))
        o_d = o_d + gate * ref[...]
    hb = hb_ref[...]
    branches = (oa_ref[...], ob_ref[...], oc_ref[...], o_d.astype(BF16))
    merged = jnp.zeros((x_ref.shape[0], d), F32)
    for g, o in enumerate(branches):
        br = jnp.dot(o, wbr_ref[g], preferred_element_type=F32)
        z = jnp.dot(hb, wg_ref[:, g * d:(g + 1) * d], preferred_element_type=F32) + bg_ref[:, g * d:(g + 1) * d]
        merged = merged + jax.nn.sigmoid(z) * br
    y = jnp.dot(merged.astype(BF16), wo_ref[...], preferred_element_type=F32)
    x = x_ref[...] + g1_ref[...] * y
    xo_ref[...] = x
    ms = jnp.mean(x * x, axis=-1, keepdims=True)
    h2 = (x * lax.rsqrt(ms + EPS) * gn2_ref[...]) * (1.0 + sc2_ref[...]) + sh2_ref[...]
    h2_ref[...] = h2.astype(BF16)


def _mod_spec(mod, d, tm, rows_per_mod):
    if rows_per_mod == 1:
        return pl.BlockSpec((tm, d), lambda i: (i, 0)), mod
    tpm = rows_per_mod // tm
    return pl.BlockSpec((None, 1, d), lambda i: (i // tpm, 0, 0)), mod.reshape(-1, 1, d)


def _merge(x2d, hb, o_a, o_b, o_c, o_dc, o_ds, o_dw, small, g1, sc2, sh2, gn2,
           w_branch, w_gate, b_gate, w_out, *, rows_per_mod, tm):
    n, d = x2d.shape
    tm = min(tm, n)
    row = lambda w: pl.BlockSpec((tm, w), lambda i: (i, 0))
    mods = [_mod_spec(m, d, tm, rows_per_mod) for m in (g1, sc2, sh2)]
    return pl.pallas_call(
        _merge_kernel,
        grid=(n // tm,),
        in_specs=[row(d), row(d)] + [row(BR_WIDTH)] * 6 + [row(128)] + [m[0] for m in mods]
                 + [_const_spec((1, d)), _const_spec((3, 128, BR_WIDTH)),
                    _resident_spec(w_branch.shape), _resident_spec(w_gate.shape),
                    _const_spec((1, 4 * d)), _resident_spec(w_out.shape)],
        out_specs=(row(d), row(d)),
        out_shape=(jax.ShapeDtypeStruct((n, d), F32), jax.ShapeDtypeStruct((n, d), BF16)),
        compiler_params=_cparams(("parallel",)),
        name="merge",
    )(x2d, hb, o_a, o_b, o_c, o_dc, o_ds, o_dw, small, *[m[1] for m in mods],
      gn2.reshape(1, d).astype(F32), _gate_expand_matrix(),
      w_branch.astype(BF16), w_gate.astype(BF16), b_gate.reshape(1, 4 * d).astype(F32), w_out.astype(BF16))


def _swiglu_acc(h, wg_ref, wu_ref, wd_ref, chunk):
    ff = wg_ref.shape[-1]
    acc = None
    for c0 in range(0, ff, chunk):
        g = jnp.dot(h, wg_ref[:, c0:c0 + chunk], preferred_element_type=F32)
        u = jnp.dot(h, wu_ref[:, c0:c0 + chunk], preferred_element_type=F32)
        part = jnp.dot((g * jax.nn.sigmoid(g) * u).astype(BF16), wd_ref[c0:c0 + chunk, :],
                       preferred_element_type=F32)
        acc = part if acc is None else acc + part
    return acc


def _ffn_kernel(x_ref, h_ref, g2_ref, wg_ref, wu_ref, wd_ref, o_ref, *, chunk):
    o_ref[...] = x_ref[...] + g2_ref[...] * _swiglu_acc(h_ref[...], wg_ref, wu_ref, wd_ref, chunk)


def _ff_chunk(ff):
    for c in (1408, 1024, 896, 512, 256, 128):
        if ff % c == 0:
            return c
    return ff


def _ffn(x2d, h2, g2, wg, wu, wd, *, rows_per_mod, tm):
    n, d = x2d.shape
    tm = min(tm, n)
    row = lambda w: pl.BlockSpec((tm, w), lambda i: (i, 0))
    g2_spec, g2 = _mod_spec(g2, d, tm, rows_per_mod)
    return pl.pallas_call(
        functools.partial(_ffn_kernel, chunk=_ff_chunk(wg.shape[1])),
        grid=(n // tm,),
        in_specs=[row(d), row(d), g2_spec, _resident_spec(wg.shape), _resident_spec(wu.shape),
                  _resident_spec(wd.shape)],
        out_specs=row(d),
        out_shape=jax.ShapeDtypeStruct((n, d), F32),
        compiler_params=_cparams(("parallel",)),
        name="ffn",
    )(x2d, h2, g2, wg.astype(BF16), wu.astype(BF16), wd.astype(BF16))


def _moe_kernel(x_ref, h_ref, g2_ref, wr_ref, br_ref, wg_ref, wu_ref, wd_ref, o_ref,
                comb_ref, acc_ref, *, chunk, n_exp):
    e = pl.program_id(1)
    h = h_ref[...]

    @pl.when(e == 0)
    def _():
        logits = jnp.dot(h, wr_ref[...], preferred_element_type=F32) + br_ref[...]
        lane = lax.broadcasted_iota(jnp.int32, logits.shape, 1)
        real = lane < n_exp
        logits = jnp.where(real, logits, NEG)
        ex = jnp.exp(logits - jnp.max(logits, axis=-1, keepdims=True))
        prob = ex / jnp.sum(ex, axis=-1, keepdims=True)
        avail = real
        comb = jnp.zeros(prob.shape, F32)
        for _ in range(2):
            val = jnp.where(avail, prob, -1.0)
            mx = jnp.max(val, axis=-1, keepdims=True)
            idx = jnp.min(jnp.where(jnp.logical_and(avail, val == mx), lane, 1 << 20), axis=-1, keepdims=True)
            pick = lane == idx
            comb = jnp.where(pick, prob, comb)
            avail = jnp.logical_and(avail, jnp.logical_not(pick))
        comb_ref[...] = comb / jnp.sum(comb, axis=-1, keepdims=True)
        acc_ref[...] = jnp.zeros(acc_ref.shape, F32)

    lane = lax.broadcasted_iota(jnp.int32, comb_ref.shape, 1)
    w_e = jnp.sum(jnp.where(lane == e, comb_ref[...], 0.0), axis=-1, keepdims=True)

    @pl.when(jnp.max(w_e) > 0.0)
    def _():
        acc_ref[...] += w_e * _swiglu_acc(h, wg_ref, wu_ref, wd_ref, chunk)

    @pl.when(e == n_exp - 1)
    def _():
        o_ref[...] = x_ref[...] + g2_ref[...] * acc_ref[...]


def _moe(x2d, h2, g2, router, router_b, wg, wu, wd, *, rows_per_mod, tm):
    n, d = x2d.shape
    n_exp = wg.shape[0]
    tm = min(tm, n)
    row = lambda w: pl.BlockSpec((tm, w), lambda i, e: (i, 0))
    if rows_per_mod == 1:
        g2_spec = row(d)
    else:
        tpm = rows_per_mod // tm
        g2_spec = pl.BlockSpec((None, 1, d), lambda i, e: (i // tpm, 0, 0))
        g2 = g2.reshape(-1, 1, d)
    wr = jnp.pad(router.astype(BF16), ((0, 0), (0, 128 - n_exp)))
    br = jnp.pad(router_b.astype(F32), (0, 128 - n_exp)).reshape(1, 128)
    exp_spec = lambda a: pl.BlockSpec((None,) + a.shape[1:], lambda i, e: (e, 0, 0))
    return pl.pallas_call(
        functools.partial(_moe_kernel, chunk=_ff_chunk(wg.shape[2]), n_exp=n_exp),
        grid=(n // tm, n_exp),
        in_specs=[row(d), row(d), g2_spec,
                  pl.BlockSpec((d, 128), lambda i, e: (0, 0)), pl.BlockSpec((1, 128), lambda i, e: (0, 0)),
                  exp_spec(wg), exp_spec(wu), exp_spec(wd)],
        out_specs=row(d),
        out_shape=jax.ShapeDtypeStruct((n, d), F32),
        scratch_shapes=[pltpu.VMEM((tm, 128), F32), pltpu.VMEM((tm, d), F32)],
        compiler_params=_cparams(("parallel", "arbitrary")),
        name="moe",
    )(x2d, h2, g2, wr, br, wg.astype(BF16), wu.astype(BF16), wd.astype(BF16))


def _decode_row_masks(diff):
    lane = np.arange(BR_WIDTH)
    qm = np.zeros((8, BR_WIDTH), np.float32)
    om = np.zeros((8, BR_WIDTH), np.float32)
    for r in range(8 if diff else N_HEADS):
        h = r // 2 if diff else r
        lo, hi = (h * HEAD_DIM + (r % 2) * DIFF_DIM, h * HEAD_DIM + (r % 2 + 1) * DIFF_DIM) if diff \
            else (h * HEAD_DIM, (h + 1) * HEAD_DIM)
        qm[r] = (lane >= lo) & (lane < hi)
        om[r] = (lane >= h * HEAD_DIM) & (lane < (h + 1) * HEAD_DIM)
    return jnp.asarray(qm, BF16), jnp.asarray(om, F32)


def _segment_matrices(n_keys, blk):
    seg = (np.arange(n_keys)[:, None] // blk == np.arange(128)[None, :]).astype(np.float32)
    return jnp.asarray(seg / blk, F32), jnp.asarray(seg.T, BF16)


def _decode_kernel(pt_ref, *refs, n_pg, mode, fox, diff, lam_scale, skip_first):
    del pt_ref
    refs = list(refs)
    q_ref, kvn_ref = refs[:2]
    pages = refs[2:2 + n_pg]
    refs = refs[2 + n_pg:]
    if fox:
        lf_pages = refs[:n_pg]
        smalln_ref, tri_ref, diag_ref = refs[n_pg:n_pg + 3]
        refs = refs[n_pg + 3:]
    if mode == "sel":
        sel_ref = refs.pop(0)
    qm_ref, om_ref = refs[:2]
    refs = refs[2:]
    if mode == "moba":
        segm_ref, segt_ref = refs[:2]
        refs = refs[2:]
    elif mode == "sel":
        segt_ref = refs.pop(0)
    if diff:
        lam_ref, bd_ref, gsub_ref = refs[:3]
        refs = refs[3:]
    (o_ref,) = refs

    qrows = q_ref[...] * qm_ref[...]
    kn = kvn_ref[:, :BR_WIDTH].astype(BF16).astype(F32)
    vn = kvn_ref[:, BR_WIDTH:].astype(BF16).astype(F32)
    s_new = jnp.sum(qrows.astype(F32) * kn, axis=-1, keepdims=True)
    s = jnp.concatenate([jnp.dot(qrows, pg[0].astype(BF16), preferred_element_type=F32) for pg in pages], axis=1)
    n_keys = s.shape[1]
    if fox:
        carry = jnp.sum(diag_ref[...] * smalln_ref[...], axis=-1, keepdims=True)
        parts = []
        for lf_ref in reversed(lf_pages):
            lf = jnp.concatenate([lf_ref[...], jnp.zeros((8 - N_HEADS, lf_ref.shape[1]), F32)], axis=0)
            parts.append(jnp.dot(lf, tri_ref[...], precision=lax.Precision.HIGHEST,
                                 preferred_element_type=F32) + carry)
            carry = carry + jnp.sum(lf, axis=-1, keepdims=True)
        s = s + jnp.concatenate(parts[::-1], axis=1)
    keep = None
    if mode == "moba":
        rel = jnp.dot(s, segm_ref[...], precision=lax.Precision.HIGHEST, preferred_element_type=F32)
        lane = lax.broadcasted_iota(jnp.int32, rel.shape, 1)
        nb = n_keys // MOBA_BLOCK
        picked = _select_bias(rel, jnp.where(lane < nb, lane, -1), jnp.full((8, 1), nb, jnp.int32), MOBA_TOPK)
        sel = jnp.where(jnp.logical_and(picked == 0.0, lane < nb), 1.0, 0.0).astype(BF16)
        keep = jnp.dot(sel, segt_ref[...], preferred_element_type=F32) > 0.5
    elif mode == "sel":
        keep = jnp.dot(sel_ref[...].astype(BF16), segt_ref[...], preferred_element_type=F32) > 0.5
    if skip_first:
        fresh = lax.broadcasted_iota(jnp.int32, s.shape, 1) >= skip_first
        keep = fresh if keep is None else jnp.logical_and(keep, fresh)
    if keep is not None:
        s = jnp.where(keep, s, NEG)
    m = jnp.maximum(jnp.max(s, axis=-1, keepdims=True), s_new)
    e = jnp.exp(s - m)
    e_new = jnp.exp(s_new - m)
    denom = jnp.sum(e, axis=-1, keepdims=True) + e_new
    eb = e.astype(BF16)
    w = pages[0].shape[-1]
    o = e_new.astype(BF16).astype(F32) * vn
    for i, pg in enumerate(pages):
        o = o + _dot_nt(eb[:, i * w:(i + 1) * w], pg[1].astype(BF16))
    o = o / denom
    if diff:
        row = lax.broadcasted_iota(jnp.int32, (8, 1), 0)
        o = o * jnp.where(row % 2 == 0, 1.0, -lam_ref[0])
    out = jnp.sum(o * om_ref[...], axis=0, keepdims=True)
    if diff:
        out = _group_rms(out, bd_ref[...], gsub_ref[...]) * lam_scale
    o_ref[...] = out


def _decode_attn(att3, q_col, kvn3, cache_t, li, page_table, *, mode="plain", fox=None, diff=None,
                 sel=None, blk=None, dense=False, skip_first=0):
    s_n = att3.shape[0]
    rows = cache_t.shape[-1]
    n_pg = 1 if dense else page_table.shape[1]
    n_keys = n_pg * rows
    pt = page_table.reshape(-1).astype(jnp.int32)
    if dense:
        page_specs = [pl.BlockSpec((None, None, 2, BR_WIDTH, rows), lambda b, pt: (li, b, 0, 0, 0))]
    else:
        page_specs = [pl.BlockSpec((None, None, 2, BR_WIDTH, rows),
                                   lambda b, pt, j=j: (li, pt[b * n_pg + j], 0, 0, 0)) for j in range(n_pg)]
    in_specs = [pl.BlockSpec((None, 1, BR_WIDTH), lambda b, pt: (b, 0, q_col)),
                pl.BlockSpec((None, 1, 2 * BR_WIDTH), lambda b, pt: (b, 0, 0))] + page_specs
    args = [att3, kvn3] + [cache_t] * n_pg
    cst = lambda a: (pl.BlockSpec(a.shape, lambda b, pt: (0,) * a.ndim), a)
    consts = []
    if fox is not None:
        logf_t, small3 = fox
        in_specs += [pl.BlockSpec((None, None, N_HEADS, rows), lambda b, pt, j=j: (li, pt[b * n_pg + j], 0, 0))
                     for j in range(n_pg)]
        args += [logf_t] * n_pg
        in_specs.append(pl.BlockSpec((None, 1, 128), lambda b, pt: (b, 0, 0)))
        args.append(small3)
        consts += [cst(jnp.asarray(np.tril(np.ones((rows, rows), np.float32), -1))),
                   cst(jnp.asarray(np.eye(8, 128, dtype=np.float32) * (np.arange(8)[:, None] < N_HEADS)))]
    qm, om = _decode_row_masks(diff is not None)
    consts += [cst(qm), cst(om)]
    if mode == "moba":
        segm, segt = _segment_matrices(n_keys, MOBA_BLOCK)
        consts += [cst(segm), cst(segt)]
    elif mode == "sel":
        in_specs.append(pl.BlockSpec((None, 8, 128), lambda b, pt: (b, 0, 0)))
        args.append(sel)
        consts.append(cst(_segment_matrices(n_keys, blk)[1]))
    in_specs += [c[0] for c in consts]
    args += [c[1] for c in consts]
    lam_scale = 1.0
    if diff is not None:
        lam, lam_scale, gsub = diff
        bd = _group_mean_matrix(BR_WIDTH, HEAD_DIM)
        in_specs += [pl.BlockSpec(memory_space=pltpu.SMEM), pl.BlockSpec(bd.shape, lambda b, pt: (0, 0)),
                     pl.BlockSpec((1, BR_WIDTH), lambda b, pt: (0, 0))]
        args += [lam.reshape(1).astype(F32), bd, gsub.reshape(1, BR_WIDTH)]
    kern = functools.partial(_decode_kernel, n_pg=n_pg, mode=mode, fox=fox is not None, diff=diff is not None,
                             lam_scale=lam_scale, skip_first=skip_first)
    return pl.pallas_call(
        kern,
        grid_spec=pltpu.PrefetchScalarGridSpec(
            num_scalar_prefetch=1, grid=(s_n,), in_specs=in_specs,
            out_specs=pl.BlockSpec((None, 1, BR_WIDTH), lambda b, pt: (b, 0, 0))),
        out_shape=jax.ShapeDtypeStruct((s_n, 1, BR_WIDTH), F32),
        compiler_params=_cparams(("parallel",)),
        name="decode_" + mode + ("_fox" if fox is not None else "") + ("_diff" if diff is not None else "")
             + ("_dense" if dense else ""),
    )(pt, *args)


def _decode_cmp_kernel(q_ref, kc_ref, vc_ref, map_ref, qm_ref, om_ref, oc_ref, sel_ref, *, n_cmp, own):
    qrows = q_ref[...] * qm_ref[...]
    s = _dot_nt(qrows, kc_ref[...].astype(BF16))
    visible = lax.broadcasted_iota(jnp.int32, s.shape, 1) < n_cmp
    s = jnp.where(visible, s, NEG)
    e = jnp.where(visible, jnp.exp(s - jnp.max(s, axis=-1, keepdims=True)), 0.0)
    p = e / jnp.maximum(jnp.sum(e, axis=-1, keepdims=True), 1e-30)
    p_hi = p.astype(BF16)
    p_lo = (p - p_hi.astype(F32)).astype(BF16)
    o = jnp.dot(p_hi, vc_ref[...].astype(BF16), preferred_element_type=F32)
    oc_ref[...] = jnp.sum(o * om_ref[...], axis=0, keepdims=True)
    rel = (jnp.dot(p_hi, map_ref[...], preferred_element_type=F32)
           + jnp.dot(p_lo, map_ref[...], preferred_element_type=F32))
    lane = lax.broadcasted_iota(jnp.int32, rel.shape, 1)
    picked = _select_bias(rel, lane, jnp.full((8, 1), own, jnp.int32), NSA_TOPK)
    sel_ref[...] = jnp.where(jnp.logical_and(picked == 0.0, lane < own), 1.0, 0.0)


def _decode_cmp(att3, cmp_kv, n_cmp, own):
    s_n, n_pad, _ = cmp_kv.shape
    assert own <= 128
    start = np.arange(n_pad)[:, None] * NSA_CMP_STRIDE
    blk = np.arange(128)[None, :] * NSA_SLC_BLOCK
    ov = np.minimum(start + NSA_CMP_LEN, blk + NSA_SLC_BLOCK) - np.maximum(start, blk)
    cmap = np.maximum(ov, 0).astype(np.float32) / NSA_CMP_LEN
    cmap[n_cmp:] = 0.0
    qm, om = _decode_row_masks(False)
    return pl.pallas_call(
        functools.partial(_decode_cmp_kernel, n_cmp=n_cmp, own=own),
        grid=(s_n,),
        in_specs=[pl.BlockSpec((None, 1, BR_WIDTH), lambda b: (b, 0, G_QDN)),
                  pl.BlockSpec((None, n_pad, BR_WIDTH), lambda b: (b, 0, 0)),
                  pl.BlockSpec((None, n_pad, BR_WIDTH), lambda b: (b, 0, 1)),
                  _const_spec((n_pad, 128)), _const_spec((8, BR_WIDTH)), _const_spec((8, BR_WIDTH))],
        out_specs=(pl.BlockSpec((None, 1, BR_WIDTH), lambda b: (b, 0, 0)),
                   pl.BlockSpec((None, 8, 128), lambda b: (b, 0, 0))),
        out_shape=(jax.ShapeDtypeStruct((s_n, 1, BR_WIDTH), F32), jax.ShapeDtypeStruct((s_n, 8, 128), F32)),
        compiler_params=_cparams(("parallel",)),
        name="decode_cmp",
    )(att3, cmp_kv, cmp_kv, jnp.asarray(cmap, BF16), qm, om)


def _sample_attention(att, kvs, small, caches_t, logf_t, win_t, cmp_rows, page_table, li, diff, cmpw, past_len):
    s_n = att.shape[0]
    kva, kvb, kvc, kvds, kvdw = (a.reshape(s_n, 1, 2 * BR_WIDTH) for a in kvs)
    att3 = att.reshape(s_n, 1, -1)
    ca, cb, cc, cds = caches_t
    dec = functools.partial(_decode_attn, att3, li=li, page_table=page_table)
    o_a = dec(G_QA, kva, ca, fox=(logf_t, small.reshape(s_n, 1, 128)))
    o_b = dec(G_QB, kvb, cb, diff=diff)
    o_c = dec(G_QC, kvc, cc, mode="moba")
    pos_emb, w1, b1, w2, b2, k_gain = cmpw
    m = past_len // NSA_CMP_STRIDE
    tok = _compress(_chunk_rows(cmp_rows), pos_emb, w1, b1, w2, b2, k_gain, chunks_per_seq=m,
                    seqs_per_tile=max(1, 512 // m))
    n_cmp = (past_len + 1 - NSA_CMP_LEN) // NSA_CMP_STRIDE + 1
    o_dc, sel = _decode_cmp(att3, _unchunk_tokens(tok, s_n, m), n_cmp, past_len // NSA_SLC_BLOCK)
    o_ds = dec(G_QD, kvds, cds, mode="sel", sel=sel, blk=NSA_SLC_BLOCK)
    buf_len = win_t.shape[-1]
    o_dw = dec(G_QD, kvdw, win_t, dense=True, skip_first=max(0, buf_len - NSA_WINDOW + 1))
    flat = lambda a, dt: a.reshape(s_n, BR_WIDTH).astype(dt)
    return flat(o_a, BF16), flat(o_b, BF16), flat(o_c, BF16), flat(o_dc, F32), flat(o_ds, F32), flat(o_dw, F32)


def _layer_gains(fox_qnorm, fox_knorm, diff_qnorm, diff_knorm, moba_qnorm, moba_knorm, nsa_qnorm, nsa_knorm):
    return jnp.stack([_tile_gain(fox_qnorm, HEAD_DIM), _tile_gain(fox_knorm, HEAD_DIM),
                      _tile_gain(diff_qnorm, DIFF_DIM), _tile_gain(diff_knorm, DIFF_DIM),
                      _tile_gain(moba_qnorm, HEAD_DIM), _tile_gain(moba_knorm, HEAD_DIM),
                      _tile_gain(nsa_qnorm, HEAD_DIM), _tile_gain(nsa_knorm[1], HEAD_DIM),
                      _tile_gain(nsa_knorm[2], HEAD_DIM)])


def _chunk_rows(kv5):
    b, l = kv5.shape[:2]
    m = l // NSA_CMP_STRIDE
    c = kv5.reshape(b, m, NSA_CMP_STRIDE, 2, N_HEADS, HEAD_DIM)
    return jnp.transpose(c, (3, 0, 4, 1, 2, 5)).reshape(2, b * N_HEADS * m, NSA_CMP_STRIDE * HEAD_DIM)


def _unchunk_tokens(tok, b, m):
    return jnp.transpose(tok.reshape(2, b, N_HEADS, m, HEAD_DIM), (1, 3, 0, 2, 4)).reshape(b, m, 2 * BR_WIDTH)


def _prompt_attention(att, kvc, kvdc, small, diff, cmpw, batch, seq, *, tq=256, tk=512):
    fl = functools.partial(_flash, batch=batch, seq=seq, tq=tq, tk=tk)
    ft8 = _cumsum_t(small, batch, seq)
    ft = jnp.pad(ft8[:, :N_HEADS].reshape(batch, 2, 2, seq), ((0, 0), (0, 0), (0, 6), (0, 0)))
    o_a = fl(att, att, att, q_col=G_QA, k_col=G_KA, v_col=G_VA, fox_t=ft)
    o_b = fl(att, att, att, q_col=G_QB, k_col=G_KB, v_col=G_VB, diff=diff)
    nb = seq // MOBA_BLOCK
    kmean = _blockmean(kvc, batch, seq, MOBA_BLOCK, -(-nb // 8) * 8)
    qa_c, ka_c = _moba_router(att, kmean, batch, seq, tq=tq)
    o_c = fl(qa_c, ka_c, att, v_col=G_VC, aug=True)
    m = seq // NSA_CMP_STRIDE
    pos_emb, w1, b1, w2, b2, k_gain = cmpw
    tok = _compress(_chunk_rows(kvdc.reshape(batch, seq, 2, N_HEADS, HEAD_DIM)), pos_emb, w1, b1, w2, b2, k_gain,
                    chunks_per_seq=m, seqs_per_tile=max(1, 512 // m))
    cmp_kv = _unchunk_tokens(tok, batch, m)
    o_dc, qa_d, ka_d = _nsa_cmp(att, cmp_kv, batch, seq, m - 1, tq=tq)
    o_ds = fl(qa_d, ka_d, att, v_col=G_VDS, aug=True, out_dtype=F32)
    o_dw = fl(att, att, att, q_col=G_QD, k_col=G_KDW, v_col=G_VDW, window=NSA_WINDOW, out_dtype=F32)
    return o_a, o_b, o_c, o_dc, o_ds, o_dw


def _trunk_tail(x2d, hb, branch, small, g1, sc2, sh2, g2, gn2, w_branch, w_gate, b_gate, w_out, ffn,
                *, rows_per_mod, tm=512):
    x2d, h2 = _merge(x2d, hb, *branch, small, g1, sc2, sh2, gn2, w_branch, w_gate, b_gate, w_out,
                     rows_per_mod=rows_per_mod, tm=tm)
    if ffn[0] == "dense":
        return _ffn(x2d, h2, g2, *ffn[1:], rows_per_mod=rows_per_mod, tm=tm)
    return _moe(x2d, h2, g2, *ffn[1:], rows_per_mod=rows_per_mod, tm=tm)


def kernel(x_prompt, x_sample, c_prompt, c_sample, cache_a_kv, cache_a_logf, cache_b_kv, cache_c_kv, cache_d_cmp_kv, cache_d_slc_kv, state_d_win_kv, page_table, ada_w, ada_b, norm_attn, norm_ffn, w_in, fox_fbias, fox_qnorm, fox_knorm, diff_qnorm, diff_knorm, diff_lambda, diff_subnorm, moba_qnorm, moba_knorm, nsa_qnorm, nsa_knorm, nsa_cmp_pos, nsa_cmp_w1, nsa_cmp_b1, nsa_cmp_w2, nsa_cmp_b2, w_branch, w_gate, b_gate, w_out, ffn_w_gate, ffn_w_up, ffn_w_down, moe_router, moe_router_b, moe_w_gate, moe_w_up, moe_w_down):
    bp, t, d = x_prompt.shape
    bs = x_sample.shape[0]
    depth = w_in.shape[0]
    n_pages = page_table.shape[1]
    past_len = n_pages * cache_a_kv.shape[2]
    pos_p = jnp.arange(t, dtype=jnp.int32)
    pos_s = jnp.full((1,), past_len, jnp.int32)
    xp = x_prompt.reshape(bp * t, d)
    xs = x_sample.reshape(bs, d)
    c_all = jnp.concatenate([c_prompt, c_sample, jnp.zeros((-(bp + bs) % 8, d), F32)], axis=0)
    page_t = lambda c: jnp.transpose(c, (0, 1, 3, 4, 5, 2)).reshape(c.shape[:2] + (2, BR_WIDTH, c.shape[2]))
    caches_t = tuple(page_t(c) for c in (cache_a_kv, cache_b_kv, cache_c_kv, cache_d_slc_kv))
    logf_t = jnp.swapaxes(cache_a_logf, 2, 3)
    win_t = page_t(state_d_win_kv)
    st_p = [[] for _ in range(7)]
    st_s = [[] for _ in range(7)]
    for li in range(depth):
        lam_init = 0.8 - 0.6 * math.exp(-0.3 * li)
        lq = diff_lambda[li].astype(F32)
        lam = jnp.exp(jnp.sum(lq[0] * lq[1])) - jnp.exp(jnp.sum(lq[2] * lq[3])) + lam_init
        diff = (lam, 1.0 - lam_init, jnp.tile(diff_subnorm[li].astype(F32), 2))
        cmpw = (nsa_cmp_pos[li], nsa_cmp_w1[li], nsa_cmp_b1[li], nsa_cmp_w2[li], nsa_cmp_b2[li], nsa_knorm[li][0])
        mod = _rowmm(c_all, ada_w[li], ada_b[li], silu_in=True)
        modp = mod[:bp].reshape(bp, 6, d)
        wm, ws, bsm = _split_w_in(w_in[li], fox_fbias[li])
        gains = _layer_gains(fox_qnorm[li], fox_knorm[li], diff_qnorm[li], diff_knorm[li], moba_qnorm[li],
                             moba_knorm[li], nsa_qnorm[li], nsa_knorm[li])
        hb, att, kva, kvb, kvc, kvdc, kvds, kvdw, small = _inproj(
            xp, modp[:, 1], modp[:, 0], norm_attn[li], wm, ws, bsm, gains, pos_p, rows_per_mod=t, tm=512)
        branch = _prompt_attention(att, kvc, kvdc, small, diff, cmpw, bp, t)
        j = li // 2
        if li % 2 == 0:
            ffn = ("dense", ffn_w_gate[j], ffn_w_up[j], ffn_w_down[j])
        else:
            ffn = ("moe", moe_router[j], moe_router_b[j], moe_w_gate[j], moe_w_up[j], moe_w_down[j])
        xp = _trunk_tail(xp, hb, branch, small, modp[:, 2], modp[:, 4], modp[:, 3], modp[:, 5], norm_ffn[li],
                         w_branch[li], w_gate[li], b_gate[li], w_out[li], ffn, rows_per_mod=t)
        kv6 = lambda a: a.reshape(bp, t, 2, N_HEADS, HEAD_DIM)
        win = min(NSA_WINDOW, t)
        for lst, val in zip(st_p, (kv6(kva), small[:, :N_HEADS].reshape(bp, t, N_HEADS), kv6(kvb), kv6(kvc),
                                   kv6(kvdc), kv6(kvds), kv6(kvdw)[:, t - win:])):
            lst.append(val)
        mods = mod[bp:bp + bs].reshape(bs, 6, d)
        hb, att, kva, kvb, kvc, kvdc, kvds, kvdw, small = _inproj(
            xs, mods[:, 1], mods[:, 0], norm_attn[li], wm, ws, bsm, gains, pos_s, rows_per_mod=1, tm=bs)
        cmp_rows = cache_d_cmp_kv[li][page_table].reshape((bs, past_len) + cache_d_cmp_kv.shape[3:])
        diff_s = (lam, 1.0 - lam_init, jnp.tile(diff_subnorm[li].astype(F32), N_HEADS))
        branch = _sample_attention(att, (kva, kvb, kvc, kvds, kvdw), small, caches_t, logf_t, win_t, cmp_rows,
                                   page_table, li, diff_s, cmpw, past_len)
        xs = _trunk_tail(xs, hb, branch, small, mods[:, 2], mods[:, 4], mods[:, 3], mods[:, 5], norm_ffn[li],
                         w_branch[li], w_gate[li], b_gate[li], w_out[li], ffn, rows_per_mod=1)
        kv6 = lambda a: a.reshape(bs, 1, 2, N_HEADS, HEAD_DIM)
        new_win = jnp.concatenate([state_d_win_kv[li][:, 1:], kv6(kvdw)], axis=1)
        for lst, val in zip(st_s, (kv6(kva), small[:, :N_HEADS].reshape(bs, 1, N_HEADS), kv6(kvb), kv6(kvc),
                                   kv6(kvdc), kv6(kvds), new_win)):
            lst.append(val)
    stack = lambda st: tuple(jnp.stack(lst, axis=0) for lst in st)
    return (xp.reshape(bp, t, d), xs.reshape(bs, 1, d)) + stack(st_p) + stack(st_s)
```

```python
import functools
import math

import jax
import jax.numpy as jnp
import numpy as np
from jax import lax
from jax.experimental import pallas as pl
from jax.experimental.pallas import tpu as pltpu

F32 = jnp.float32
BF16 = jnp.bfloat16

N_HEADS = 4
HEAD_DIM = 64
BR_WIDTH = N_HEADS * HEAD_DIM
PAIR_LANES = 2 * HEAD_DIM
DIFF_DIM = HEAD_DIM // 2
ROPE_THETA = 500000.0
MOBA_BLOCK = 256
MOBA_TOPK = 3
NSA_CMP_LEN = 32
NSA_CMP_STRIDE = 16
NSA_SLC_BLOCK = 64
NSA_TOPK = 4
NSA_WINDOW = 512
N_EXPERTS = 8
EPS = 1e-6
NEG = -1e30
LOG2E = math.log2(math.e)
V7X_VMEM_BYTES = 64 * 1024 * 1024
VMEM_LIMIT = V7X_VMEM_BYTES - 16 * 1024 * 1024

(G_QA, G_KA, G_VA, G_QB, G_KB, G_VB, G_QC, G_KC, G_VC,
 G_QDN, G_QD, G_KDS, G_VDS, G_KDW, G_VDW) = range(15)
N_ATT_GROUPS = 15


def _cparams(sem):
    return pltpu.CompilerParams(dimension_semantics=sem, vmem_limit_bytes=VMEM_LIMIT)


def _const_spec(shape):
    nd = len(shape)
    return pl.BlockSpec(shape, lambda *_: (0,) * nd)


def _rowmm_kernel(x_ref, w_ref, b_ref, o_ref, *, silu_in):
    x = x_ref[...]
    if silu_in:
        x = x * jax.nn.sigmoid(x)
    o_ref[...] = jnp.dot(x.astype(BF16), w_ref[...].astype(BF16),
                         preferred_element_type=F32) + b_ref[...]


def _rowmm(x, w, b, *, silu_in=False, tn=1024):
    m, k = x.shape
    n = w.shape[1]
    tn = math.gcd(tn, n)
    return pl.pallas_call(
        functools.partial(_rowmm_kernel, silu_in=silu_in),
        grid=(n // tn,),
        in_specs=[pl.BlockSpec((m, k), lambda j: (0, 0)),
                  pl.BlockSpec((k, tn), lambda j: (0, j)),
                  pl.BlockSpec((1, tn), lambda j: (0, j))],
        out_specs=pl.BlockSpec((m, tn), lambda j: (0, j)),
        out_shape=jax.ShapeDtypeStruct((m, n), F32),
        compiler_params=_cparams(("parallel",)),
        name="rowmm",
    )(x, w, b.reshape(1, n))


def _rope_tables(pos, group):
    r = group // 4
    half = r // 2
    inv = ROPE_THETA ** (-(np.arange(half, dtype=np.float32) / half))
    lane = np.arange(BR_WIDTH)
    j = lane % group
    ang = pos.astype(F32)[:, None] * jnp.asarray(inv[j % half], F32)[None, :]
    cos = jnp.where(j[None, :] < r, jnp.cos(ang), 1.0)
    sin = jnp.sin(ang)
    sin_a = jnp.where(j[None, :] < half, -sin, 0.0)
    sin_b = jnp.where((j[None, :] >= half) & (j[None, :] < r), sin, 0.0)
    return cos.astype(F32), sin_a.astype(F32), sin_b.astype(F32)


def _group_mean_matrix(width, group):
    i = np.arange(width)
    return jnp.asarray((i[:, None] // group == i[None, :] // group).astype(np.float32) / group, BF16)


def _group_rms(a, bd, gain):
    ms = jnp.dot((a * a).astype(BF16), bd, preferred_element_type=F32)
    return a * lax.rsqrt(ms + EPS) * gain


def _rope(a, cos, sin_a, sin_b, half):
    w = a.shape[-1]
    return a * cos + pltpu.roll(a, w - half, 1) * sin_a + pltpu.roll(a, half, 1) * sin_b


def _log_sigmoid(x):
    return jnp.minimum(x, 0.0) - jnp.log1p(jnp.exp(-jnp.abs(x)))


def _inproj_kernel(x_ref, sc_ref, sh_ref, gn_ref, w_ref, ws_ref, bs_ref, gains_ref,
                   bd64_ref, bd32_ref, c64_ref, sa64_ref, sb64_ref, c32_ref, sa32_ref, sb32_ref,
                   h_ref, att_ref, kva_ref, kvb_ref, kvc_ref, kvdc_ref, kvds_ref, kvdw_ref, small_ref):
    x = x_ref[...]
    ms = jnp.mean(x * x, axis=-1, keepdims=True)
    h = (x * lax.rsqrt(ms + EPS) * gn_ref[...]) * (1.0 + sc_ref[...]) + sh_ref[...]
    hb = h.astype(BF16)
    h_ref[...] = hb
    bd64 = bd64_ref[...]
    bd32 = bd32_ref[...]

    def proj(g):
        return jnp.dot(hb, w_ref[:, g * BR_WIDTH:(g + 1) * BR_WIDTH], preferred_element_type=F32)

    def gain(i):
        return gains_ref[i:i + 1, :]

    def rope64(a):
        return _rope(a, c64_ref[...], sa64_ref[...], sb64_ref[...], 8)

    def rope32(a):
        return _rope(a, c32_ref[...], sa32_ref[...], sb32_ref[...], 4)

    def att(g, a):
        att_ref[:, g * BR_WIDTH:(g + 1) * BR_WIDTH] = a.astype(BF16)

    sm_scale = HEAD_DIM ** -0.5 * LOG2E
    att(G_QA, _group_rms(proj(0), bd64, gain(0)) * sm_scale)
    k = _group_rms(proj(1), bd64, gain(1))
    v = proj(2)
    kva_ref[:, :BR_WIDTH] = k
    kva_ref[:, BR_WIDTH:] = v
    att(G_KA, k)
    att(G_VA, v)
    att(G_QB, rope32(_group_rms(proj(3), bd32, gain(2))) * (DIFF_DIM ** -0.5 * LOG2E))
    k = rope32(_group_rms(proj(4), bd32, gain(3)))
    v = proj(5)
    kvb_ref[:, :BR_WIDTH] = k
    kvb_ref[:, BR_WIDTH:] = v
    att(G_KB, k)
    att(G_VB, v)
    att(G_QC, rope64(_group_rms(proj(6), bd64, gain(4))) * sm_scale)
    k = rope64(_group_rms(proj(7), bd64, gain(5)))
    v = proj(8)
    kvc_ref[:, :BR_WIDTH] = k
    kvc_ref[:, BR_WIDTH:] = v
    att(G_KC, k)
    att(G_VC, v)
    qn = _group_rms(proj(9), bd64, gain(6))
    att(G_QDN, qn * sm_scale)
    att(G_QD, rope64(qn) * sm_scale)
    kvdc_ref[:, :BR_WIDTH] = proj(10)
    kvdc_ref[:, BR_WIDTH:] = proj(11)
    k = rope64(_group_rms(proj(12), bd64, gain(7)))
    v = proj(13)
    kvds_ref[:, :BR_WIDTH] = k
    kvds_ref[:, BR_WIDTH:] = v
    att(G_KDS, k)
    att(G_VDS, v)
    k = rope64(_group_rms(proj(14), bd64, gain(8)))
    v = proj(15)
    kvdw_ref[:, :BR_WIDTH] = k
    kvdw_ref[:, BR_WIDTH:] = v
    att(G_KDW, k)
    att(G_VDW, v)
    z = jnp.dot(hb, ws_ref[...], preferred_element_type=F32) + bs_ref[...]
    lane = lax.broadcasted_iota(jnp.int32, z.shape, 1)
    small_ref[...] = jnp.where(lane < N_HEADS, _log_sigmoid(z), jax.nn.sigmoid(z))


def _split_w_in(w_in, fox_fbias):
    d = w_in.shape[0]
    c0 = 3 * BR_WIDTH
    main = jnp.concatenate([w_in[:, :c0], w_in[:, c0 + N_HEADS:c0 + N_HEADS + 13 * BR_WIDTH]], axis=1)
    small = jnp.concatenate([w_in[:, c0:c0 + N_HEADS], w_in[:, c0 + N_HEADS + 13 * BR_WIDTH:],
                             jnp.zeros((d, 128 - 4 * N_HEADS), w_in.dtype)], axis=1)
    bias = jnp.concatenate([fox_fbias.astype(F32), jnp.zeros((128 - N_HEADS,), F32)]).reshape(1, 128)
    return main.astype(BF16), small.astype(BF16), bias


def _inproj(x2d, sc, sh, gn, w_main, w_small, b_small, gains, pos, *, rows_per_mod, tm):
    n, d = x2d.shape
    tm = min(tm, n)
    assert n % tm == 0
    per_tok = rows_per_mod == 1
    if not per_tok:
        assert rows_per_mod % tm == 0
    tabs = _rope_tables(pos, HEAD_DIM) + _rope_tables(pos, DIFF_DIM)
    single_pos = pos.shape[0] == 1
    tiles_per_seq = 1 if single_pos else pos.shape[0] // tm

    if per_tok:
        mod_spec = pl.BlockSpec((tm, d), lambda i: (i, 0))
    else:
        tpm = rows_per_mod // tm
        mod_spec = pl.BlockSpec((None, 1, d), lambda i: (i // tpm, 0, 0))
        sc, sh = sc.reshape(-1, 1, d), sh.reshape(-1, 1, d)
    if single_pos:
        tab_spec = pl.BlockSpec((1, BR_WIDTH), lambda i: (0, 0))
    else:
        tab_spec = pl.BlockSpec((tm, BR_WIDTH), lambda i: (i % tiles_per_seq, 0))

    row = lambda w: pl.BlockSpec((tm, w), lambda i: (i, 0))
    kv_shape = jax.ShapeDtypeStruct((n, 2 * BR_WIDTH), F32)
    out_shape = (jax.ShapeDtypeStruct((n, d), BF16),
                 jax.ShapeDtypeStruct((n, N_ATT_GROUPS * BR_WIDTH), BF16),
                 kv_shape, kv_shape, kv_shape, kv_shape, kv_shape, kv_shape,
                 jax.ShapeDtypeStruct((n, 128), F32))
    out_specs = (row(d), row(N_ATT_GROUPS * BR_WIDTH)) + (row(2 * BR_WIDTH),) * 6 + (row(128),)
    return pl.pallas_call(
        _inproj_kernel,
        grid=(n // tm,),
        in_specs=[row(d), mod_spec, mod_spec, _const_spec((1, d)),
                  _const_spec(w_main.shape), _const_spec(w_small.shape), _const_spec((1, 128)),
                  _const_spec(gains.shape),
                  _const_spec((BR_WIDTH, BR_WIDTH)), _const_spec((BR_WIDTH, BR_WIDTH))] + [tab_spec] * 6,
        out_specs=out_specs,
        out_shape=out_shape,
        compiler_params=_cparams(("parallel",)),
        name="inproj",
    )(x2d, sc, sh, gn.reshape(1, d), w_main, w_small, b_small, gains,
      _group_mean_matrix(BR_WIDTH, HEAD_DIM), _group_mean_matrix(BR_WIDTH, DIFF_DIM), *tabs)


def _tile_gain(g, group):
    return jnp.tile(g.astype(F32), BR_WIDTH // group)


def _cumsum_kernel(x_ref, tri_ref, ft_ref, carry_ref):
    @pl.when(pl.program_id(1) == 0)
    def _():
        carry_ref[...] = jnp.zeros_like(carry_ref)

    c = jnp.dot(tri_ref[...], x_ref[...], precision=lax.Precision.HIGHEST,
                preferred_element_type=F32) + carry_ref[...]
    carry_ref[...] = c[-1:, :]
    ft_ref[...] = jnp.transpose(c)[:8, :]


def _cumsum_t(small, batch, seq, *, tc=256):
    tc = min(tc, seq)
    nt = seq // tc
    tri = jnp.asarray(np.tril(np.ones((tc, tc), np.float32)))
    return pl.pallas_call(
        _cumsum_kernel,
        grid=(batch, nt),
        in_specs=[pl.BlockSpec((tc, 128), lambda b, t: (b * nt + t, 0)), _const_spec((tc, tc))],
        out_specs=pl.BlockSpec((None, 8, tc), lambda b, t: (b, 0, t)),
        out_shape=jax.ShapeDtypeStruct((batch, 8, seq), F32),
        scratch_shapes=[pltpu.VMEM((1, 128), F32)],
        compiler_params=_cparams(("parallel", "arbitrary")),
        name="cumsum_t",
    )(small, tri)


def _flash_steps(seq, tq, tk, window):
    assert tk % tq == 0 and seq % tk == 0
    qi_l, kj_l, fl_l = [], [], []
    for qi in range(seq // tq):
        t0, t1 = qi * tq, (qi + 1) * tq - 1
        js = []
        for j in range(t0 // tk, -1, -1):
            s0, s1 = j * tk, (j + 1) * tk - 1
            if window is not None and s1 <= t0 - window:
                break
            partial = s1 > t0 or (window is not None and s0 <= t1 - window)
            js.append((j, partial))
        for n, (j, partial) in enumerate(js):
            qi_l.append(qi)
            kj_l.append(j)
            fl_l.append((1 if n == 0 else 0) | (2 if n == len(js) - 1 else 0) | (4 if partial else 0))
    return (jnp.asarray(qi_l, jnp.int32), jnp.asarray(kj_l, jnp.int32), jnp.asarray(fl_l, jnp.int32))


def _dot_nt(a, b):
    return lax.dot_general(a, b, (((1,), (1,)), ((), ())), preferred_element_type=F32)


def _flash_kernel(qi_ref, kj_ref, fl_ref, *refs, aug, groups, tq, tk, fox, window, combine, lam_scale):
    refs = list(refs)
    q_ref, k_ref, v_ref = refs[:3]
    refs = refs[3:]
    mask_ref = None if aug else refs.pop(0)
    ft_ref = refs.pop(0) if fox else None
    if combine == "diff":
        lam_ref, bd_ref, gsub_ref = refs[:3]
        refs = refs[3:]
    o_ref, m_ref, acc_ref, qs_ref = refs

    step = pl.program_id(1)
    fl = fl_ref[step]
    qi = qi_ref[step]
    kj = kj_ref[step]
    pair = lambda ref, p: ref[:, p * PAIR_LANES:(p + 1) * PAIR_LANES]

    @pl.when((fl & 1) != 0)
    def _():
        m_ref[...] = jnp.full(m_ref.shape, NEG, F32)
        acc_ref[...] = jnp.zeros(acc_ref.shape, F32)
        if not aug:
            for p in range(2):
                q = pair(q_ref, p)
                for r in range(groups):
                    qs_ref[p * groups + r] = q * mask_ref[r:r + 1, :]

    def body(apply_mask):
        if apply_mask:
            row = qi * tq + lax.broadcasted_iota(jnp.int32, (tq, tk), 0)
            col = kj * tk + lax.broadcasted_iota(jnp.int32, (tq, tk), 1)
            valid = col <= row
            if window is not None:
                valid = jnp.logical_and(valid, col > row - window)
        if fox:
            ft = ft_ref[...] * LOG2E
        for p in range(2):
            v1 = jnp.concatenate([pair(v_ref, p), jnp.ones((tk, PAIR_LANES), BF16)], axis=1)
            for r in range(groups):
                g = p * groups + r
                if aug:
                    s = _dot_nt(q_ref[g], k_ref[g])
                else:
                    s = _dot_nt(qs_ref[g], pair(k_ref, p))
                if fox:
                    s = s - ft[g:g + 1, :]
                if apply_mask:
                    s = jnp.where(valid, s, NEG)
                m_prev = m_ref[g]
                m_new = jnp.maximum(m_prev, jnp.max(s, axis=-1, keepdims=True))
                alpha = jnp.exp2(m_prev - m_new)
                pexp = jnp.exp2(s - m_new)
                acc_ref[g] = alpha * acc_ref[g] + jnp.dot(pexp.astype(BF16), v1, preferred_element_type=F32)
                m_ref[g] = m_new

    masked = (fl & 4) != 0
    pl.when(masked)(lambda: body(True))
    pl.when(jnp.logical_not(masked))(lambda: body(False))

    @pl.when((fl & 2) != 0)
    def _():
        lane = lax.broadcasted_iota(jnp.int32, (tq, PAIR_LANES), 1)
        for p in range(2):
            outs = []
            for r in range(groups):
                a = acc_ref[p * groups + r]
                outs.append(a[:, :PAIR_LANES] / a[:, PAIR_LANES:])
            if combine == "diff":
                lam = lam_ref[0]
                o = jnp.where(lane < HEAD_DIM, outs[0] - lam * outs[1], outs[2] - lam * outs[3])
                o = _group_rms(o, bd_ref[...], gsub_ref[...]) * lam_scale
            else:
                o = jnp.where(lane < HEAD_DIM, outs[0], outs[1])
            o_ref[:, p * PAIR_LANES:(p + 1) * PAIR_LANES] = o.astype(o_ref.dtype)


def _lane_masks(bounds):
    lane = np.arange(PAIR_LANES)
    return jnp.asarray(np.stack([(lane >= a) & (lane < b) for a, b in bounds]).astype(np.float32), BF16)


PAIR_MASKS = ((0, 64), (64, 128))
DIFF_MASKS = ((0, 32), (32, 64), (64, 96), (96, 128))


def _flash(q_src, k_src, v_src, *, batch, seq, tq, tk, q_col=None, k_col=None, v_col, aug=False,
           fox_t=None, window=None, diff=None, out_dtype=BF16):
    tq, tk = min(tq, seq), min(tk, seq)
    nq, nk = seq // tq, seq // tk
    qi_t, kj_t, fl_t = _flash_steps(seq, tq, tk, window)
    n_steps = int(qi_t.shape[0]) // 1
    groups = 4 if diff is not None else 2
    combine = "diff" if diff is not None else "pair"

    def tok(use_q, col):
        tile, n_t = (tq, nq) if use_q else (tk, nk)
        return pl.BlockSpec((tile, BR_WIDTH),
                            lambda b, s, qi, kj, fl: (b * n_t + (qi if use_q else kj)[s], col))

    in_specs, args = [], []
    if aug:
        in_specs += [pl.BlockSpec((None, N_HEADS, tq, PAIR_LANES), lambda b, s, qi, kj, fl: (b, 0, qi[s], 0)),
                     pl.BlockSpec((None, N_HEADS, tk, PAIR_LANES), lambda b, s, qi, kj, fl: (b, 0, kj[s], 0))]
    else:
        in_specs += [tok(True, q_col), tok(False, k_col)]
    in_specs.append(tok(False, v_col))
    args += [q_src, k_src, v_src]
    if not aug:
        in_specs.append(_const_spec((groups, PAIR_LANES)))
        args.append(_lane_masks(DIFF_MASKS if diff is not None else PAIR_MASKS))
    if fox_t is not None:
        in_specs.append(pl.BlockSpec((None, 8, tk), lambda b, s, qi, kj, fl: (b, 0, kj[s])))
        args.append(fox_t)
    lam_scale = 1.0
    if diff is not None:
        lam, lam_scale, gsub = diff
        in_specs += [pl.BlockSpec(memory_space=pltpu.SMEM), _const_spec((PAIR_LANES, PAIR_LANES)),
                     _const_spec((1, PAIR_LANES))]
        args += [lam.reshape(1).astype(F32), _group_mean_matrix(PAIR_LANES, HEAD_DIM), gsub.reshape(1, PAIR_LANES)]

    kern = functools.partial(_flash_kernel, aug=aug, groups=groups, tq=tq, tk=tk, fox=fox_t is not None,
                             window=window, combine=combine, lam_scale=lam_scale)
    return pl.pallas_call(
        kern,
        grid_spec=pltpu.PrefetchScalarGridSpec(
            num_scalar_prefetch=3,
            grid=(batch, n_steps),
            in_specs=in_specs,
            out_specs=pl.BlockSpec((tq, BR_WIDTH), lambda b, s, qi, kj, fl: (b * nq + qi[s], 0)),
            scratch_shapes=[pltpu.VMEM((2 * groups, tq, 1), F32),
                            pltpu.VMEM((2 * groups, tq, 2 * PAIR_LANES), F32),
                            pltpu.VMEM((2 * groups, tq, PAIR_LANES), BF16)]),
        out_shape=jax.ShapeDtypeStruct((batch * seq, BR_WIDTH), out_dtype),
        compiler_params=_cparams(("parallel", "arbitrary")),
        name="flash_" + ("aug" if aug else combine) + ("_fox" if fox_t is not None else "")
             + ("_win" if window is not None else ""),
    )(qi_t, kj_t, fl_t, *args)


def _select_bias(rel, j, own, topk):
    big = jnp.int32(1 << 20)
    avail = jnp.logical_and(j >= 0, j < own)
    sel = j == own
    for _ in range(topk):
        val = jnp.where(avail, rel, NEG)
        mx = jnp.max(val, axis=-1, keepdims=True)
        cand = jnp.logical_and(avail, val == mx)
        idx = jnp.min(jnp.where(cand, j, big), axis=-1, keepdims=True)
        pick = j == idx
        sel = jnp.logical_or(sel, pick)
        avail = jnp.logical_and(avail, jnp.logical_not(pick))
    return jnp.where(sel, 0.0, NEG)


def _blockmean_kernel(x_ref, o_ref):
    j = pl.program_id(1)
    o_ref[pl.ds(j, 1), :] = jnp.mean(x_ref[...], axis=0, keepdims=True)


def _blockmean(kv, batch, seq, blk, nb_pad):
    nb = seq // blk
    out = pl.pallas_call(
        _blockmean_kernel,
        grid=(batch, nb),
        in_specs=[pl.BlockSpec((blk, BR_WIDTH), lambda b, j: (b * nb + j, 0))],
        out_specs=pl.BlockSpec((None, nb, BR_WIDTH), lambda b, j: (b, 0, 0)),
        out_shape=jax.ShapeDtypeStruct((batch, nb, BR_WIDTH), F32),
        compiler_params=_cparams(("parallel", "arbitrary")),
        name="blockmean",
    )(kv)
    return jnp.pad(out, ((0, 0), (0, nb_pad - nb), (0, 0)))


def _augment(x, bias_or_onehot, r):
    lane = lax.broadcasted_iota(jnp.int32, x.shape, 1)
    mine = (lane < HEAD_DIM) if r == 0 else (lane >= HEAD_DIM)
    return jnp.where(mine, x.astype(F32), bias_or_onehot).astype(BF16)


def _payload_block_id(shape, r):
    lane = lax.broadcasted_iota(jnp.int32, shape, 1)
    return (lane - HEAD_DIM) if r == 0 else jnp.where(lane < HEAD_DIM, lane, -1)


def _moba_router_kernel(q_ref, k_ref, km_ref, mask_ref, qa_ref, ka_ref, *, tq, blk, nb_pad):
    qi = pl.program_id(2)
    q = q_ref[...]
    k = k_ref[...]
    row = qi * tq + lax.broadcasted_iota(jnp.int32, (tq, 1), 0)
    own = row // blk
    for r in range(2):
        km = (km_ref[...] * mask_ref[r:r + 1, :].astype(F32)).astype(BF16)
        lo = HEAD_DIM if r == 0 else 0
        pieces = [km, jnp.zeros((PAIR_LANES - nb_pad, PAIR_LANES), BF16)]
        if lo:
            pieces = [jnp.zeros((lo, PAIR_LANES), BF16), km, jnp.zeros((PAIR_LANES - lo - nb_pad, PAIR_LANES), BF16)]
        rel = _dot_nt(q * mask_ref[r:r + 1, :], jnp.concatenate(pieces, axis=0))
        j = _payload_block_id((tq, PAIR_LANES), r)
        bias = _select_bias(rel, j, own, MOBA_TOPK)
        qa_ref[r] = _augment(q, bias, r)
        ka_ref[r] = _augment(k, (j == own).astype(F32), r)


def _moba_router(att, kmean, batch, seq, *, tq):
    tq = min(tq, seq)
    nq = seq // tq
    nb_pad = kmean.shape[1]
    out = jax.ShapeDtypeStruct((batch, N_HEADS, seq, PAIR_LANES), BF16)
    aug_spec = pl.BlockSpec((None, 2, tq, PAIR_LANES), lambda b, p, i: (b, p, i, 0))
    return pl.pallas_call(
        functools.partial(_moba_router_kernel, tq=tq, blk=MOBA_BLOCK, nb_pad=nb_pad),
        grid=(batch, 2, nq),
        in_specs=[pl.BlockSpec((tq, PAIR_LANES), lambda b, p, i: (b * nq + i, G_QC * 2 + p)),
                  pl.BlockSpec((tq, PAIR_LANES), lambda b, p, i: (b * nq + i, G_KC * 2 + p)),
                  pl.BlockSpec((None, nb_pad, PAIR_LANES), lambda b, p, i: (b, 0, p)),
                  _const_spec((2, PAIR_LANES))],
        out_specs=(aug_spec, aug_spec),
        out_shape=(out, out),
        compiler_params=_cparams(("parallel", "parallel", "parallel")),
        name="moba_router",
    )(att, att, kmean, _lane_masks(PAIR_MASKS))


def _compress_kernel(c_ref, w1_ref, b1_ref, w2_ref, b2_ref, gain_ref, o_ref, *, rows):
    kv = pl.program_id(0)
    uv = jnp.dot(c_ref[...].astype(BF16), w1_ref[...], preferred_element_type=F32)
    hid = uv.shape[1] // 2
    pre = uv[:, :hid] + pltpu.roll(uv[:, hid:], rows - 1, 0) + b1_ref[...]
    y = jnp.dot(jax.nn.gelu(pre, approximate=True).astype(BF16), w2_ref[...],
                preferred_element_type=F32) + b2_ref[...]
    ms = jnp.mean(y * y, axis=-1, keepdims=True)
    yn = y * lax.rsqrt(ms + EPS) * gain_ref[...]
    o_ref[...] = jnp.where(kv == 0, yn, y)


def _compress(chunks, pos_emb, w1, b1, w2, b2, k_gain, *, chunks_per_seq, seqs_per_tile):
    _, r, cw = chunks.shape
    hid = w1.shape[-1]
    rows = chunks_per_seq * seqs_per_tile
    assert r % rows == 0
    w1ab, bias1, w2, b2, gain = _compress_weights(pos_emb, w1, b1, w2, b2, k_gain)
    return pl.pallas_call(
        functools.partial(_compress_kernel, rows=rows),
        grid=(2, r // rows),
        in_specs=[pl.BlockSpec((None, rows, cw), lambda kv, i: (kv, i, 0)),
                  pl.BlockSpec((None, cw, 2 * hid), lambda kv, i: (kv, 0, 0)),
                  pl.BlockSpec((None, 1, hid), lambda kv, i: (kv, 0, 0)),
                  pl.BlockSpec((None, hid, HEAD_DIM), lambda kv, i: (kv, 0, 0)),
                  pl.BlockSpec((None, 1, HEAD_DIM), lambda kv, i: (kv, 0, 0)),
                  _const_spec((1, HEAD_DIM))],
        out_specs=pl.BlockSpec((None, rows, HEAD_DIM), lambda kv, i: (kv, i, 0)),
        out_shape=jax.ShapeDtypeStruct((2, r, HEAD_DIM), F32),
        compiler_params=_cparams(("parallel", "parallel")),
        name="nsa_compress",
    )(chunks, w1ab, bias1, w2, b2, gain)


def _compress_weights(pos_emb, w1, b1, w2, b2, k_gain):
    cw = w1.shape[1] // 2
    hid = w1.shape[-1]
    w1ab = jnp.concatenate([w1[:, :cw], w1[:, cw:]], axis=-1).astype(BF16)
    bias1 = (jnp.einsum('kf,kfh->kh', pos_emb.reshape(2, -1).astype(F32), w1.astype(F32),
                        precision=lax.Precision.HIGHEST) + b1.astype(F32)).reshape(2, 1, hid)
    return (w1ab, bias1, w2.astype(BF16), b2.astype(F32).reshape(2, 1, HEAD_DIM),
            k_gain.astype(F32).reshape(1, HEAD_DIM))


def _cmp_to_slc_map(n_cmp_pad, n_cmp):
    start = np.arange(n_cmp_pad)[:, None] * NSA_CMP_STRIDE
    blk = np.arange(HEAD_DIM)[None, :] * NSA_SLC_BLOCK
    ov = np.minimum(start + NSA_CMP_LEN, blk + NSA_SLC_BLOCK) - np.maximum(start, blk)
    m = np.maximum(ov, 0).astype(np.float32) / NSA_CMP_LEN
    m[n_cmp:] = 0.0
    z = np.zeros_like(m)
    return jnp.asarray(np.stack([np.concatenate([z, m], 1), np.concatenate([m, z], 1)]), BF16)


def _nsa_cmp_kernel(qn_ref, q_ref, k_ref, kc_ref, vc_ref, map_ref, mask_ref,
                    oc_ref, qa_ref, ka_ref, *, tq, n_pad):
    qi = pl.program_id(2)
    qn = qn_ref[...]
    q = q_ref[...]
    k = k_ref[...]
    kc = kc_ref[...].astype(BF16)
    vc = vc_ref[...].astype(BF16)
    row = qi * tq + lax.broadcasted_iota(jnp.int32, (tq, 1), 0)
    own = row // NSA_SLC_BLOCK
    cmp_end = lax.broadcasted_iota(jnp.int32, (tq, n_pad), 1) * NSA_CMP_STRIDE + (NSA_CMP_LEN - 1)
    visible = cmp_end <= row
    outs = []
    for r in range(2):
        s = jnp.where(visible, _dot_nt(qn * mask_ref[r:r + 1, :], kc), NEG)
        e = jnp.where(visible, jnp.exp2(s - jnp.max(s, axis=-1, keepdims=True)), 0.0)
        p = e / jnp.maximum(jnp.sum(e, axis=-1, keepdims=True), 1e-30)
        p_hi = p.astype(BF16)
        p_lo = (p - p_hi.astype(F32)).astype(BF16)
        outs.append(jnp.dot(p_hi, vc, preferred_element_type=F32))
        rel = (jnp.dot(p_hi, map_ref[r], preferred_element_type=F32)
               + jnp.dot(p_lo, map_ref[r], preferred_element_type=F32))
        j = _payload_block_id((tq, PAIR_LANES), r)
        bias = _select_bias(rel, j, own, NSA_TOPK)
        qa_ref[r] = _augment(q, bias, r)
        ka_ref[r] = _augment(k, (j == own).astype(F32), r)
    lane = lax.broadcasted_iota(jnp.int32, (tq, PAIR_LANES), 1)
    oc_ref[...] = jnp.where(lane < HEAD_DIM, outs[0], outs[1])


def _nsa_cmp(att, cmp_kv, batch, seq, n_cmp, *, tq):
    tq = min(tq, seq)
    nq = seq // tq
    n_pad = cmp_kv.shape[1]
    aug = jax.ShapeDtypeStruct((batch, N_HEADS, seq, PAIR_LANES), BF16)
    aug_spec = pl.BlockSpec((None, 2, tq, PAIR_LANES), lambda b, p, i: (b, p, i, 0))
    tok = lambda g: pl.BlockSpec((tq, PAIR_LANES), lambda b, p, i: (b * nq + i, g * 2 + p))
    return pl.pallas_call(
        functools.partial(_nsa_cmp_kernel, tq=tq, n_pad=n_pad),
        grid=(batch, 2, nq),
        in_specs=[tok(G_QDN), tok(G_QD), tok(G_KDS),
                  pl.BlockSpec((None, n_pad, PAIR_LANES), lambda b, p, i: (b, 0, p)),
                  pl.BlockSpec((None, n_pad, PAIR_LANES), lambda b, p, i: (b, 0, 2 + p)),
                  _const_spec((2, n_pad, PAIR_LANES)), _const_spec((2, PAIR_LANES))],
        out_specs=(pl.BlockSpec((tq, PAIR_LANES), lambda b, p, i: (b * nq + i, p)), aug_spec, aug_spec),
        out_shape=(jax.ShapeDtypeStruct((batch * seq, BR_WIDTH), F32), aug, aug),
        compiler_params=_cparams(("parallel", "parallel", "parallel")),
        name="nsa_cmp",
    )(att, att, att, cmp_kv, cmp_kv, _cmp_to_slc_map(n_pad, n_cmp), _lane_masks(PAIR_MASKS))


def _resident_spec(shape):
    nd = len(shape)
    return pl.BlockSpec(shape, lambda *_: (0,) * nd, pipeline_mode=pl.Buffered(1))


def _gate_expand_matrix():
    m = np.zeros((3, 128, BR_WIDTH), np.float32)
    for c in range(3):
        for h in range(N_HEADS):
            m[c, N_HEADS + 3 * h + c, h * HEAD_DIM:(h + 1) * HEAD_DIM] = 1.0
    return jnp.asarray(m, BF16)


def _merge_kernel(x_ref, hb_ref, oa_ref, ob_ref, oc_ref, odc_ref, ods_ref, odw_ref, small_ref,
                  g1_ref, sc2_ref, sh2_ref, gn2_ref, ex_ref, wbr_ref, wg_ref, bg_ref, wo_ref,
                  xo_ref, h2_ref):
    d = x_ref.shape[1]
    sm = small_ref[...]
    sm_hi = sm.astype(BF16)
    sm_lo = (sm - sm_hi.astype(F32)).astype(BF16)
    o_d = jnp.zeros(odc_ref.shape, F32)
    for c, ref in enumerate((odc_ref, ods_ref, odw_ref)):
        gate = (jnp.dot(sm_hi, ex_ref[c], preferred_element_type=F32)
                + jnp.dot(sm_lo, ex_ref[c], preferred_element_type=F32))
        o_d = o_d + gate * ref[...]
    hb = hb_ref[...]
    branches = (oa_ref[...], ob_ref[...], oc_ref[...], o_d.astype(BF16))
    merged = jnp.zeros((x_ref.shape[0], d), F32)
    for g, o in enumerate(branches):
        br = jnp.dot(o, wbr_ref[g], preferred_element_type=F32)
        z = jnp.dot(hb, wg_ref[:, g * d:(g + 1) * d], preferred_element_type=F32) + bg_ref[:, g * d:(g + 1) * d]
        merged = merged + jax.nn.sigmoid(z) * br
    y = jnp.dot(merged.astype(BF16), wo_ref[...], preferred_element_type=F32)
    x = x_ref[...] + g1_ref[...] * y
    xo_ref[...] = x
    ms = jnp.mean(x * x, axis=-1, keepdims=True)
    h2 = (x * lax.rsqrt(ms + EPS) * gn2_ref[...]) * (1.0 + sc2_ref[...]) + sh2_ref[...]
    h2_ref[...] = h2.astype(BF16)


def _mod_spec(mod, d, tm, rows_per_mod):
    if rows_per_mod == 1:
        return pl.BlockSpec((tm, d), lambda i: (i, 0)), mod
    tpm = rows_per_mod // tm
    return pl.BlockSpec((None, 1, d), lambda i: (i // tpm, 0, 0)), mod.reshape(-1, 1, d)


def _merge(x2d, hb, o_a, o_b, o_c, o_dc, o_ds, o_dw, small, g1, sc2, sh2, gn2,
           w_branch, w_gate, b_gate, w_out, *, rows_per_mod, tm):
    n, d = x2d.shape
    tm = min(tm, n)
    row = lambda w: pl.BlockSpec((tm, w), lambda i: (i, 0))
    mods = [_mod_spec(m, d, tm, rows_per_mod) for m in (g1, sc2, sh2)]
    return pl.pallas_call(
        _merge_kernel,
        grid=(n // tm,),
        in_specs=[row(d), row(d)] + [row(BR_WIDTH)] * 6 + [row(128)] + [m[0] for m in mods]
                 + [_const_spec((1, d)), _const_spec((3, 128, BR_WIDTH)),
                    _resident_spec(w_branch.shape), _resident_spec(w_gate.shape),
                    _const_spec((1, 4 * d)), _resident_spec(w_out.shape)],
        out_specs=(row(d), row(d)),
        out_shape=(jax.ShapeDtypeStruct((n, d), F32), jax.ShapeDtypeStruct((n, d), BF16)),
        compiler_params=_cparams(("parallel",)),
        name="merge",
    )(x2d, hb, o_a, o_b, o_c, o_dc, o_ds, o_dw, small, *[m[1] for m in mods],
      gn2.reshape(1, d).astype(F32), _gate_expand_matrix(),
      w_branch.astype(BF16), w_gate.astype(BF16), b_gate.reshape(1, 4 * d).astype(F32), w_out.astype(BF16))


def _swiglu_acc(h, wg_ref, wu_ref, wd_ref, chunk):
    ff = wg_ref.shape[-1]
    acc = None
    for c0 in range(0, ff, chunk):
        g = jnp.dot(h, wg_ref[:, c0:c0 + chunk], preferred_element_type=F32)
        u = jnp.dot(h, wu_ref[:, c0:c0 + chunk], preferred_element_type=F32)
        part = jnp.dot((g * jax.nn.sigmoid(g) * u).astype(BF16), wd_ref[c0:c0 + chunk, :],
                       preferred_element_type=F32)
        acc = part if acc is None else acc + part
    return acc


def _ffn_kernel(x_ref, h_ref, g2_ref, wg_ref, wu_ref, wd_ref, o_ref, *, chunk):
    o_ref[...] = x_ref[...] + g2_ref[...] * _swiglu_acc(h_ref[...], wg_ref, wu_ref, wd_ref, chunk)


def _ff_chunk(ff):
    for c in (1408, 1024, 896, 512, 256, 128):
        if ff % c == 0:
            return c
    return ff


def _ffn(x2d, h2, g2, wg, wu, wd, *, rows_per_mod, tm):
    n, d = x2d.shape
    tm = min(tm, n)
    row = lambda w: pl.BlockSpec((tm, w), lambda i: (i, 0))
    g2_spec, g2 = _mod_spec(g2, d, tm, rows_per_mod)
    return pl.pallas_call(
        functools.partial(_ffn_kernel, chunk=_ff_chunk(wg.shape[1])),
        grid=(n // tm,),
        in_specs=[row(d), row(d), g2_spec, _resident_spec(wg.shape), _resident_spec(wu.shape),
                  _resident_spec(wd.shape)],
        out_specs=row(d),
        out_shape=jax.ShapeDtypeStruct((n, d), F32),
        compiler_params=_cparams(("parallel",)),
        name="ffn",
    )(x2d, h2, g2, wg.astype(BF16), wu.astype(BF16), wd.astype(BF16))


def _moe_kernel(x_ref, h_ref, g2_ref, wr_ref, br_ref, wg_ref, wu_ref, wd_ref, o_ref,
                comb_ref, acc_ref, *, chunk, n_exp):
    e = pl.program_id(1)
    h = h_ref[...]

    @pl.when(e == 0)
    def _():
        logits = jnp.dot(h, wr_ref[...], preferred_element_type=F32) + br_ref[...]
        lane = lax.broadcasted_iota(jnp.int32, logits.shape, 1)
        real = lane < n_exp
        logits = jnp.where(real, logits, NEG)
        ex = jnp.exp(logits - jnp.max(logits, axis=-1, keepdims=True))
        prob = ex / jnp.sum(ex, axis=-1, keepdims=True)
        avail = real
        comb = jnp.zeros(prob.shape, F32)
        for _ in range(2):
            val = jnp.where(avail, prob, -1.0)
            mx = jnp.max(val, axis=-1, keepdims=True)
            idx = jnp.min(jnp.where(jnp.logical_and(avail, val == mx), lane, 1 << 20), axis=-1, keepdims=True)
            pick = lane == idx
            comb = jnp.where(pick, prob, comb)
            avail = jnp.logical_and(avail, jnp.logical_not(pick))
        comb_ref[...] = comb / jnp.sum(comb, axis=-1, keepdims=True)
        acc_ref[...] = jnp.zeros(acc_ref.shape, F32)

    lane = lax.broadcasted_iota(jnp.int32, comb_ref.shape, 1)
    w_e = jnp.sum(jnp.where(lane == e, comb_ref[...], 0.0), axis=-1, keepdims=True)

    @pl.when(jnp.max(w_e) > 0.0)
    def _():
        acc_ref[...] += w_e * _swiglu_acc(h, wg_ref, wu_ref, wd_ref, chunk)

    @pl.when(e == n_exp - 1)
    def _():
        o_ref[...] = x_ref[...] + g2_ref[...] * acc_ref[...]


def _moe(x2d, h2, g2, router, router_b, wg, wu, wd, *, rows_per_mod, tm):
    n, d = x2d.shape
    n_exp = wg.shape[0]
    tm = min(tm, n)
    row = lambda w: pl.BlockSpec((tm, w), lambda i, e: (i, 0))
    if rows_per_mod == 1:
        g2_spec = row(d)
    else:
        tpm = rows_per_mod // tm
        g2_spec = pl.BlockSpec((None, 1, d), lambda i, e: (i // tpm, 0, 0))
        g2 = g2.reshape(-1, 1, d)
    wr = jnp.pad(router.astype(BF16), ((0, 0), (0, 128 - n_exp)))
    br = jnp.pad(router_b.astype(F32), (0, 128 - n_exp)).reshape(1, 128)
    exp_spec = lambda a: pl.BlockSpec((None,) + a.shape[1:], lambda i, e: (e, 0, 0))
    return pl.pallas_call(
        functools.partial(_moe_kernel, chunk=_ff_chunk(wg.shape[2]), n_exp=n_exp),
        grid=(n // tm, n_exp),
        in_specs=[row(d), row(d), g2_spec,
                  pl.BlockSpec((d, 128), lambda i, e: (0, 0)), pl.BlockSpec((1, 128), lambda i, e: (0, 0)),
                  exp_spec(wg), exp_spec(wu), exp_spec(wd)],
        out_specs=row(d),
        out_shape=jax.ShapeDtypeStruct((n, d), F32),
        scratch_shapes=[pltpu.VMEM((tm, 128), F32), pltpu.VMEM((tm, d), F32)],
        compiler_params=_cparams(("parallel", "arbitrary")),
        name="moe",
    )(x2d, h2, g2, wr, br, wg.astype(BF16), wu.astype(BF16), wd.astype(BF16))


def _decode_row_masks(diff):
    lane = np.arange(BR_WIDTH)
    qm = np.zeros((8, BR_WIDTH), np.float32)
    om = np.zeros((8, BR_WIDTH), np.float32)
    for r in range(8 if diff else N_HEADS):
        h = r // 2 if diff else r
        lo, hi = (h * HEAD_DIM + (r % 2) * DIFF_DIM, h * HEAD_DIM + (r % 2 + 1) * DIFF_DIM) if diff \
            else (h * HEAD_DIM, (h + 1) * HEAD_DIM)
        qm[r] = (lane >= lo) & (lane < hi)
        om[r] = (lane >= h * HEAD_DIM) & (lane < (h + 1) * HEAD_DIM)
    return jnp.asarray(qm, BF16), jnp.asarray(om, F32)


def _segment_matrices(n_keys, blk):
    seg = (np.arange(n_keys)[:, None] // blk == np.arange(128)[None, :]).astype(np.float32)
    return jnp.asarray(seg / blk, F32), jnp.asarray(seg.T, BF16)


def _decode_kernel(pt_ref, *refs, n_pg, mode, fox, diff, lam_scale, skip_first):
    del pt_ref
    refs = list(refs)
    q_ref, kvn_ref = refs[:2]
    pages = refs[2:2 + n_pg]
    refs = refs[2 + n_pg:]
    if fox:
        lf_pages = refs[:n_pg]
        smalln_ref, tri_ref, diag_ref = refs[n_pg:n_pg + 3]
        refs = refs[n_pg + 3:]
    if mode == "sel":
        sel_ref = refs.pop(0)
    qm_ref, om_ref = refs[:2]
    refs = refs[2:]
    if mode == "moba":
        segm_ref, segt_ref = refs[:2]
        refs = refs[2:]
    elif mode == "sel":
        segt_ref = refs.pop(0)
    if diff:
        lam_ref, bd_ref, gsub_ref = refs[:3]
        refs = refs[3:]
    (o_ref,) = refs

    qrows = q_ref[...] * qm_ref[...]
    kn = kvn_ref[:, :BR_WIDTH].astype(BF16).astype(F32)
    vn = kvn_ref[:, BR_WIDTH:].astype(BF16).astype(F32)
    s_new = jnp.sum(qrows.astype(F32) * kn, axis=-1, keepdims=True)
    s = jnp.concatenate([jnp.dot(qrows, pg[0].astype(BF16), preferred_element_type=F32) for pg in pages], axis=1)
    n_keys = s.shape[1]
    if fox:
        carry = jnp.sum(diag_ref[...] * smalln_ref[...], axis=-1, keepdims=True)
        parts = []
        for lf_ref in reversed(lf_pages):
            lf = jnp.concatenate([lf_ref[...], jnp.zeros((8 - N_HEADS, lf_ref.shape[1]), F32)], axis=0)
            parts.append(jnp.dot(lf, tri_ref[...], precision=lax.Precision.HIGHEST,
                                 preferred_element_type=F32) + carry)
            carry = carry + jnp.sum(lf, axis=-1, keepdims=True)
        s = s + jnp.concatenate(parts[::-1], axis=1) * LOG2E
    keep = None
    if mode == "moba":
        rel = jnp.dot(s, segm_ref[...], precision=lax.Precision.HIGHEST, preferred_element_type=F32)
        lane = lax.broadcasted_iota(jnp.int32, rel.shape, 1)
        nb = n_keys // MOBA_BLOCK
        picked = _select_bias(rel, jnp.where(lane < nb, lane, -1), jnp.full((8, 1), nb, jnp.int32), MOBA_TOPK)
        sel = jnp.where(jnp.logical_and(picked == 0.0, lane < nb), 1.0, 0.0).astype(BF16)
        keep = jnp.dot(sel, segt_ref[...], preferred_element_type=F32) > 0.5
    elif mode == "sel":
        keep = jnp.dot(sel_ref[...].astype(BF16), segt_ref[...], preferred_element_type=F32) > 0.5
    if skip_first:
        fresh = lax.broadcasted_iota(jnp.int32, s.shape, 1) >= skip_first
        keep = fresh if keep is None else jnp.logical_and(keep, fresh)
    if keep is not None:
        s = jnp.where(keep, s, NEG)
    m = jnp.maximum(jnp.max(s, axis=-1, keepdims=True), s_new)
    e = jnp.exp2(s - m)
    e_new = jnp.exp2(s_new - m)
    denom = jnp.sum(e, axis=-1, keepdims=True) + e_new
    eb = e.astype(BF16)
    w = pages[0].shape[-1]
    o = e_new.astype(BF16).astype(F32) * vn
    for i, pg in enumerate(pages):
        o = o + _dot_nt(eb[:, i * w:(i + 1) * w], pg[1].astype(BF16))
    o = o / denom
    if diff:
        row = lax.broadcasted_iota(jnp.int32, (8, 1), 0)
        o = o * jnp.where(row % 2 == 0, 1.0, -lam_ref[0])
    out = jnp.sum(o * om_ref[...], axis=0, keepdims=True)
    if diff:
        out = _group_rms(out, bd_ref[...], gsub_ref[...]) * lam_scale
    o_ref[...] = out


def _decode_attn(att3, q_col, kvn3, cache_t, li, page_table, *, mode="plain", fox=None, diff=None,
                 sel=None, blk=None, dense=False, skip_first=0):
    s_n = att3.shape[0]
    rows = cache_t.shape[-1]
    n_pg = 1 if dense else page_table.shape[1]
    n_keys = n_pg * rows
    pt = page_table.reshape(-1).astype(jnp.int32)
    if dense:
        page_specs = [pl.BlockSpec((None, None, 2, BR_WIDTH, rows), lambda b, pt: (li, b, 0, 0, 0))]
    else:
        page_specs = [pl.BlockSpec((None, None, 2, BR_WIDTH, rows),
                                   lambda b, pt, j=j: (li, pt[b * n_pg + j], 0, 0, 0)) for j in range(n_pg)]
    in_specs = [pl.BlockSpec((None, 1, BR_WIDTH), lambda b, pt: (b, 0, q_col)),
                pl.BlockSpec((None, 1, 2 * BR_WIDTH), lambda b, pt: (b, 0, 0))] + page_specs
    args = [att3, kvn3] + [cache_t] * n_pg
    cst = lambda a: (pl.BlockSpec(a.shape, lambda b, pt: (0,) * a.ndim), a)
    consts = []
    if fox is not None:
        logf_t, small3 = fox
        in_specs += [pl.BlockSpec((None, None, N_HEADS, rows), lambda b, pt, j=j: (li, pt[b * n_pg + j], 0, 0))
                     for j in range(n_pg)]
        args += [logf_t] * n_pg
        in_specs.append(pl.BlockSpec((None, 1, 128), lambda b, pt: (b, 0, 0)))
        args.append(small3)
        consts += [cst(jnp.asarray(np.tril(np.ones((rows, rows), np.float32), -1))),
                   cst(jnp.asarray(np.eye(8, 128, dtype=np.float32) * (np.arange(8)[:, None] < N_HEADS)))]
    qm, om = _decode_row_masks(diff is not None)
    consts += [cst(qm), cst(om)]
    if mode == "moba":
        segm, segt = _segment_matrices(n_keys, MOBA_BLOCK)
        consts += [cst(segm), cst(segt)]
    elif mode == "sel":
        in_specs.append(pl.BlockSpec((None, 8, 128), lambda b, pt: (b, 0, 0)))
        args.append(sel)
        consts.append(cst(_segment_matrices(n_keys, blk)[1]))
    in_specs += [c[0] for c in consts]
    args += [c[1] for c in consts]
    lam_scale = 1.0
    if diff is not None:
        lam, lam_scale, gsub = diff
        bd = _group_mean_matrix(BR_WIDTH, HEAD_DIM)
        in_specs += [pl.BlockSpec(memory_space=pltpu.SMEM), pl.BlockSpec(bd.shape, lambda b, pt: (0, 0)),
                     pl.BlockSpec((1, BR_WIDTH), lambda b, pt: (0, 0))]
        args += [lam.reshape(1).astype(F32), bd, gsub.reshape(1, BR_WIDTH)]
    kern = functools.partial(_decode_kernel, n_pg=n_pg, mode=mode, fox=fox is not None, diff=diff is not None,
                             lam_scale=lam_scale, skip_first=skip_first)
    return pl.pallas_call(
        kern,
        grid_spec=pltpu.PrefetchScalarGridSpec(
            num_scalar_prefetch=1, grid=(s_n,), in_specs=in_specs,
            out_specs=pl.BlockSpec((None, 1, BR_WIDTH), lambda b, pt: (b, 0, 0))),
        out_shape=jax.ShapeDtypeStruct((s_n, 1, BR_WIDTH), F32),
        compiler_params=_cparams(("parallel",)),
        name="decode_" + mode + ("_fox" if fox is not None else "") + ("_diff" if diff is not None else "")
             + ("_dense" if dense else ""),
    )(pt, *args)


def _decode_cmp_kernel(q_ref, kc_ref, vc_ref, map_ref, qm_ref, om_ref, oc_ref, sel_ref, *, n_cmp, own):
    qrows = q_ref[...] * qm_ref[...]
    s = _dot_nt(qrows, kc_ref[...].astype(BF16))
    visible = lax.broadcasted_iota(jnp.int32, s.shape, 1) < n_cmp
    s = jnp.where(visible, s, NEG)
    e = jnp.where(visible, jnp.exp2(s - jnp.max(s, axis=-1, keepdims=True)), 0.0)
    p = e / jnp.maximum(jnp.sum(e, axis=-1, keepdims=True), 1e-30)
    p_hi = p.astype(BF16)
    p_lo = (p - p_hi.astype(F32)).astype(BF16)
    o = jnp.dot(p_hi, vc_ref[...].astype(BF16), preferred_element_type=F32)
    oc_ref[...] = jnp.sum(o * om_ref[...], axis=0, keepdims=True)
    rel = (jnp.dot(p_hi, map_ref[...], preferred_element_type=F32)
           + jnp.dot(p_lo, map_ref[...], preferred_element_type=F32))
    lane = lax.broadcasted_iota(jnp.int32, rel.shape, 1)
    picked = _select_bias(rel, lane, jnp.full((8, 1), own, jnp.int32), NSA_TOPK)
    sel_ref[...] = jnp.where(jnp.logical_and(picked == 0.0, lane < own), 1.0, 0.0)


def _decode_cmp(att3, cmp_kv, n_cmp, own):
    s_n, n_pad, _ = cmp_kv.shape
    assert own <= 128
    start = np.arange(n_pad)[:, None] * NSA_CMP_STRIDE
    blk = np.arange(128)[None, :] * NSA_SLC_BLOCK
    ov = np.minimum(start + NSA_CMP_LEN, blk + NSA_SLC_BLOCK) - np.maximum(start, blk)
    cmap = np.maximum(ov, 0).astype(np.float32) / NSA_CMP_LEN
    cmap[n_cmp:] = 0.0
    qm, om = _decode_row_masks(False)
    return pl.pallas_call(
        functools.partial(_decode_cmp_kernel, n_cmp=n_cmp, own=own),
        grid=(s_n,),
        in_specs=[pl.BlockSpec((None, 1, BR_WIDTH), lambda b: (b, 0, G_QDN)),
                  pl.BlockSpec((None, n_pad, BR_WIDTH), lambda b: (b, 0, 0)),
                  pl.BlockSpec((None, n_pad, BR_WIDTH), lambda b: (b, 0, 1)),
                  _const_spec((n_pad, 128)), _const_spec((8, BR_WIDTH)), _const_spec((8, BR_WIDTH))],
        out_specs=(pl.BlockSpec((None, 1, BR_WIDTH), lambda b: (b, 0, 0)),
                   pl.BlockSpec((None, 8, 128), lambda b: (b, 0, 0))),
        out_shape=(jax.ShapeDtypeStruct((s_n, 1, BR_WIDTH), F32), jax.ShapeDtypeStruct((s_n, 8, 128), F32)),
        compiler_params=_cparams(("parallel",)),
        name="decode_cmp",
    )(att3, cmp_kv, cmp_kv, jnp.asarray(cmap, BF16), qm, om)


def _decode_nsa_cmp_kernel(pt_ref, *refs, n_pg, n_cmp, own):
    del pt_ref
    q_ref = refs[0]
    pages = refs[1:1 + n_pg]
    (w1_ref, b1_ref, w2_ref, b2_ref, gain_ref, map_ref, oc_ref, sel_ref, xs_ref, cst_ref) = refs[1 + n_pg:]
    rows = pages[0].shape[-1]
    n_ch = n_pg * rows // NSA_CMP_STRIDE
    for j, pg in enumerate(pages):
        for kv in range(2):
            for p in range(2):
                xs_ref[kv, p, j * rows:(j + 1) * rows, :] = jnp.transpose(pg[kv, p * PAIR_LANES:(p + 1) * PAIR_LANES, :])
    lane = lax.broadcasted_iota(jnp.int32, (n_ch, PAIR_LANES), 1)
    for kv in range(2):
        for p in range(2):
            for i in range(NSA_CMP_STRIDE // 2):
                x0 = xs_ref[kv, p, pl.ds(2 * i, n_ch, stride=NSA_CMP_STRIDE), :]
                x1 = xs_ref[kv, p, pl.ds(2 * i + 1, n_ch, stride=NSA_CMP_STRIDE), :]
                even = jnp.where(lane < HEAD_DIM, x0, pltpu.roll(x1, HEAD_DIM, 1))
                odd = jnp.where(lane < HEAD_DIM, pltpu.roll(x0, HEAD_DIM, 1), x1)
                cols = slice(i * PAIR_LANES, (i + 1) * PAIR_LANES)
                cst_ref[kv, (2 * p) * n_ch:(2 * p + 1) * n_ch, cols] = even.astype(BF16)
                cst_ref[kv, (2 * p + 1) * n_ch:(2 * p + 2) * n_ch, cols] = odd.astype(BF16)
    toks = []
    for kv in range(2):
        uv = jnp.dot(cst_ref[kv], w1_ref[kv], preferred_element_type=F32)
        hid = uv.shape[1] // 2
        pre = uv[:, :hid] + pltpu.roll(uv[:, hid:], N_HEADS * n_ch - 1, 0) + b1_ref[kv]
        y = jnp.dot(jax.nn.gelu(pre, approximate=True).astype(BF16), w2_ref[kv],
                    preferred_element_type=F32) + b2_ref[kv]
        if kv == 0:
            y = y * lax.rsqrt(jnp.mean(y * y, axis=-1, keepdims=True) + EPS) * gain_ref[...]
        toks.append(y.astype(BF16))
    yk, yv = toks
    q = q_ref[...].astype(F32)
    q8 = jnp.concatenate([q[:, h * HEAD_DIM:(h + 1) * HEAD_DIM] for h in range(N_HEADS)]
                         + [jnp.zeros((8 - N_HEADS, HEAD_DIM), F32)], axis=0).astype(BF16)
    s_all = _dot_nt(q8, yk)
    row = lax.broadcasted_iota(jnp.int32, (8, n_ch), 0)
    s = jnp.zeros((8, n_ch), F32)
    for h in range(N_HEADS):
        s = s + jnp.where(row == h, s_all[:, h * n_ch:(h + 1) * n_ch], 0.0)
    visible = lax.broadcasted_iota(jnp.int32, (8, n_ch), 1) < n_cmp
    s = jnp.where(visible, s, NEG)
    e = jnp.where(visible, jnp.exp2(s - jnp.max(s, axis=-1, keepdims=True)), 0.0)
    p = e / jnp.maximum(jnp.sum(e, axis=-1, keepdims=True), 1e-30)
    p_hi = p.astype(BF16)
    p_lo = (p - p_hi.astype(F32)).astype(BF16)
    p_all = jnp.concatenate([jnp.where(row == h, p_hi, jnp.zeros_like(p_hi)) for h in range(N_HEADS)], axis=1)
    o4 = jnp.dot(p_all, yv, preferred_element_type=F32)
    oc_ref[...] = jnp.concatenate([o4[h:h + 1, :] for h in range(N_HEADS)], axis=1)
    rel = (jnp.dot(p_hi, map_ref[...], preferred_element_type=F32)
           + jnp.dot(p_lo, map_ref[...], preferred_element_type=F32))
    lane_b = lax.broadcasted_iota(jnp.int32, rel.shape, 1)
    picked = _select_bias(rel, lane_b, jnp.full((8, 1), own, jnp.int32), NSA_TOPK)
    sel_ref[...] = jnp.where(jnp.logical_and(picked == 0.0, lane_b < own), 1.0, 0.0)


def _decode_nsa_cmp(att3, cache_t, li, page_table, cmpw, n_cmp, own):
    s_n = att3.shape[0]
    rows = cache_t.shape[-1]
    n_pg = page_table.shape[1]
    n_ch = n_pg * rows // NSA_CMP_STRIDE
    assert own <= 128
    w1ab, bias1, w2, b2, gain = _compress_weights(*cmpw)
    start = np.arange(n_ch)[:, None] * NSA_CMP_STRIDE
    blk = np.arange(128)[None, :] * NSA_SLC_BLOCK
    ov = np.minimum(start + NSA_CMP_LEN, blk + NSA_SLC_BLOCK) - np.maximum(start, blk)
    cmap = np.maximum(ov, 0).astype(np.float32) / NSA_CMP_LEN
    cmap[n_cmp:] = 0.0
    pt = page_table.reshape(-1).astype(jnp.int32)
    cst = lambda a: pl.BlockSpec(a.shape, lambda b, pt: (0,) * a.ndim)
    consts = [w1ab, bias1, w2, b2, gain, jnp.asarray(cmap, BF16)]
    return pl.pallas_call(
        functools.partial(_decode_nsa_cmp_kernel, n_pg=n_pg, n_cmp=n_cmp, own=own),
        grid_spec=pltpu.PrefetchScalarGridSpec(
            num_scalar_prefetch=1, grid=(s_n,),
            in_specs=[pl.BlockSpec((None, 1, BR_WIDTH), lambda b, pt: (b, 0, G_QDN))]
                     + [pl.BlockSpec((None, None, 2, BR_WIDTH, rows),
                                     lambda b, pt, j=j: (li, pt[b * n_pg + j], 0, 0, 0)) for j in range(n_pg)]
                     + [cst(a) for a in consts],
            out_specs=(pl.BlockSpec((None, 1, BR_WIDTH), lambda b, pt: (b, 0, 0)),
                       pl.BlockSpec((None, 8, 128), lambda b, pt: (b, 0, 0))),
            scratch_shapes=[pltpu.VMEM((2, 2, n_pg * rows, PAIR_LANES), F32),
                            pltpu.VMEM((2, N_HEADS * n_ch, NSA_CMP_STRIDE * HEAD_DIM), BF16)]),
        out_shape=(jax.ShapeDtypeStruct((s_n, 1, BR_WIDTH), F32), jax.ShapeDtypeStruct((s_n, 8, 128), F32)),
        compiler_params=_cparams(("parallel",)),
        name="decode_nsa_cmp",
    )(pt, att3, *([cache_t] * n_pg), *consts)


def _sample_attention(att, kvs, small, caches_t, logf_t, win_t, cmp_t, page_table, li, diff, cmpw, past_len):
    s_n = att.shape[0]
    kva, kvb, kvc, kvds, kvdw = (a.reshape(s_n, 1, 2 * BR_WIDTH) for a in kvs)
    att3 = att.reshape(s_n, 1, -1)
    ca, cb, cc, cds = caches_t
    dec = functools.partial(_decode_attn, att3, li=li, page_table=page_table)
    o_a = dec(G_QA, kva, ca, fox=(logf_t, small.reshape(s_n, 1, 128)))
    o_b = dec(G_QB, kvb, cb, diff=diff)
    o_c = dec(G_QC, kvc, cc, mode="moba")
    n_cmp = (past_len + 1 - NSA_CMP_LEN) // NSA_CMP_STRIDE + 1
    o_dc, sel = _decode_nsa_cmp(att3, cmp_t, li, page_table, cmpw, n_cmp, past_len // NSA_SLC_BLOCK)
    o_ds = dec(G_QD, kvds, cds, mode="sel", sel=sel, blk=NSA_SLC_BLOCK)
    buf_len = win_t.shape[-1]
    o_dw = dec(G_QD, kvdw, win_t, dense=True, skip_first=max(0, buf_len - NSA_WINDOW + 1))
    flat = lambda a, dt: a.reshape(s_n, BR_WIDTH).astype(dt)
    return flat(o_a, BF16), flat(o_b, BF16), flat(o_c, BF16), flat(o_dc, F32), flat(o_ds, F32), flat(o_dw, F32)


def _layer_gains(fox_qnorm, fox_knorm, diff_qnorm, diff_knorm, moba_qnorm, moba_knorm, nsa_qnorm, nsa_knorm):
    return jnp.stack([_tile_gain(fox_qnorm, HEAD_DIM), _tile_gain(fox_knorm, HEAD_DIM),
                      _tile_gain(diff_qnorm, DIFF_DIM), _tile_gain(diff_knorm, DIFF_DIM),
                      _tile_gain(moba_qnorm, HEAD_DIM), _tile_gain(moba_knorm, HEAD_DIM),
                      _tile_gain(nsa_qnorm, HEAD_DIM), _tile_gain(nsa_knorm[1], HEAD_DIM),
                      _tile_gain(nsa_knorm[2], HEAD_DIM)])


def _chunk_rows(kv5):
    b, l = kv5.shape[:2]
    m = l // NSA_CMP_STRIDE
    c = kv5.reshape(b, m, NSA_CMP_STRIDE, 2, N_HEADS, HEAD_DIM)
    return jnp.transpose(c, (3, 0, 4, 1, 2, 5)).reshape(2, b * N_HEADS * m, NSA_CMP_STRIDE * HEAD_DIM)


def _unchunk_tokens(tok, b, m):
    return jnp.transpose(tok.reshape(2, b, N_HEADS, m, HEAD_DIM), (1, 3, 0, 2, 4)).reshape(b, m, 2 * BR_WIDTH)


def _prompt_attention(att, kvc, kvdc, small, diff, cmpw, batch, seq, *, tq=256, tk=512):
    fl = functools.partial(_flash, batch=batch, seq=seq, tq=tq, tk=tk)
    ft8 = _cumsum_t(small, batch, seq)
    o_a = fl(att, att, att, q_col=G_QA, k_col=G_KA, v_col=G_VA, fox_t=ft8)
    o_b = fl(att, att, att, q_col=G_QB, k_col=G_KB, v_col=G_VB, diff=diff)
    nb = seq // MOBA_BLOCK
    kmean = _blockmean(kvc, batch, seq, MOBA_BLOCK, -(-nb // 8) * 8)
    qa_c, ka_c = _moba_router(att, kmean, batch, seq, tq=tq)
    o_c = fl(qa_c, ka_c, att, v_col=G_VC, aug=True)
    m = seq // NSA_CMP_STRIDE
    pos_emb, w1, b1, w2, b2, k_gain = cmpw
    tok = _compress(_chunk_rows(kvdc.reshape(batch, seq, 2, N_HEADS, HEAD_DIM)), pos_emb, w1, b1, w2, b2, k_gain,
                    chunks_per_seq=m, seqs_per_tile=max(1, 512 // m))
    cmp_kv = _unchunk_tokens(tok, batch, m)
    o_dc, qa_d, ka_d = _nsa_cmp(att, cmp_kv, batch, seq, m - 1, tq=tq)
    o_ds = fl(qa_d, ka_d, att, v_col=G_VDS, aug=True, out_dtype=F32)
    o_dw = fl(att, att, att, q_col=G_QD, k_col=G_KDW, v_col=G_VDW, window=NSA_WINDOW, out_dtype=F32)
    return o_a, o_b, o_c, o_dc, o_ds, o_dw


def _trunk_tail(x2d, hb, branch, small, g1, sc2, sh2, g2, gn2, w_branch, w_gate, b_gate, w_out, ffn,
                *, rows_per_mod, tm=512):
    x2d, h2 = _merge(x2d, hb, *branch, small, g1, sc2, sh2, gn2, w_branch, w_gate, b_gate, w_out,
                     rows_per_mod=rows_per_mod, tm=tm)
    if ffn[0] == "dense":
        return _ffn(x2d, h2, g2, *ffn[1:], rows_per_mod=rows_per_mod, tm=tm)
    return _moe(x2d, h2, g2, *ffn[1:], rows_per_mod=rows_per_mod, tm=tm)


def kernel(x_prompt, x_sample, c_prompt, c_sample, cache_a_kv, cache_a_logf, cache_b_kv, cache_c_kv, cache_d_cmp_kv, cache_d_slc_kv, state_d_win_kv, page_table, ada_w, ada_b, norm_attn, norm_ffn, w_in, fox_fbias, fox_qnorm, fox_knorm, diff_qnorm, diff_knorm, diff_lambda, diff_subnorm, moba_qnorm, moba_knorm, nsa_qnorm, nsa_knorm, nsa_cmp_pos, nsa_cmp_w1, nsa_cmp_b1, nsa_cmp_w2, nsa_cmp_b2, w_branch, w_gate, b_gate, w_out, ffn_w_gate, ffn_w_up, ffn_w_down, moe_router, moe_router_b, moe_w_gate, moe_w_up, moe_w_down):
    bp, t, d = x_prompt.shape
    bs = x_sample.shape[0]
    depth = w_in.shape[0]
    n_pages = page_table.shape[1]
    past_len = n_pages * cache_a_kv.shape[2]
    pos_p = jnp.arange(t, dtype=jnp.int32)
    pos_s = jnp.full((1,), past_len, jnp.int32)
    xp = x_prompt.reshape(bp * t, d)
    xs = x_sample.reshape(bs, d)
    c_all = jnp.concatenate([c_prompt, c_sample, jnp.zeros((-(bp + bs) % 8, d), F32)], axis=0)
    page_t = lambda c: jnp.transpose(c, (0, 1, 3, 4, 5, 2)).reshape(c.shape[:2] + (2, BR_WIDTH, c.shape[2]))
    caches_t = tuple(page_t(c) for c in (cache_a_kv, cache_b_kv, cache_c_kv, cache_d_slc_kv))
    logf_t = jnp.swapaxes(cache_a_logf, 2, 3)
    win_t = page_t(state_d_win_kv)
    cmp_t = page_t(cache_d_cmp_kv)
    st_p = [[] for _ in range(7)]
    st_s = [[] for _ in range(7)]
    for li in range(depth):
        lam_init = 0.8 - 0.6 * math.exp(-0.3 * li)
        lq = diff_lambda[li].astype(F32)
        lam = jnp.exp(jnp.sum(lq[0] * lq[1])) - jnp.exp(jnp.sum(lq[2] * lq[3])) + lam_init
        diff = (lam, 1.0 - lam_init, jnp.tile(diff_subnorm[li].astype(F32), 2))
        cmpw = (nsa_cmp_pos[li], nsa_cmp_w1[li], nsa_cmp_b1[li], nsa_cmp_w2[li], nsa_cmp_b2[li], nsa_knorm[li][0])
        mod = _rowmm(c_all, ada_w[li], ada_b[li], silu_in=True)
        modp = mod[:bp].reshape(bp, 6, d)
        wm, ws, bsm = _split_w_in(w_in[li], fox_fbias[li])
        gains = _layer_gains(fox_qnorm[li], fox_knorm[li], diff_qnorm[li], diff_knorm[li], moba_qnorm[li],
                             moba_knorm[li], nsa_qnorm[li], nsa_knorm[li])
        hb, att, kva, kvb, kvc, kvdc, kvds, kvdw, small = _inproj(
            xp, modp[:, 1], modp[:, 0], norm_attn[li], wm, ws, bsm, gains, pos_p, rows_per_mod=t, tm=512)
        branch = _prompt_attention(att, kvc, kvdc, small, diff, cmpw, bp, t)
        j = li // 2
        if li % 2 == 0:
            ffn = ("dense", ffn_w_gate[j], ffn_w_up[j], ffn_w_down[j])
        else:
            ffn = ("moe", moe_router[j], moe_router_b[j], moe_w_gate[j], moe_w_up[j], moe_w_down[j])
        xp = _trunk_tail(xp, hb, branch, small, modp[:, 2], modp[:, 4], modp[:, 3], modp[:, 5], norm_ffn[li],
                         w_branch[li], w_gate[li], b_gate[li], w_out[li], ffn, rows_per_mod=t)
        kv6 = lambda a: a.reshape(bp, t, 2, N_HEADS, HEAD_DIM)
        win = min(NSA_WINDOW, t)
        for lst, val in zip(st_p, (kv6(kva), small[:, :N_HEADS].reshape(bp, t, N_HEADS), kv6(kvb), kv6(kvc),
                                   kv6(kvdc), kv6(kvds), kv6(kvdw)[:, t - win:])):
            lst.append(val)
        mods = mod[bp:bp + bs].reshape(bs, 6, d)
        hb, att, kva, kvb, kvc, kvdc, kvds, kvdw, small = _inproj(
            xs, mods[:, 1], mods[:, 0], norm_attn[li], wm, ws, bsm, gains, pos_s, rows_per_mod=1, tm=bs)
        diff_s = (lam, 1.0 - lam_init, jnp.tile(diff_subnorm[li].astype(F32), N_HEADS))
        branch = _sample_attention(att, (kva, kvb, kvc, kvds, kvdw), small, caches_t, logf_t, win_t, cmp_t,
                                   page_table, li, diff_s, cmpw, past_len)
        xs = _trunk_tail(xs, hb, branch, small, mods[:, 2], mods[:, 4], mods[:, 3], mods[:, 5], norm_ffn[li],
                         w_branch[li], w_gate[li], b_gate[li], w_out[li], ffn, rows_per_mod=1)
        kv6 = lambda a: a.reshape(bs, 1, 2, N_HEADS, HEAD_DIM)
        new_win = jnp.concatenate([state_d_win_kv[li][:, 1:], kv6(kvdw)], axis=1)
        for lst, val in zip(st_s, (kv6(kva), small[:, :N_HEADS].reshape(bs, 1, N_HEADS), kv6(kvb), kv6(kvc),
                                   kv6(kvdc), kv6(kvds), new_win)):
            lst.append(val)
    stack = lambda st: tuple(jnp.stack(lst, axis=0) for lst in st)
    return (xp.reshape(bp, t, d), xs.reshape(bs, 1, d)) + stack(st_p) + stack(st_s)
```

```python
import functools
import math

import jax
import jax.numpy as jnp
import numpy as np
from jax import lax
from jax.experimental import pallas as pl
from jax.experimental.pallas import tpu as pltpu

F32 = jnp.float32
BF16 = jnp.bfloat16

N_HEADS = 4
HEAD_DIM = 64
BR_WIDTH = N_HEADS * HEAD_DIM
PAIR_LANES = 2 * HEAD_DIM
DIFF_DIM = HEAD_DIM // 2
ROPE_THETA = 500000.0
MOBA_BLOCK = 256
MOBA_TOPK = 3
NSA_CMP_LEN = 32
NSA_CMP_STRIDE = 16
NSA_SLC_BLOCK = 64
NSA_TOPK = 4
NSA_WINDOW = 512
N_EXPERTS = 8
EPS = 1e-6
NEG = -1e30
LOG2E = math.log2(math.e)
V7X_VMEM_BYTES = 64 * 1024 * 1024
VMEM_LIMIT = V7X_VMEM_BYTES - 16 * 1024 * 1024

(G_QA, G_KA, G_VA, G_QB, G_KB, G_VB, G_QC, G_KC, G_VC,
 G_QDN, G_QD, G_KDS, G_VDS, G_KDW, G_VDW) = range(15)
N_ATT_GROUPS = 15


def _cparams(sem):
    return pltpu.CompilerParams(dimension_semantics=sem, vmem_limit_bytes=VMEM_LIMIT)


def _const_spec(shape):
    nd = len(shape)
    return pl.BlockSpec(shape, lambda *_: (0,) * nd)


def _rowmm_kernel(x_ref, w_ref, b_ref, o_ref, *, silu_in):
    x = x_ref[...]
    if silu_in:
        x = x * jax.nn.sigmoid(x)
    o_ref[...] = jnp.dot(x.astype(BF16), w_ref[...].astype(BF16),
                         preferred_element_type=F32) + b_ref[...]


def _rowmm(x, w, b, *, silu_in=False, tn=1024):
    m, k = x.shape
    n = w.shape[1]
    tn = math.gcd(tn, n)
    return pl.pallas_call(
        functools.partial(_rowmm_kernel, silu_in=silu_in),
        grid=(n // tn,),
        in_specs=[pl.BlockSpec((m, k), lambda j: (0, 0)),
                  pl.BlockSpec((k, tn), lambda j: (0, j)),
                  pl.BlockSpec((1, tn), lambda j: (0, j))],
        out_specs=pl.BlockSpec((m, tn), lambda j: (0, j)),
        out_shape=jax.ShapeDtypeStruct((m, n), F32),
        compiler_params=_cparams(("parallel",)),
        name="rowmm",
    )(x, w, b.reshape(1, n))


def _rope_tables(pos, group):
    r = group // 4
    half = r // 2
    inv = ROPE_THETA ** (-(np.arange(half, dtype=np.float32) / half))
    lane = np.arange(BR_WIDTH)
    j = lane % group
    ang = pos.astype(F32)[:, None] * jnp.asarray(inv[j % half], F32)[None, :]
    cos = jnp.where(j[None, :] < r, jnp.cos(ang), 1.0)
    sin = jnp.sin(ang)
    sin_a = jnp.where(j[None, :] < half, -sin, 0.0)
    sin_b = jnp.where((j[None, :] >= half) & (j[None, :] < r), sin, 0.0)
    return cos.astype(F32), sin_a.astype(F32), sin_b.astype(F32)


def _group_mean_matrix(width, group):
    i = np.arange(width)
    return jnp.asarray((i[:, None] // group == i[None, :] // group).astype(np.float32) / group, BF16)


def _group_rms(a, bd, gain):
    ms = jnp.dot((a * a).astype(BF16), bd, preferred_element_type=F32)
    return a * lax.rsqrt(ms + EPS) * gain


def _rope(a, cos, sin_a, sin_b, half):
    w = a.shape[-1]
    return a * cos + pltpu.roll(a, w - half, 1) * sin_a + pltpu.roll(a, half, 1) * sin_b


def _log_sigmoid(x):
    return jnp.minimum(x, 0.0) - jnp.log1p(jnp.exp(-jnp.abs(x)))


def _inproj_kernel(x_ref, sc_ref, sh_ref, gn_ref, w_ref, ws_ref, bs_ref, gains_ref,
                   bd64_ref, bd32_ref, c64_ref, sa64_ref, sb64_ref, c32_ref, sa32_ref, sb32_ref,
                   *rest, transposed, n_alias):
    rest = rest[n_alias:]
    h_ref, att_ref, kva_ref, kvb_ref, kvc_ref, kvdc_ref, kvds_ref, kvdw_ref, small_ref = rest[:9]

    def put_kv(ref, k, v):
        if transposed:
            stage_ref = rest[-1]
            stage_ref[0] = k
            stage_ref[1] = v
            ref[0] = jnp.transpose(stage_ref[0])
            ref[1] = jnp.transpose(stage_ref[1])
        else:
            ref[:, :BR_WIDTH] = k
            ref[:, BR_WIDTH:] = v

    x = x_ref[...]
    ms = jnp.mean(x * x, axis=-1, keepdims=True)
    h = (x * lax.rsqrt(ms + EPS) * gn_ref[...]) * (1.0 + sc_ref[...]) + sh_ref[...]
    hb = h.astype(BF16)
    h_ref[...] = hb
    bd64 = bd64_ref[...]
    bd32 = bd32_ref[...]

    def proj(g):
        return jnp.dot(hb, w_ref[:, g * BR_WIDTH:(g + 1) * BR_WIDTH], preferred_element_type=F32)

    def gain(i):
        return gains_ref[i:i + 1, :]

    def rope64(a):
        return _rope(a, c64_ref[...], sa64_ref[...], sb64_ref[...], 8)

    def rope32(a):
        return _rope(a, c32_ref[...], sa32_ref[...], sb32_ref[...], 4)

    def att(g, a):
        att_ref[:, g * BR_WIDTH:(g + 1) * BR_WIDTH] = a.astype(BF16)

    sm_scale = HEAD_DIM ** -0.5 * LOG2E
    att(G_QA, _group_rms(proj(0), bd64, gain(0)) * sm_scale)
    k = _group_rms(proj(1), bd64, gain(1))
    v = proj(2)
    put_kv(kva_ref, k, v)
    att(G_KA, k)
    att(G_VA, v)
    att(G_QB, rope32(_group_rms(proj(3), bd32, gain(2))) * (DIFF_DIM ** -0.5 * LOG2E))
    k = rope32(_group_rms(proj(4), bd32, gain(3)))
    v = proj(5)
    put_kv(kvb_ref, k, v)
    att(G_KB, k)
    att(G_VB, v)
    att(G_QC, rope64(_group_rms(proj(6), bd64, gain(4))) * sm_scale)
    k = rope64(_group_rms(proj(7), bd64, gain(5)))
    v = proj(8)
    put_kv(kvc_ref, k, v)
    if transposed:
        km_ref = rest[9]
        nb = k.shape[0] // MOBA_BLOCK
        km_ref[...] = jnp.concatenate(
            [jnp.mean(k[b * MOBA_BLOCK:(b + 1) * MOBA_BLOCK], axis=0, keepdims=True) for b in range(nb)]
            + [jnp.zeros((8 - nb, BR_WIDTH), F32)], axis=0)
    att(G_KC, k)
    att(G_VC, v)
    qn = _group_rms(proj(9), bd64, gain(6))
    att(G_QDN, qn * sm_scale)
    att(G_QD, rope64(qn) * sm_scale)
    put_kv(kvdc_ref, proj(10), proj(11))
    k = rope64(_group_rms(proj(12), bd64, gain(7)))
    v = proj(13)
    put_kv(kvds_ref, k, v)
    att(G_KDS, k)
    att(G_VDS, v)
    k = rope64(_group_rms(proj(14), bd64, gain(8)))
    v = proj(15)
    put_kv(kvdw_ref, k, v)
    att(G_KDW, k)
    att(G_VDW, v)
    z = jnp.dot(hb, ws_ref[...], preferred_element_type=F32) + bs_ref[...]
    lane = lax.broadcasted_iota(jnp.int32, z.shape, 1)
    small_ref[...] = jnp.where(lane < N_HEADS, _log_sigmoid(z), jax.nn.sigmoid(z))


def _split_w_in(w_in, fox_fbias):
    d = w_in.shape[0]
    c0 = 3 * BR_WIDTH
    main = jnp.concatenate([w_in[:, :c0], w_in[:, c0 + N_HEADS:c0 + N_HEADS + 13 * BR_WIDTH]], axis=1)
    small = jnp.concatenate([w_in[:, c0:c0 + N_HEADS], w_in[:, c0 + N_HEADS + 13 * BR_WIDTH:],
                             jnp.zeros((d, 128 - 4 * N_HEADS), w_in.dtype)], axis=1)
    bias = jnp.concatenate([fox_fbias.astype(F32), jnp.zeros((128 - N_HEADS,), F32)]).reshape(1, 128)
    return main.astype(BF16), small.astype(BF16), bias


def _inproj(x2d, sc, sh, gn, w_main, w_small, b_small, gains, pos, *, rows_per_mod, tm, stacked=None):
    n, d = x2d.shape
    tm = min(tm, n)
    assert n % tm == 0
    per_tok = rows_per_mod == 1
    if not per_tok:
        assert rows_per_mod % tm == 0
    tabs = _rope_tables(pos, HEAD_DIM) + _rope_tables(pos, DIFF_DIM)
    single_pos = pos.shape[0] == 1
    tiles_per_seq = 1 if single_pos else pos.shape[0] // tm

    if per_tok:
        mod_spec = pl.BlockSpec((tm, d), lambda i: (i, 0))
    else:
        tpm = rows_per_mod // tm
        mod_spec = pl.BlockSpec((None, 1, d), lambda i: (i // tpm, 0, 0))
        sc, sh = sc.reshape(-1, 1, d), sh.reshape(-1, 1, d)
    if single_pos:
        tab_spec = pl.BlockSpec((1, BR_WIDTH), lambda i: (0, 0))
    else:
        tab_spec = pl.BlockSpec((tm, BR_WIDTH), lambda i: (i % tiles_per_seq, 0))

    row = lambda w: pl.BlockSpec((tm, w), lambda i: (i, 0))
    in_specs = [row(d), mod_spec, mod_spec, _const_spec((1, d)),
                _const_spec(w_main.shape), _const_spec(w_small.shape), _const_spec((1, 128)),
                _const_spec(gains.shape),
                _const_spec((BR_WIDTH, BR_WIDTH)), _const_spec((BR_WIDTH, BR_WIDTH))] + [tab_spec] * 6
    args = [x2d, sc, sh, gn.reshape(1, d), w_main, w_small, b_small, gains,
            _group_mean_matrix(BR_WIDTH, HEAD_DIM), _group_mean_matrix(BR_WIDTH, DIFF_DIM), *tabs]
    aliases = {}
    if stacked is None:
        kv_shapes = (jax.ShapeDtypeStruct((n, 2 * BR_WIDTH), F32),) * 6
        kv_specs = (row(2 * BR_WIDTH),) * 6
        extra_shapes, extra_specs = (), ()
    else:
        li, depth, prev = stacked
        seq = pos.shape[0]
        batch = n // seq
        assert tm % MOBA_BLOCK == 0
        kv_shapes = (jax.ShapeDtypeStruct((depth, batch, 2, BR_WIDTH, seq), F32),) * 6
        kv_specs = (pl.BlockSpec((None, None, 2, BR_WIDTH, tm),
                                 lambda i: (li, i // tiles_per_seq, 0, 0, i % tiles_per_seq)),) * 6
        extra_shapes = (jax.ShapeDtypeStruct((n // tm, 8, BR_WIDTH), F32),)
        extra_specs = (pl.BlockSpec((None, 8, BR_WIDTH), lambda i: (i, 0, 0)),)
        if prev is not None:
            aliases = {len(args) + k: 2 + k for k in range(6)}
            in_specs += [pl.BlockSpec(memory_space=pl.ANY)] * 6
            args += list(prev)
    out_shape = ((jax.ShapeDtypeStruct((n, d), BF16), jax.ShapeDtypeStruct((n, N_ATT_GROUPS * BR_WIDTH), BF16))
                 + kv_shapes + (jax.ShapeDtypeStruct((n, 128), F32),) + extra_shapes)
    out_specs = (row(d), row(N_ATT_GROUPS * BR_WIDTH)) + kv_specs + (row(128),) + extra_specs
    return pl.pallas_call(
        functools.partial(_inproj_kernel, transposed=stacked is not None, n_alias=len(aliases)),
        grid=(n // tm,),
        in_specs=in_specs,
        out_specs=out_specs,
        out_shape=out_shape,
        scratch_shapes=[pltpu.VMEM((2, tm, BR_WIDTH), F32)] if stacked is not None else [],
        input_output_aliases=aliases,
        compiler_params=_cparams(("parallel",)),
        name="inproj",
    )(*args)


def _tile_gain(g, group):
    return jnp.tile(g.astype(F32), BR_WIDTH // group)


def _cumsum_kernel(x_ref, tri_ref, ft_ref, carry_ref):
    @pl.when(pl.program_id(1) == 0)
    def _():
        carry_ref[...] = jnp.zeros_like(carry_ref)

    c = jnp.dot(tri_ref[...], x_ref[...], precision=lax.Precision.HIGHEST,
                preferred_element_type=F32) + carry_ref[...]
    carry_ref[...] = c[-1:, :]
    ft_ref[...] = jnp.transpose(c)[:8, :]


def _cumsum_t(small, batch, seq, *, tc=256):
    tc = min(tc, seq)
    nt = seq // tc
    tri = jnp.asarray(np.tril(np.ones((tc, tc), np.float32)))
    return pl.pallas_call(
        _cumsum_kernel,
        grid=(batch, nt),
        in_specs=[pl.BlockSpec((tc, 128), lambda b, t: (b * nt + t, 0)), _const_spec((tc, tc))],
        out_specs=pl.BlockSpec((None, 8, tc), lambda b, t: (b, 0, t)),
        out_shape=jax.ShapeDtypeStruct((batch, 8, seq), F32),
        scratch_shapes=[pltpu.VMEM((1, 128), F32)],
        compiler_params=_cparams(("parallel", "arbitrary")),
        name="cumsum_t",
    )(small, tri)


def _flash_steps(seq, tq, tk, window):
    assert tk % tq == 0 and seq % tk == 0
    qi_l, kj_l, fl_l = [], [], []
    for qi in range(seq // tq):
        t0, t1 = qi * tq, (qi + 1) * tq - 1
        js = []
        for j in range(t0 // tk, -1, -1):
            s0, s1 = j * tk, (j + 1) * tk - 1
            if window is not None and s1 <= t0 - window:
                break
            partial = s1 > t0 or (window is not None and s0 <= t1 - window)
            js.append((j, partial))
        for n, (j, partial) in enumerate(js):
            qi_l.append(qi)
            kj_l.append(j)
            fl_l.append((1 if n == 0 else 0) | (2 if n == len(js) - 1 else 0) | (4 if partial else 0))
    return (jnp.asarray(qi_l, jnp.int32), jnp.asarray(kj_l, jnp.int32), jnp.asarray(fl_l, jnp.int32))


def _dot_nt(a, b):
    return lax.dot_general(a, b, (((1,), (1,)), ((), ())), preferred_element_type=F32)


def _flash_kernel(qi_ref, kj_ref, fl_ref, *refs, aug, groups, tq, tk, fox, window, combine, lam_scale):
    refs = list(refs)
    q_ref, k_ref, v_ref = refs[:3]
    refs = refs[3:]
    mask_ref = None if aug else refs.pop(0)
    ft_ref = refs.pop(0) if fox else None
    if combine == "diff":
        lam_ref, bd_ref, gsub_ref = refs[:3]
        refs = refs[3:]
    o_ref, m_ref, acc_ref, qs_ref = refs

    step = pl.program_id(1)
    fl = fl_ref[step]
    qi = qi_ref[step]
    kj = kj_ref[step]
    pair = lambda ref, p: ref[:, p * PAIR_LANES:(p + 1) * PAIR_LANES]

    @pl.when((fl & 1) != 0)
    def _():
        m_ref[...] = jnp.full(m_ref.shape, NEG, F32)
        acc_ref[...] = jnp.zeros(acc_ref.shape, F32)
        if not aug:
            for p in range(2):
                q = pair(q_ref, p)
                for r in range(groups):
                    qs_ref[p * groups + r] = q * mask_ref[r:r + 1, :]

    def body(apply_mask):
        if apply_mask:
            row = qi * tq + lax.broadcasted_iota(jnp.int32, (tq, tk), 0)
            col = kj * tk + lax.broadcasted_iota(jnp.int32, (tq, tk), 1)
            valid = col <= row
            if window is not None:
                valid = jnp.logical_and(valid, col > row - window)
        if fox:
            ft = ft_ref[...] * LOG2E
        for p in range(2):
            v1 = jnp.concatenate([pair(v_ref, p), jnp.ones((tk, PAIR_LANES), BF16)], axis=1)
            for r in range(groups):
                g = p * groups + r
                if aug:
                    s = _dot_nt(q_ref[g], k_ref[g])
                else:
                    s = _dot_nt(qs_ref[g], pair(k_ref, p))
                if fox:
                    s = s - ft[g:g + 1, :]
                if apply_mask:
                    s = jnp.where(valid, s, NEG)
                m_prev = m_ref[g]
                m_new = jnp.maximum(m_prev, jnp.max(s, axis=-1, keepdims=True))
                alpha = jnp.exp2(m_prev - m_new)
                pexp = jnp.exp2(s - m_new)
                acc_ref[g] = alpha * acc_ref[g] + jnp.dot(pexp.astype(BF16), v1, preferred_element_type=F32)
                m_ref[g] = m_new

    masked = (fl & 4) != 0
    pl.when(masked)(lambda: body(True))
    pl.when(jnp.logical_not(masked))(lambda: body(False))

    @pl.when((fl & 2) != 0)
    def _():
        lane = lax.broadcasted_iota(jnp.int32, (tq, PAIR_LANES), 1)
        for p in range(2):
            outs = []
            for r in range(groups):
                a = acc_ref[p * groups + r]
                outs.append(a[:, :PAIR_LANES] / a[:, PAIR_LANES:])
            if combine == "diff":
                lam = lam_ref[0]
                o = jnp.where(lane < HEAD_DIM, outs[0] - lam * outs[1], outs[2] - lam * outs[3])
                o = _group_rms(o, bd_ref[...], gsub_ref[...]) * lam_scale
            else:
                o = jnp.where(lane < HEAD_DIM, outs[0], outs[1])
            o_ref[:, p * PAIR_LANES:(p + 1) * PAIR_LANES] = o.astype(o_ref.dtype)


def _lane_masks(bounds):
    lane = np.arange(PAIR_LANES)
    return jnp.asarray(np.stack([(lane >= a) & (lane < b) for a, b in bounds]).astype(np.float32), BF16)


PAIR_MASKS = ((0, 64), (64, 128))
DIFF_MASKS = ((0, 32), (32, 64), (64, 96), (96, 128))


def _flash(q_src, k_src, v_src, *, batch, seq, tq, tk, q_col=None, k_col=None, v_col, aug=False,
           fox_t=None, window=None, diff=None, out_dtype=BF16):
    tq, tk = min(tq, seq), min(tk, seq)
    nq, nk = seq // tq, seq // tk
    qi_t, kj_t, fl_t = _flash_steps(seq, tq, tk, window)
    n_steps = int(qi_t.shape[0]) // 1
    groups = 4 if diff is not None else 2
    combine = "diff" if diff is not None else "pair"

    def tok(use_q, col):
        tile, n_t = (tq, nq) if use_q else (tk, nk)
        return pl.BlockSpec((tile, BR_WIDTH),
                            lambda b, s, qi, kj, fl: (b * n_t + (qi if use_q else kj)[s], col))

    in_specs, args = [], []
    if aug:
        in_specs += [pl.BlockSpec((None, N_HEADS, tq, PAIR_LANES), lambda b, s, qi, kj, fl: (b, 0, qi[s], 0)),
                     pl.BlockSpec((None, N_HEADS, tk, PAIR_LANES), lambda b, s, qi, kj, fl: (b, 0, kj[s], 0))]
    else:
        in_specs += [tok(True, q_col), tok(False, k_col)]
    in_specs.append(tok(False, v_col))
    args += [q_src, k_src, v_src]
    if not aug:
        in_specs.append(_const_spec((groups, PAIR_LANES)))
        args.append(_lane_masks(DIFF_MASKS if diff is not None else PAIR_MASKS))
    if fox_t is not None:
        in_specs.append(pl.BlockSpec((None, 8, tk), lambda b, s, qi, kj, fl: (b, 0, kj[s])))
        args.append(fox_t)
    lam_scale = 1.0
    if diff is not None:
        lam, lam_scale, gsub = diff
        in_specs += [pl.BlockSpec(memory_space=pltpu.SMEM), _const_spec((PAIR_LANES, PAIR_LANES)),
                     _const_spec((1, PAIR_LANES))]
        args += [lam.reshape(1).astype(F32), _group_mean_matrix(PAIR_LANES, HEAD_DIM), gsub.reshape(1, PAIR_LANES)]

    kern = functools.partial(_flash_kernel, aug=aug, groups=groups, tq=tq, tk=tk, fox=fox_t is not None,
                             window=window, combine=combine, lam_scale=lam_scale)
    return pl.pallas_call(
        kern,
        grid_spec=pltpu.PrefetchScalarGridSpec(
            num_scalar_prefetch=3,
            grid=(batch, n_steps),
            in_specs=in_specs,
            out_specs=pl.BlockSpec((tq, BR_WIDTH), lambda b, s, qi, kj, fl: (b * nq + qi[s], 0)),
            scratch_shapes=[pltpu.VMEM((2 * groups, tq, 1), F32),
                            pltpu.VMEM((2 * groups, tq, 2 * PAIR_LANES), F32),
                            pltpu.VMEM((2 * groups, tq, PAIR_LANES), BF16)]),
        out_shape=jax.ShapeDtypeStruct((batch * seq, BR_WIDTH), out_dtype),
        compiler_params=_cparams(("parallel", "arbitrary")),
        name="flash_" + ("aug" if aug else combine) + ("_fox" if fox_t is not None else "")
             + ("_win" if window is not None else ""),
    )(qi_t, kj_t, fl_t, *args)


def _select_bias(rel, j, own, topk):
    big = jnp.int32(1 << 20)
    avail = jnp.logical_and(j >= 0, j < own)
    sel = j == own
    for _ in range(topk):
        val = jnp.where(avail, rel, NEG)
        mx = jnp.max(val, axis=-1, keepdims=True)
        cand = jnp.logical_and(avail, val == mx)
        idx = jnp.min(jnp.where(cand, j, big), axis=-1, keepdims=True)
        pick = j == idx
        sel = jnp.logical_or(sel, pick)
        avail = jnp.logical_and(avail, jnp.logical_not(pick))
    return jnp.where(sel, 0.0, NEG)


def _blockmean_kernel(x_ref, o_ref):
    j = pl.program_id(1)
    o_ref[pl.ds(j, 1), :] = jnp.mean(x_ref[...], axis=0, keepdims=True)


def _blockmean(kv, batch, seq, blk, nb_pad):
    nb = seq // blk
    out = pl.pallas_call(
        _blockmean_kernel,
        grid=(batch, nb),
        in_specs=[pl.BlockSpec((blk, BR_WIDTH), lambda b, j: (b * nb + j, 0))],
        out_specs=pl.BlockSpec((None, nb, BR_WIDTH), lambda b, j: (b, 0, 0)),
        out_shape=jax.ShapeDtypeStruct((batch, nb, BR_WIDTH), F32),
        compiler_params=_cparams(("parallel", "arbitrary")),
        name="blockmean",
    )(kv)
    return jnp.pad(out, ((0, 0), (0, nb_pad - nb), (0, 0)))


def _augment(x, bias_or_onehot, r):
    lane = lax.broadcasted_iota(jnp.int32, x.shape, 1)
    mine = (lane < HEAD_DIM) if r == 0 else (lane >= HEAD_DIM)
    return jnp.where(mine, x.astype(F32), bias_or_onehot).astype(BF16)


def _payload_block_id(shape, r):
    lane = lax.broadcasted_iota(jnp.int32, shape, 1)
    return (lane - HEAD_DIM) if r == 0 else jnp.where(lane < HEAD_DIM, lane, -1)


def _moba_router_kernel(q_ref, k_ref, km_ref, mask_ref, qa_ref, ka_ref, *, tq, blk, nb_pad):
    qi = pl.program_id(2)
    q = q_ref[...]
    k = k_ref[...]
    row = qi * tq + lax.broadcasted_iota(jnp.int32, (tq, 1), 0)
    own = row // blk
    for r in range(2):
        km = (km_ref[...] * mask_ref[r:r + 1, :].astype(F32)).astype(BF16)
        lo = HEAD_DIM if r == 0 else 0
        pieces = [km, jnp.zeros((PAIR_LANES - nb_pad, PAIR_LANES), BF16)]
        if lo:
            pieces = [jnp.zeros((lo, PAIR_LANES), BF16), km, jnp.zeros((PAIR_LANES - lo - nb_pad, PAIR_LANES), BF16)]
        rel = _dot_nt(q * mask_ref[r:r + 1, :], jnp.concatenate(pieces, axis=0))
        j = _payload_block_id((tq, PAIR_LANES), r)
        bias = _select_bias(rel, j, own, MOBA_TOPK)
        qa_ref[r] = _augment(q, bias, r)
        ka_ref[r] = _augment(k, (j == own).astype(F32), r)


def _moba_router(att, kmean, batch, seq, *, tq):
    tq = min(tq, seq)
    nq = seq // tq
    nb_pad = kmean.shape[1]
    out = jax.ShapeDtypeStruct((batch, N_HEADS, seq, PAIR_LANES), BF16)
    aug_spec = pl.BlockSpec((None, 2, tq, PAIR_LANES), lambda b, p, i: (b, p, i, 0))
    return pl.pallas_call(
        functools.partial(_moba_router_kernel, tq=tq, blk=MOBA_BLOCK, nb_pad=nb_pad),
        grid=(batch, 2, nq),
        in_specs=[pl.BlockSpec((tq, PAIR_LANES), lambda b, p, i: (b * nq + i, G_QC * 2 + p)),
                  pl.BlockSpec((tq, PAIR_LANES), lambda b, p, i: (b * nq + i, G_KC * 2 + p)),
                  pl.BlockSpec((None, nb_pad, PAIR_LANES), lambda b, p, i: (b, 0, p)),
                  _const_spec((2, PAIR_LANES))],
        out_specs=(aug_spec, aug_spec),
        out_shape=(out, out),
        compiler_params=_cparams(("parallel", "parallel", "parallel")),
        name="moba_router",
    )(att, att, kmean, _lane_masks(PAIR_MASKS))


def _compress_kernel(c_ref, w1_ref, b1_ref, w2_ref, b2_ref, gain_ref, o_ref, *, rows):
    kv = pl.program_id(0)
    uv = jnp.dot(c_ref[...].astype(BF16), w1_ref[...], preferred_element_type=F32)
    hid = uv.shape[1] // 2
    pre = uv[:, :hid] + pltpu.roll(uv[:, hid:], rows - 1, 0) + b1_ref[...]
    y = jnp.dot(jax.nn.gelu(pre, approximate=True).astype(BF16), w2_ref[...],
                preferred_element_type=F32) + b2_ref[...]
    ms = jnp.mean(y * y, axis=-1, keepdims=True)
    yn = y * lax.rsqrt(ms + EPS) * gain_ref[...]
    o_ref[...] = jnp.where(kv == 0, yn, y)


def _compress(chunks, pos_emb, w1, b1, w2, b2, k_gain, *, chunks_per_seq, seqs_per_tile):
    _, r, cw = chunks.shape
    hid = w1.shape[-1]
    rows = chunks_per_seq * seqs_per_tile
    assert r % rows == 0
    w1ab, bias1, w2, b2, gain = _compress_weights(pos_emb, w1, b1, w2, b2, k_gain)
    return pl.pallas_call(
        functools.partial(_compress_kernel, rows=rows),
        grid=(2, r // rows),
        in_specs=[pl.BlockSpec((None, rows, cw), lambda kv, i: (kv, i, 0)),
                  pl.BlockSpec((None, cw, 2 * hid), lambda kv, i: (kv, 0, 0)),
                  pl.BlockSpec((None, 1, hid), lambda kv, i: (kv, 0, 0)),
                  pl.BlockSpec((None, hid, HEAD_DIM), lambda kv, i: (kv, 0, 0)),
                  pl.BlockSpec((None, 1, HEAD_DIM), lambda kv, i: (kv, 0, 0)),
                  _const_spec((1, HEAD_DIM))],
        out_specs=pl.BlockSpec((None, rows, HEAD_DIM), lambda kv, i: (kv, i, 0)),
        out_shape=jax.ShapeDtypeStruct((2, r, HEAD_DIM), F32),
        compiler_params=_cparams(("parallel", "parallel")),
        name="nsa_compress",
    )(chunks, w1ab, bias1, w2, b2, gain)


def _compress_weights(pos_emb, w1, b1, w2, b2, k_gain):
    cw = w1.shape[1] // 2
    hid = w1.shape[-1]
    w1ab = jnp.concatenate([w1[:, :cw], w1[:, cw:]], axis=-1).astype(BF16)
    bias1 = (jnp.einsum('kf,kfh->kh', pos_emb.reshape(2, -1).astype(F32), w1.astype(F32),
                        precision=lax.Precision.HIGHEST) + b1.astype(F32)).reshape(2, 1, hid)
    return (w1ab, bias1, w2.astype(BF16), b2.astype(F32).reshape(2, 1, HEAD_DIM),
            k_gain.astype(F32).reshape(1, HEAD_DIM))


def _cmp_to_slc_map(n_cmp_pad, n_cmp):
    start = np.arange(n_cmp_pad)[:, None] * NSA_CMP_STRIDE
    blk = np.arange(HEAD_DIM)[None, :] * NSA_SLC_BLOCK
    ov = np.minimum(start + NSA_CMP_LEN, blk + NSA_SLC_BLOCK) - np.maximum(start, blk)
    m = np.maximum(ov, 0).astype(np.float32) / NSA_CMP_LEN
    m[n_cmp:] = 0.0
    z = np.zeros_like(m)
    return jnp.asarray(np.stack([np.concatenate([z, m], 1), np.concatenate([m, z], 1)]), BF16)


def _nsa_cmp_kernel(qn_ref, q_ref, k_ref, kc_ref, vc_ref, map_ref, mask_ref,
                    oc_ref, qa_ref, ka_ref, *, tq, n_pad):
    qi = pl.program_id(2)
    qn = qn_ref[...]
    q = q_ref[...]
    k = k_ref[...]
    kc = kc_ref[...].astype(BF16)
    vc = vc_ref[...].astype(BF16)
    row = qi * tq + lax.broadcasted_iota(jnp.int32, (tq, 1), 0)
    own = row // NSA_SLC_BLOCK
    cmp_end = lax.broadcasted_iota(jnp.int32, (tq, n_pad), 1) * NSA_CMP_STRIDE + (NSA_CMP_LEN - 1)
    visible = cmp_end <= row
    outs = []
    for r in range(2):
        s = jnp.where(visible, _dot_nt(qn * mask_ref[r:r + 1, :], kc), NEG)
        e = jnp.where(visible, jnp.exp2(s - jnp.max(s, axis=-1, keepdims=True)), 0.0)
        p = e / jnp.maximum(jnp.sum(e, axis=-1, keepdims=True), 1e-30)
        p_hi = p.astype(BF16)
        p_lo = (p - p_hi.astype(F32)).astype(BF16)
        outs.append(jnp.dot(p_hi, vc, preferred_element_type=F32))
        rel = (jnp.dot(p_hi, map_ref[r], preferred_element_type=F32)
               + jnp.dot(p_lo, map_ref[r], preferred_element_type=F32))
        j = _payload_block_id((tq, PAIR_LANES), r)
        bias = _select_bias(rel, j, own, NSA_TOPK)
        qa_ref[r] = _augment(q, bias, r)
        ka_ref[r] = _augment(k, (j == own).astype(F32), r)
    lane = lax.broadcasted_iota(jnp.int32, (tq, PAIR_LANES), 1)
    oc_ref[...] = jnp.where(lane < HEAD_DIM, outs[0], outs[1])


def _nsa_cmp(att, cmp_kv, batch, seq, n_cmp, *, tq):
    tq = min(tq, seq)
    nq = seq // tq
    n_pad = cmp_kv.shape[1]
    aug = jax.ShapeDtypeStruct((batch, N_HEADS, seq, PAIR_LANES), BF16)
    aug_spec = pl.BlockSpec((None, 2, tq, PAIR_LANES), lambda b, p, i: (b, p, i, 0))
    tok = lambda g: pl.BlockSpec((tq, PAIR_LANES), lambda b, p, i: (b * nq + i, g * 2 + p))
    return pl.pallas_call(
        functools.partial(_nsa_cmp_kernel, tq=tq, n_pad=n_pad),
        grid=(batch, 2, nq),
        in_specs=[tok(G_QDN), tok(G_QD), tok(G_KDS),
                  pl.BlockSpec((None, n_pad, PAIR_LANES), lambda b, p, i: (b, 0, p)),
                  pl.BlockSpec((None, n_pad, PAIR_LANES), lambda b, p, i: (b, 0, 2 + p)),
                  _const_spec((2, n_pad, PAIR_LANES)), _const_spec((2, PAIR_LANES))],
        out_specs=(pl.BlockSpec((tq, PAIR_LANES), lambda b, p, i: (b * nq + i, p)), aug_spec, aug_spec),
        out_shape=(jax.ShapeDtypeStruct((batch * seq, BR_WIDTH), F32), aug, aug),
        compiler_params=_cparams(("parallel", "parallel", "parallel")),
        name="nsa_cmp",
    )(att, att, att, cmp_kv, cmp_kv, _cmp_to_slc_map(n_pad, n_cmp), _lane_masks(PAIR_MASKS))


def _resident_spec(shape):
    nd = len(shape)
    return pl.BlockSpec(shape, lambda *_: (0,) * nd, pipeline_mode=pl.Buffered(1))


def _gate_expand_matrix():
    m = np.zeros((3, 128, BR_WIDTH), np.float32)
    for c in range(3):
        for h in range(N_HEADS):
            m[c, N_HEADS + 3 * h + c, h * HEAD_DIM:(h + 1) * HEAD_DIM] = 1.0
    return jnp.asarray(m, BF16)


def _merge_kernel(x_ref, hb_ref, oa_ref, ob_ref, oc_ref, odc_ref, ods_ref, odw_ref, small_ref,
                  g1_ref, sc2_ref, sh2_ref, gn2_ref, ex_ref, wbr_ref, wg_ref, bg_ref, wo_ref,
                  xo_ref, h2_ref):
    d = x_ref.shape[1]
    sm = small_ref[...]
    sm_hi = sm.astype(BF16)
    sm_lo = (sm - sm_hi.astype(F32)).astype(BF16)
    o_d = jnp.zeros(odc_ref.shape, F32)
    for c, ref in enumerate((odc_ref, ods_ref, odw_ref)):
        gate = (jnp.dot(sm_hi, ex_ref[c], preferred_element_type=F32)
                + jnp.dot(sm_lo, ex_ref[c], preferred_element_type=F32))
        o_d = o_d + gate * ref[...]
    hb = hb_ref[...]
    branches = (oa_ref[...], ob_ref[...], oc_ref[...], o_d.astype(BF16))
    merged = jnp.zeros((x_ref.shape[0], d), F32)
    for g, o in enumerate(branches):
        br = jnp.dot(o, wbr_ref[g], preferred_element_type=F32)
        z = jnp.dot(hb, wg_ref[:, g * d:(g + 1) * d], preferred_element_type=F32) + bg_ref[:, g * d:(g + 1) * d]
        merged = merged + jax.nn.sigmoid(z) * br
    y = jnp.dot(merged.astype(BF16), wo_ref[...], preferred_element_type=F32)
    x = x_ref[...] + g1_ref[...] * y
    xo_ref[...] = x
    ms = jnp.mean(x * x, axis=-1, keepdims=True)
    h2 = (x * lax.rsqrt(ms + EPS) * gn2_ref[...]) * (1.0 + sc2_ref[...]) + sh2_ref[...]
    h2_ref[...] = h2.astype(BF16)


def _mod_spec(mod, d, tm, rows_per_mod):
    if rows_per_mod == 1:
        return pl.BlockSpec((tm, d), lambda i: (i, 0)), mod
    tpm = rows_per_mod // tm
    return pl.BlockSpec((None, 1, d), lambda i: (i // tpm, 0, 0)), mod.reshape(-1, 1, d)


def _merge(x2d, hb, o_a, o_b, o_c, o_dc, o_ds, o_dw, small, g1, sc2, sh2, gn2,
           w_branch, w_gate, b_gate, w_out, *, rows_per_mod, tm):
    n, d = x2d.shape
    tm = min(tm, n)
    row = lambda w: pl.BlockSpec((tm, w), lambda i: (i, 0))
    mods = [_mod_spec(m, d, tm, rows_per_mod) for m in (g1, sc2, sh2)]
    return pl.pallas_call(
        _merge_kernel,
        grid=(n // tm,),
        in_specs=[row(d), row(d)] + [row(BR_WIDTH)] * 6 + [row(128)] + [m[0] for m in mods]
                 + [_const_spec((1, d)), _const_spec((3, 128, BR_WIDTH)),
                    _resident_spec(w_branch.shape), _resident_spec(w_gate.shape),
                    _const_spec((1, 4 * d)), _resident_spec(w_out.shape)],
        out_specs=(row(d), row(d)),
        out_shape=(jax.ShapeDtypeStruct((n, d), F32), jax.ShapeDtypeStruct((n, d), BF16)),
        compiler_params=_cparams(("parallel",)),
        name="merge",
    )(x2d, hb, o_a, o_b, o_c, o_dc, o_ds, o_dw, small, *[m[1] for m in mods],
      gn2.reshape(1, d).astype(F32), _gate_expand_matrix(),
      w_branch.astype(BF16), w_gate.astype(BF16), b_gate.reshape(1, 4 * d).astype(F32), w_out.astype(BF16))


def _swiglu_acc(h, wg_ref, wu_ref, wd_ref, chunk):
    ff = wg_ref.shape[-1]
    acc = None
    for c0 in range(0, ff, chunk):
        g = jnp.dot(h, wg_ref[:, c0:c0 + chunk], preferred_element_type=F32)
        u = jnp.dot(h, wu_ref[:, c0:c0 + chunk], preferred_element_type=F32)
        part = jnp.dot((g * jax.nn.sigmoid(g) * u).astype(BF16), wd_ref[c0:c0 + chunk, :],
                       preferred_element_type=F32)
        acc = part if acc is None else acc + part
    return acc


def _ffn_kernel(x_ref, h_ref, g2_ref, wg_ref, wu_ref, wd_ref, o_ref, *, chunk):
    o_ref[...] = x_ref[...] + g2_ref[...] * _swiglu_acc(h_ref[...], wg_ref, wu_ref, wd_ref, chunk)


def _ff_chunk(ff):
    for c in (1408, 1024, 896, 512, 256, 128):
        if ff % c == 0:
            return c
    return ff


def _ffn(x2d, h2, g2, wg, wu, wd, *, rows_per_mod, tm):
    n, d = x2d.shape
    tm = min(tm, n)
    row = lambda w: pl.BlockSpec((tm, w), lambda i: (i, 0))
    g2_spec, g2 = _mod_spec(g2, d, tm, rows_per_mod)
    return pl.pallas_call(
        functools.partial(_ffn_kernel, chunk=_ff_chunk(wg.shape[1])),
        grid=(n // tm,),
        in_specs=[row(d), row(d), g2_spec, _resident_spec(wg.shape), _resident_spec(wu.shape),
                  _resident_spec(wd.shape)],
        out_specs=row(d),
        out_shape=jax.ShapeDtypeStruct((n, d), F32),
        compiler_params=_cparams(("parallel",)),
        name="ffn",
    )(x2d, h2, g2, wg.astype(BF16), wu.astype(BF16), wd.astype(BF16))


def _route_top2(logits, n_exp):
    lane = lax.broadcasted_iota(jnp.int32, logits.shape, 1)
    real = lane < n_exp
    logits = jnp.where(real, logits, NEG)
    ex = jnp.exp(logits - jnp.max(logits, axis=-1, keepdims=True))
    prob = ex / jnp.sum(ex, axis=-1, keepdims=True)
    avail = real
    comb = jnp.zeros(prob.shape, F32)
    for _ in range(2):
        val = jnp.where(avail, prob, -1.0)
        mx = jnp.max(val, axis=-1, keepdims=True)
        idx = jnp.min(jnp.where(jnp.logical_and(avail, val == mx), lane, 1 << 20), axis=-1, keepdims=True)
        pick = lane == idx
        comb = jnp.where(pick, prob, comb)
        avail = jnp.logical_and(avail, jnp.logical_not(pick))
    return comb / jnp.sum(comb, axis=-1, keepdims=True)


def _moe_sparse_kernel(h_ref, wr_ref, br_ref, ltri_ref, wg_ref, wu_ref, wd_ref, o_ref,
                       comb_ref, rank_ref, rank_t_ref, comb_t_ref, *, chunk, n_exp, cap):
    e = pl.program_id(1)
    tm = h_ref.shape[0]
    h = h_ref[...]

    @pl.when(e == 0)
    def _():
        comb = _route_top2(jnp.dot(h, wr_ref[...], preferred_element_type=F32) + br_ref[...], n_exp)
        routed = jnp.where(comb > 0.0, 1.0, 0.0)
        rank = jnp.dot(ltri_ref[...], routed.astype(BF16), preferred_element_type=F32)
        comb_ref[...] = comb
        rank_ref[...] = rank
        blk = min(tm, 128)
        for i in range(tm // blk):
            rows = slice(i * blk, (i + 1) * blk)
            rank_t_ref[:, rows] = jnp.transpose(rank[rows, :])
            comb_t_ref[:, rows] = jnp.transpose(comb[rows, :])
        o_ref[...] = jnp.zeros(o_ref.shape, F32)

    lane = lax.broadcasted_iota(jnp.int32, (tm, 128), 1)
    mine = lane == e
    w_col = jnp.sum(jnp.where(mine, comb_ref[...], 0.0), axis=-1, keepdims=True)
    key_col = jnp.where(w_col > 0.0, jnp.sum(jnp.where(mine, rank_ref[...], 0.0), axis=-1, keepdims=True), -1.0)
    key_row = jnp.where(comb_t_ref[pl.ds(e, 1), :] > 0.0, rank_t_ref[pl.ds(e, 1), :], -1.0)
    count = jnp.sum(jnp.where(w_col > 0.0, 1.0, 0.0))

    for c in range(-(-tm // cap)):
        @pl.when(count > c * cap)
        def _(c=c):
            slot_r = (lax.broadcasted_iota(jnp.int32, (cap, tm), 0) + c * cap).astype(F32)
            gather = jnp.where(key_row == slot_r, 1.0, 0.0).astype(BF16)
            hc = jnp.dot(gather, h, preferred_element_type=F32).astype(BF16)
            y = _swiglu_acc(hc, wg_ref, wu_ref, wd_ref, chunk)
            slot_c = (lax.broadcasted_iota(jnp.int32, (tm, cap), 1) + c * cap).astype(F32)
            scatter = jnp.where(key_col == slot_c, 1.0, 0.0).astype(BF16)
            o_ref[...] += w_col * jnp.dot(scatter, y.astype(BF16), preferred_element_type=F32)


def _moe_sparse(h2, router, router_b, wg, wu, wd, *, tm, cap):
    n, d = h2.shape
    n_exp = wg.shape[0]
    tm = min(tm, n)
    cap = min(cap, tm)
    assert n % tm == 0 and tm % min(tm, 128) == 0
    wr = jnp.pad(router.astype(BF16), ((0, 0), (0, 128 - n_exp)))
    br = jnp.pad(router_b.astype(F32), (0, 128 - n_exp)).reshape(1, 128)
    ltri = jnp.asarray(np.tril(np.ones((tm, tm), np.float32), -1), BF16)
    exp_spec = lambda a: pl.BlockSpec((None,) + a.shape[1:], lambda i, e: (e, 0, 0))
    cst = lambda shape: pl.BlockSpec(shape, lambda i, e: (0,) * len(shape))
    return pl.pallas_call(
        functools.partial(_moe_sparse_kernel, chunk=_ff_chunk(wg.shape[2]), n_exp=n_exp, cap=cap),
        grid=(n // tm, n_exp),
        in_specs=[pl.BlockSpec((tm, d), lambda i, e: (i, 0)), cst((d, 128)), cst((1, 128)), cst((tm, tm)),
                  exp_spec(wg), exp_spec(wu), exp_spec(wd)],
        out_specs=pl.BlockSpec((tm, d), lambda i, e: (i, 0)),
        out_shape=jax.ShapeDtypeStruct((n, d), F32),
        scratch_shapes=[pltpu.VMEM((tm, 128), F32), pltpu.VMEM((tm, 128), F32),
                        pltpu.VMEM((128, tm), F32), pltpu.VMEM((128, tm), F32)],
        compiler_params=_cparams(("parallel", "arbitrary")),
        name="moe_sparse",
    )(h2, wr, br, ltri, wg.astype(BF16), wu.astype(BF16), wd.astype(BF16))


def _residual_kernel(x_ref, y_ref, g_ref, o_ref):
    o_ref[...] = x_ref[...] + g_ref[...] * y_ref[...]


def _residual(x2d, y, g, *, rows_per_mod, tm):
    n, d = x2d.shape
    tm = min(tm, n)
    row = pl.BlockSpec((tm, d), lambda i: (i, 0))
    g_spec, g = _mod_spec(g, d, tm, rows_per_mod)
    return pl.pallas_call(
        _residual_kernel, grid=(n // tm,), in_specs=[row, row, g_spec], out_specs=row,
        out_shape=jax.ShapeDtypeStruct((n, d), F32), compiler_params=_cparams(("parallel",)),
        name="residual",
    )(x2d, y, g)


def _moe_kernel(x_ref, h_ref, g2_ref, wr_ref, br_ref, wg_ref, wu_ref, wd_ref, o_ref,
                comb_ref, acc_ref, *, chunk, n_exp):
    e = pl.program_id(1)
    h = h_ref[...]

    @pl.when(e == 0)
    def _():
        logits = jnp.dot(h, wr_ref[...], preferred_element_type=F32) + br_ref[...]
        lane = lax.broadcasted_iota(jnp.int32, logits.shape, 1)
        real = lane < n_exp
        logits = jnp.where(real, logits, NEG)
        ex = jnp.exp(logits - jnp.max(logits, axis=-1, keepdims=True))
        prob = ex / jnp.sum(ex, axis=-1, keepdims=True)
        avail = real
        comb = jnp.zeros(prob.shape, F32)
        for _ in range(2):
            val = jnp.where(avail, prob, -1.0)
            mx = jnp.max(val, axis=-1, keepdims=True)
            idx = jnp.min(jnp.where(jnp.logical_and(avail, val == mx), lane, 1 << 20), axis=-1, keepdims=True)
            pick = lane == idx
            comb = jnp.where(pick, prob, comb)
            avail = jnp.logical_and(avail, jnp.logical_not(pick))
        comb_ref[...] = comb / jnp.sum(comb, axis=-1, keepdims=True)
        acc_ref[...] = jnp.zeros(acc_ref.shape, F32)

    lane = lax.broadcasted_iota(jnp.int32, comb_ref.shape, 1)
    w_e = jnp.sum(jnp.where(lane == e, comb_ref[...], 0.0), axis=-1, keepdims=True)

    @pl.when(jnp.max(w_e) > 0.0)
    def _():
        acc_ref[...] += w_e * _swiglu_acc(h, wg_ref, wu_ref, wd_ref, chunk)

    @pl.when(e == n_exp - 1)
    def _():
        o_ref[...] = x_ref[...] + g2_ref[...] * acc_ref[...]


def _moe(x2d, h2, g2, router, router_b, wg, wu, wd, *, rows_per_mod, tm):
    n, d = x2d.shape
    n_exp = wg.shape[0]
    tm = min(tm, n)
    row = lambda w: pl.BlockSpec((tm, w), lambda i, e: (i, 0))
    if rows_per_mod == 1:
        g2_spec = row(d)
    else:
        tpm = rows_per_mod // tm
        g2_spec = pl.BlockSpec((None, 1, d), lambda i, e: (i // tpm, 0, 0))
        g2 = g2.reshape(-1, 1, d)
    wr = jnp.pad(router.astype(BF16), ((0, 0), (0, 128 - n_exp)))
    br = jnp.pad(router_b.astype(F32), (0, 128 - n_exp)).reshape(1, 128)
    exp_spec = lambda a: pl.BlockSpec((None,) + a.shape[1:], lambda i, e: (e, 0, 0))
    return pl.pallas_call(
        functools.partial(_moe_kernel, chunk=_ff_chunk(wg.shape[2]), n_exp=n_exp),
        grid=(n // tm, n_exp),
        in_specs=[row(d), row(d), g2_spec,
                  pl.BlockSpec((d, 128), lambda i, e: (0, 0)), pl.BlockSpec((1, 128), lambda i, e: (0, 0)),
                  exp_spec(wg), exp_spec(wu), exp_spec(wd)],
        out_specs=row(d),
        out_shape=jax.ShapeDtypeStruct((n, d), F32),
        scratch_shapes=[pltpu.VMEM((tm, 128), F32), pltpu.VMEM((tm, d), F32)],
        compiler_params=_cparams(("parallel", "arbitrary")),
        name="moe",
    )(x2d, h2, g2, wr, br, wg.astype(BF16), wu.astype(BF16), wd.astype(BF16))


def _decode_row_masks(diff):
    lane = np.arange(BR_WIDTH)
    qm = np.zeros((8, BR_WIDTH), np.float32)
    om = np.zeros((8, BR_WIDTH), np.float32)
    for r in range(8 if diff else N_HEADS):
        h = r // 2 if diff else r
        lo, hi = (h * HEAD_DIM + (r % 2) * DIFF_DIM, h * HEAD_DIM + (r % 2 + 1) * DIFF_DIM) if diff \
            else (h * HEAD_DIM, (h + 1) * HEAD_DIM)
        qm[r] = (lane >= lo) & (lane < hi)
        om[r] = (lane >= h * HEAD_DIM) & (lane < (h + 1) * HEAD_DIM)
    return jnp.asarray(qm, BF16), jnp.asarray(om, F32)


def _segment_matrices(n_keys, blk):
    seg = (np.arange(n_keys)[:, None] // blk == np.arange(128)[None, :]).astype(np.float32)
    return jnp.asarray(seg / blk, F32), jnp.asarray(seg.T, BF16)


def _decode_kernel(pt_ref, *refs, n_pg, mode, fox, diff, lam_scale, skip_first, shift_out, n_alias):
    del pt_ref
    refs = list(refs)
    q_ref, kvn_ref = refs[:2]
    pages = refs[2:2 + n_pg]
    refs = refs[2 + n_pg:]
    if fox:
        lf_pages = refs[:n_pg]
        smalln_ref, tri_ref, diag_ref = refs[n_pg:n_pg + 3]
        refs = refs[n_pg + 3:]
    if mode == "sel":
        sel_ref = refs.pop(0)
    qm_ref, om_ref = refs[:2]
    refs = refs[2:]
    if mode == "moba":
        segm_ref, segt_ref = refs[:2]
        refs = refs[2:]
    elif mode == "sel":
        segt_ref = refs.pop(0)
    if diff:
        lam_ref, bd_ref, gsub_ref = refs[:3]
        refs = refs[3:]
    if shift_out:
        newt_ref = refs.pop(0)
        refs = refs[n_alias:]
        o_ref, win_ref = refs
        rows = pages[0].shape[-1]
        lane_s = lax.broadcasted_iota(jnp.int32, newt_ref.shape, 1)
        col = jnp.sum(jnp.where(lane_s == pl.program_id(0), newt_ref[...], 0.0), axis=-1, keepdims=True)
        lane = lax.broadcasted_iota(jnp.int32, (BR_WIDTH, rows), 1)
        for kv in range(2):
            win_ref[kv] = jnp.where(lane == rows - 1, col[kv * BR_WIDTH:(kv + 1) * BR_WIDTH],
                                    pltpu.roll(pages[0][kv], rows - 1, 1))
    else:
        (o_ref,) = refs

    qrows = q_ref[...] * qm_ref[...]
    kn = kvn_ref[:, :BR_WIDTH].astype(BF16).astype(F32)
    vn = kvn_ref[:, BR_WIDTH:].astype(BF16).astype(F32)
    s_new = jnp.sum(qrows.astype(F32) * kn, axis=-1, keepdims=True)
    s = jnp.concatenate([jnp.dot(qrows, pg[0].astype(BF16), preferred_element_type=F32) for pg in pages], axis=1)
    n_keys = s.shape[1]
    if fox:
        carry = jnp.sum(diag_ref[...] * smalln_ref[...], axis=-1, keepdims=True)
        parts = []
        for lf_ref in reversed(lf_pages):
            lf = jnp.concatenate([lf_ref[...], jnp.zeros((8 - N_HEADS, lf_ref.shape[1]), F32)], axis=0)
            parts.append(jnp.dot(lf, tri_ref[...], precision=lax.Precision.HIGHEST,
                                 preferred_element_type=F32) + carry)
            carry = carry + jnp.sum(lf, axis=-1, keepdims=True)
        s = s + jnp.concatenate(parts[::-1], axis=1) * LOG2E
    keep = None
    if mode == "moba":
        rel = jnp.dot(s, segm_ref[...], precision=lax.Precision.HIGHEST, preferred_element_type=F32)
        lane = lax.broadcasted_iota(jnp.int32, rel.shape, 1)
        nb = n_keys // MOBA_BLOCK
        picked = _select_bias(rel, jnp.where(lane < nb, lane, -1), jnp.full((8, 1), nb, jnp.int32), MOBA_TOPK)
        sel = jnp.where(jnp.logical_and(picked == 0.0, lane < nb), 1.0, 0.0).astype(BF16)
        keep = jnp.dot(sel, segt_ref[...], preferred_element_type=F32) > 0.5
    elif mode == "sel":
        keep = jnp.dot(sel_ref[...].astype(BF16), segt_ref[...], preferred_element_type=F32) > 0.5
    if skip_first:
        fresh = lax.broadcasted_iota(jnp.int32, s.shape, 1) >= skip_first
        keep = fresh if keep is None else jnp.logical_and(keep, fresh)
    if keep is not None:
        s = jnp.where(keep, s, NEG)
    m = jnp.maximum(jnp.max(s, axis=-1, keepdims=True), s_new)
    e = jnp.exp2(s - m)
    e_new = jnp.exp2(s_new - m)
    denom = jnp.sum(e, axis=-1, keepdims=True) + e_new
    eb = e.astype(BF16)
    w = pages[0].shape[-1]
    o = e_new.astype(BF16).astype(F32) * vn
    for i, pg in enumerate(pages):
        o = o + _dot_nt(eb[:, i * w:(i + 1) * w], pg[1].astype(BF16))
    o = o / denom
    if diff:
        row = lax.broadcasted_iota(jnp.int32, (8, 1), 0)
        o = o * jnp.where(row % 2 == 0, 1.0, -lam_ref[0])
    out = jnp.sum(o * om_ref[...], axis=0, keepdims=True)
    if diff:
        out = _group_rms(out, bd_ref[...], gsub_ref[...]) * lam_scale
    o_ref[...] = out


def _decode_attn(att3, q_col, kvn3, cache_t, li, page_table, *, mode="plain", fox=None, diff=None,
                 sel=None, blk=None, dense=False, skip_first=0, shift_out=None):
    s_n = att3.shape[0]
    rows = cache_t.shape[-1]
    n_pg = 1 if dense else page_table.shape[1]
    n_keys = n_pg * rows
    pt = page_table.reshape(-1).astype(jnp.int32)
    if dense:
        page_specs = [pl.BlockSpec((None, None, 2, BR_WIDTH, rows), lambda b, pt: (li, b, 0, 0, 0))]
    else:
        page_specs = [pl.BlockSpec((None, None, 2, BR_WIDTH, rows),
                                   lambda b, pt, j=j: (li, pt[b * n_pg + j], 0, 0, 0)) for j in range(n_pg)]
    in_specs = [pl.BlockSpec((None, 1, BR_WIDTH), lambda b, pt: (b, 0, q_col)),
                pl.BlockSpec((None, 1, 2 * BR_WIDTH), lambda b, pt: (b, 0, 0))] + page_specs
    args = [att3, kvn3] + [cache_t] * n_pg
    cst = lambda a: (pl.BlockSpec(a.shape, lambda b, pt: (0,) * a.ndim), a)
    consts = []
    if fox is not None:
        logf_t, small3 = fox
        in_specs += [pl.BlockSpec((None, None, N_HEADS, rows), lambda b, pt, j=j: (li, pt[b * n_pg + j], 0, 0))
                     for j in range(n_pg)]
        args += [logf_t] * n_pg
        in_specs.append(pl.BlockSpec((None, 1, 128), lambda b, pt: (b, 0, 0)))
        args.append(small3)
        consts += [cst(jnp.asarray(np.tril(np.ones((rows, rows), np.float32), -1))),
                   cst(jnp.asarray(np.eye(8, 128, dtype=np.float32) * (np.arange(8)[:, None] < N_HEADS)))]
    qm, om = _decode_row_masks(diff is not None)
    consts += [cst(qm), cst(om)]
    if mode == "moba":
        segm, segt = _segment_matrices(n_keys, MOBA_BLOCK)
        consts += [cst(segm), cst(segt)]
    elif mode == "sel":
        in_specs.append(pl.BlockSpec((None, 8, 128), lambda b, pt: (b, 0, 0)))
        args.append(sel)
        consts.append(cst(_segment_matrices(n_keys, blk)[1]))
    in_specs += [c[0] for c in consts]
    args += [c[1] for c in consts]
    lam_scale = 1.0
    if diff is not None:
        lam, lam_scale, gsub = diff
        bd = _group_mean_matrix(BR_WIDTH, HEAD_DIM)
        in_specs += [pl.BlockSpec(memory_space=pltpu.SMEM), pl.BlockSpec(bd.shape, lambda b, pt: (0, 0)),
                     pl.BlockSpec((1, BR_WIDTH), lambda b, pt: (0, 0))]
        args += [lam.reshape(1).astype(F32), bd, gsub.reshape(1, BR_WIDTH)]
    out_specs = pl.BlockSpec((None, 1, BR_WIDTH), lambda b, pt: (b, 0, 0))
    out_shape = jax.ShapeDtypeStruct((s_n, 1, BR_WIDTH), F32)
    aliases = {}
    if shift_out is not None:
        new_t, prev, depth = shift_out
        in_specs.append(pl.BlockSpec(new_t.shape, lambda b, pt: (0, 0)))
        args.append(new_t)
        if prev is not None:
            aliases = {len(args) + 1: 1}
            in_specs.append(pl.BlockSpec(memory_space=pl.ANY))
            args.append(prev)
        out_specs = (out_specs, pl.BlockSpec((None, None, 2, BR_WIDTH, rows), lambda b, pt: (li, b, 0, 0, 0)))
        out_shape = (out_shape, jax.ShapeDtypeStruct((depth, s_n, 2, BR_WIDTH, rows), F32))
    kern = functools.partial(_decode_kernel, n_pg=n_pg, mode=mode, fox=fox is not None, diff=diff is not None,
                             lam_scale=lam_scale, skip_first=skip_first, shift_out=shift_out is not None,
                             n_alias=len(aliases))
    return pl.pallas_call(
        kern,
        grid_spec=pltpu.PrefetchScalarGridSpec(
            num_scalar_prefetch=1, grid=(s_n,), in_specs=in_specs, out_specs=out_specs),
        out_shape=out_shape,
        input_output_aliases=aliases,
        compiler_params=_cparams(("parallel",)),
        name="decode_" + mode + ("_fox" if fox is not None else "") + ("_diff" if diff is not None else "")
             + ("_dense" if dense else ""),
    )(pt, *args)


def _decode_cmp_kernel(q_ref, kc_ref, vc_ref, map_ref, qm_ref, om_ref, oc_ref, sel_ref, *, n_cmp, own):
    qrows = q_ref[...] * qm_ref[...]
    s = _dot_nt(qrows, kc_ref[...].astype(BF16))
    visible = lax.broadcasted_iota(jnp.int32, s.shape, 1) < n_cmp
    s = jnp.where(visible, s, NEG)
    e = jnp.where(visible, jnp.exp2(s - jnp.max(s, axis=-1, keepdims=True)), 0.0)
    p = e / jnp.maximum(jnp.sum(e, axis=-1, keepdims=True), 1e-30)
    p_hi = p.astype(BF16)
    p_lo = (p - p_hi.astype(F32)).astype(BF16)
    o = jnp.dot(p_hi, vc_ref[...].astype(BF16), preferred_element_type=F32)
    oc_ref[...] = jnp.sum(o * om_ref[...], axis=0, keepdims=True)
    rel = (jnp.dot(p_hi, map_ref[...], preferred_element_type=F32)
           + jnp.dot(p_lo, map_ref[...], preferred_element_type=F32))
    lane = lax.broadcasted_iota(jnp.int32, rel.shape, 1)
    picked = _select_bias(rel, lane, jnp.full((8, 1), own, jnp.int32), NSA_TOPK)
    sel_ref[...] = jnp.where(jnp.logical_and(picked == 0.0, lane < own), 1.0, 0.0)


def _decode_cmp(att3, cmp_kv, n_cmp, own):
    s_n, n_pad, _ = cmp_kv.shape
    assert own <= 128
    start = np.arange(n_pad)[:, None] * NSA_CMP_STRIDE
    blk = np.arange(128)[None, :] * NSA_SLC_BLOCK
    ov = np.minimum(start + NSA_CMP_LEN, blk + NSA_SLC_BLOCK) - np.maximum(start, blk)
    cmap = np.maximum(ov, 0).astype(np.float32) / NSA_CMP_LEN
    cmap[n_cmp:] = 0.0
    qm, om = _decode_row_masks(False)
    return pl.pallas_call(
        functools.partial(_decode_cmp_kernel, n_cmp=n_cmp, own=own),
        grid=(s_n,),
        in_specs=[pl.BlockSpec((None, 1, BR_WIDTH), lambda b: (b, 0, G_QDN)),
                  pl.BlockSpec((None, n_pad, BR_WIDTH), lambda b: (b, 0, 0)),
                  pl.BlockSpec((None, n_pad, BR_WIDTH), lambda b: (b, 0, 1)),
                  _const_spec((n_pad, 128)), _const_spec((8, BR_WIDTH)), _const_spec((8, BR_WIDTH))],
        out_specs=(pl.BlockSpec((None, 1, BR_WIDTH), lambda b: (b, 0, 0)),
                   pl.BlockSpec((None, 8, 128), lambda b: (b, 0, 0))),
        out_shape=(jax.ShapeDtypeStruct((s_n, 1, BR_WIDTH), F32), jax.ShapeDtypeStruct((s_n, 8, 128), F32)),
        compiler_params=_cparams(("parallel",)),
        name="decode_cmp",
    )(att3, cmp_kv, cmp_kv, jnp.asarray(cmap, BF16), qm, om)


def _decode_nsa_cmp_kernel(pt_ref, *refs, n_pg, n_cmp, own):
    del pt_ref
    q_ref = refs[0]
    pages = refs[1:1 + n_pg]
    (w1_ref, b1_ref, w2_ref, b2_ref, gain_ref, map_ref, oc_ref, sel_ref, xs_ref, cst_ref) = refs[1 + n_pg:]
    rows = pages[0].shape[-1]
    n_ch = n_pg * rows // NSA_CMP_STRIDE
    for j, pg in enumerate(pages):
        for kv in range(2):
            for p in range(2):
                xs_ref[kv, p, j * rows:(j + 1) * rows, :] = jnp.transpose(pg[kv, p * PAIR_LANES:(p + 1) * PAIR_LANES, :])
    lane = lax.broadcasted_iota(jnp.int32, (n_ch, PAIR_LANES), 1)
    for kv in range(2):
        for p in range(2):
            for i in range(NSA_CMP_STRIDE // 2):
                x0 = xs_ref[kv, p, pl.ds(2 * i, n_ch, stride=NSA_CMP_STRIDE), :]
                x1 = xs_ref[kv, p, pl.ds(2 * i + 1, n_ch, stride=NSA_CMP_STRIDE), :]
                even = jnp.where(lane < HEAD_DIM, x0, pltpu.roll(x1, HEAD_DIM, 1))
                odd = jnp.where(lane < HEAD_DIM, pltpu.roll(x0, HEAD_DIM, 1), x1)
                cols = slice(i * PAIR_LANES, (i + 1) * PAIR_LANES)
                cst_ref[kv, (2 * p) * n_ch:(2 * p + 1) * n_ch, cols] = even.astype(BF16)
                cst_ref[kv, (2 * p + 1) * n_ch:(2 * p + 2) * n_ch, cols] = odd.astype(BF16)
    toks = []
    for kv in range(2):
        uv = jnp.dot(cst_ref[kv], w1_ref[kv], preferred_element_type=F32)
        hid = uv.shape[1] // 2
        pre = uv[:, :hid] + pltpu.roll(uv[:, hid:], N_HEADS * n_ch - 1, 0) + b1_ref[kv]
        y = jnp.dot(jax.nn.gelu(pre, approximate=True).astype(BF16), w2_ref[kv],
                    preferred_element_type=F32) + b2_ref[kv]
        if kv == 0:
            y = y * lax.rsqrt(jnp.mean(y * y, axis=-1, keepdims=True) + EPS) * gain_ref[...]
        toks.append(y.astype(BF16))
    yk, yv = toks
    q = q_ref[...].astype(F32)
    q8 = jnp.concatenate([q[:, h * HEAD_DIM:(h + 1) * HEAD_DIM] for h in range(N_HEADS)]
                         + [jnp.zeros((8 - N_HEADS, HEAD_DIM), F32)], axis=0).astype(BF16)
    s_all = _dot_nt(q8, yk)
    row = lax.broadcasted_iota(jnp.int32, (8, n_ch), 0)
    s = jnp.zeros((8, n_ch), F32)
    for h in range(N_HEADS):
        s = s + jnp.where(row == h, s_all[:, h * n_ch:(h + 1) * n_ch], 0.0)
    visible = lax.broadcasted_iota(jnp.int32, (8, n_ch), 1) < n_cmp
    s = jnp.where(visible, s, NEG)
    e = jnp.where(visible, jnp.exp2(s - jnp.max(s, axis=-1, keepdims=True)), 0.0)
    p = e / jnp.maximum(jnp.sum(e, axis=-1, keepdims=True), 1e-30)
    p_hi = p.astype(BF16)
    p_lo = (p - p_hi.astype(F32)).astype(BF16)
    p_all = jnp.concatenate([jnp.where(row == h, p_hi, jnp.zeros_like(p_hi)) for h in range(N_HEADS)], axis=1)
    o4 = jnp.dot(p_all, yv, preferred_element_type=F32)
    oc_ref[...] = jnp.concatenate([o4[h:h + 1, :] for h in range(N_HEADS)], axis=1)
    rel = (jnp.dot(p_hi, map_ref[...], preferred_element_type=F32)
           + jnp.dot(p_lo, map_ref[...], preferred_element_type=F32))
    lane_b = lax.broadcasted_iota(jnp.int32, rel.shape, 1)
    picked = _select_bias(rel, lane_b, jnp.full((8, 1), own, jnp.int32), NSA_TOPK)
    sel_ref[...] = jnp.where(jnp.logical_and(picked == 0.0, lane_b < own), 1.0, 0.0)


def _decode_nsa_cmp(att3, cache_t, li, page_table, cmpw, n_cmp, own):
    s_n = att3.shape[0]
    rows = cache_t.shape[-1]
    n_pg = page_table.shape[1]
    n_ch = n_pg * rows // NSA_CMP_STRIDE
    assert own <= 128
    w1ab, bias1, w2, b2, gain = _compress_weights(*cmpw)
    start = np.arange(n_ch)[:, None] * NSA_CMP_STRIDE
    blk = np.arange(128)[None, :] * NSA_SLC_BLOCK
    ov = np.minimum(start + NSA_CMP_LEN, blk + NSA_SLC_BLOCK) - np.maximum(start, blk)
    cmap = np.maximum(ov, 0).astype(np.float32) / NSA_CMP_LEN
    cmap[n_cmp:] = 0.0
    pt = page_table.reshape(-1).astype(jnp.int32)
    cst = lambda a: pl.BlockSpec(a.shape, lambda b, pt: (0,) * a.ndim)
    consts = [w1ab, bias1, w2, b2, gain, jnp.asarray(cmap, BF16)]
    return pl.pallas_call(
        functools.partial(_decode_nsa_cmp_kernel, n_pg=n_pg, n_cmp=n_cmp, own=own),
        grid_spec=pltpu.PrefetchScalarGridSpec(
            num_scalar_prefetch=1, grid=(s_n,),
            in_specs=[pl.BlockSpec((None, 1, BR_WIDTH), lambda b, pt: (b, 0, G_QDN))]
                     + [pl.BlockSpec((None, None, 2, BR_WIDTH, rows),
                                     lambda b, pt, j=j: (li, pt[b * n_pg + j], 0, 0, 0)) for j in range(n_pg)]
                     + [cst(a) for a in consts],
            out_specs=(pl.BlockSpec((None, 1, BR_WIDTH), lambda b, pt: (b, 0, 0)),
                       pl.BlockSpec((None, 8, 128), lambda b, pt: (b, 0, 0))),
            scratch_shapes=[pltpu.VMEM((2, 2, n_pg * rows, PAIR_LANES), F32),
                            pltpu.VMEM((2, N_HEADS * n_ch, NSA_CMP_STRIDE * HEAD_DIM), BF16)]),
        out_shape=(jax.ShapeDtypeStruct((s_n, 1, BR_WIDTH), F32), jax.ShapeDtypeStruct((s_n, 8, 128), F32)),
        compiler_params=_cparams(("parallel",)),
        name="decode_nsa_cmp",
    )(pt, att3, *([cache_t] * n_pg), *consts)


def _sample_attention(att, kvs, small, caches_t, logf_t, win_t, cmp_t, page_table, li, diff, cmpw, past_len,
                      win_prev):
    s_n = att.shape[0]
    kva, kvb, kvc, kvds, kvdw = (a.reshape(s_n, 1, 2 * BR_WIDTH) for a in kvs)
    att3 = att.reshape(s_n, 1, -1)
    ca, cb, cc, cds = caches_t
    dec = functools.partial(_decode_attn, att3, li=li, page_table=page_table)
    o_a = dec(G_QA, kva, ca, fox=(logf_t, small.reshape(s_n, 1, 128)))
    o_b = dec(G_QB, kvb, cb, diff=diff)
    o_c = dec(G_QC, kvc, cc, mode="moba")
    n_cmp = (past_len + 1 - NSA_CMP_LEN) // NSA_CMP_STRIDE + 1
    o_dc, sel = _decode_nsa_cmp(att3, cmp_t, li, page_table, cmpw, n_cmp, past_len // NSA_SLC_BLOCK)
    o_ds = dec(G_QD, kvds, cds, mode="sel", sel=sel, blk=NSA_SLC_BLOCK)
    buf_len = win_t.shape[-1]
    o_dw, win_next = dec(G_QD, kvdw, win_t, dense=True, skip_first=max(0, buf_len - NSA_WINDOW + 1),
                         shift_out=(jnp.transpose(kvs[4]), win_prev, win_t.shape[0]))
    flat = lambda a, dt: a.reshape(s_n, BR_WIDTH).astype(dt)
    return (flat(o_a, BF16), flat(o_b, BF16), flat(o_c, BF16), flat(o_dc, F32), flat(o_ds, F32),
            flat(o_dw, F32)), win_next


def _layer_gains(fox_qnorm, fox_knorm, diff_qnorm, diff_knorm, moba_qnorm, moba_knorm, nsa_qnorm, nsa_knorm):
    return jnp.stack([_tile_gain(fox_qnorm, HEAD_DIM), _tile_gain(fox_knorm, HEAD_DIM),
                      _tile_gain(diff_qnorm, DIFF_DIM), _tile_gain(diff_knorm, DIFF_DIM),
                      _tile_gain(moba_qnorm, HEAD_DIM), _tile_gain(moba_knorm, HEAD_DIM),
                      _tile_gain(nsa_qnorm, HEAD_DIM), _tile_gain(nsa_knorm[1], HEAD_DIM),
                      _tile_gain(nsa_knorm[2], HEAD_DIM)])


def _chunk_rows(kv5):
    b, l = kv5.shape[:2]
    m = l // NSA_CMP_STRIDE
    c = kv5.reshape(b, m, NSA_CMP_STRIDE, 2, N_HEADS, HEAD_DIM)
    return jnp.transpose(c, (3, 0, 4, 1, 2, 5)).reshape(2, b * N_HEADS * m, NSA_CMP_STRIDE * HEAD_DIM)


def _unchunk_tokens(tok, b, m):
    return jnp.transpose(tok.reshape(2, b, N_HEADS, m, HEAD_DIM), (1, 3, 0, 2, 4)).reshape(b, m, 2 * BR_WIDTH)


def _chunk_rows_t(kvt):
    b, _, _, l = kvt.shape
    m = l // NSA_CMP_STRIDE
    c = kvt.reshape(b, 2, N_HEADS, HEAD_DIM, m, NSA_CMP_STRIDE)
    return jnp.transpose(c, (1, 0, 2, 4, 5, 3)).reshape(2, b * N_HEADS * m, NSA_CMP_STRIDE * HEAD_DIM)


def _prompt_attention(att, kmean, kvdc_t, small, diff, cmpw, batch, seq, *, tq=512, tk=512):
    fl = functools.partial(_flash, batch=batch, seq=seq, tq=tq, tk=tk)
    ft8 = _cumsum_t(small, batch, seq)
    o_a = fl(att, att, att, q_col=G_QA, k_col=G_KA, v_col=G_VA, fox_t=ft8)
    o_b = fl(att, att, att, q_col=G_QB, k_col=G_KB, v_col=G_VB, diff=diff)
    qa_c, ka_c = _moba_router(att, kmean, batch, seq, tq=tq)
    o_c = fl(qa_c, ka_c, att, v_col=G_VC, aug=True)
    m = seq // NSA_CMP_STRIDE
    pos_emb, w1, b1, w2, b2, k_gain = cmpw
    tok = _compress(_chunk_rows_t(kvdc_t), pos_emb, w1, b1, w2, b2, k_gain,
                    chunks_per_seq=m, seqs_per_tile=max(1, 512 // m))
    cmp_kv = _unchunk_tokens(tok, batch, m)
    o_dc, qa_d, ka_d = _nsa_cmp(att, cmp_kv, batch, seq, m - 1, tq=tq)
    o_ds = fl(qa_d, ka_d, att, v_col=G_VDS, aug=True, out_dtype=F32)
    o_dw = fl(att, att, att, q_col=G_QD, k_col=G_KDW, v_col=G_VDW, window=NSA_WINDOW, out_dtype=F32)
    return o_a, o_b, o_c, o_dc, o_ds, o_dw


def _trunk_tail(x2d, hb, branch, small, g1, sc2, sh2, g2, gn2, w_branch, w_gate, b_gate, w_out, ffn,
                *, rows_per_mod, tm=512):
    x2d, h2 = _merge(x2d, hb, *branch, small, g1, sc2, sh2, gn2, w_branch, w_gate, b_gate, w_out,
                     rows_per_mod=rows_per_mod, tm=tm)
    if ffn[0] == "dense":
        return _ffn(x2d, h2, g2, *ffn[1:], rows_per_mod=rows_per_mod, tm=tm)
    y = _moe_sparse(h2, *ffn[1:], tm=2 * tm, cap=320)
    return _residual(x2d, y, g2, rows_per_mod=rows_per_mod, tm=tm)


def kernel(x_prompt, x_sample, c_prompt, c_sample, cache_a_kv, cache_a_logf, cache_b_kv, cache_c_kv, cache_d_cmp_kv, cache_d_slc_kv, state_d_win_kv, page_table, ada_w, ada_b, norm_attn, norm_ffn, w_in, fox_fbias, fox_qnorm, fox_knorm, diff_qnorm, diff_knorm, diff_lambda, diff_subnorm, moba_qnorm, moba_knorm, nsa_qnorm, nsa_knorm, nsa_cmp_pos, nsa_cmp_w1, nsa_cmp_b1, nsa_cmp_w2, nsa_cmp_b2, w_branch, w_gate, b_gate, w_out, ffn_w_gate, ffn_w_up, ffn_w_down, moe_router, moe_router_b, moe_w_gate, moe_w_up, moe_w_down):
    bp, t, d = x_prompt.shape
    bs = x_sample.shape[0]
    depth = w_in.shape[0]
    n_pages = page_table.shape[1]
    past_len = n_pages * cache_a_kv.shape[2]
    pos_p = jnp.arange(t, dtype=jnp.int32)
    pos_s = jnp.full((1,), past_len, jnp.int32)
    xp = x_prompt.reshape(bp * t, d)
    xs = x_sample.reshape(bs, d)
    c_all = jnp.concatenate([c_prompt, c_sample, jnp.zeros((-(bp + bs) % 8, d), F32)], axis=0)
    page_t = lambda c: jnp.transpose(c, (0, 1, 3, 4, 5, 2)).reshape(c.shape[:2] + (2, BR_WIDTH, c.shape[2]))
    caches_t = tuple(page_t(c) for c in (cache_a_kv, cache_b_kv, cache_c_kv, cache_d_slc_kv))
    logf_t = jnp.swapaxes(cache_a_logf, 2, 3)
    win_t = page_t(state_d_win_kv)
    cmp_t = page_t(cache_d_cmp_kv)
    kv_p = None
    logf_p = []
    st_s = [[] for _ in range(6)]
    win_s = None
    for li in range(depth):
        lam_init = 0.8 - 0.6 * math.exp(-0.3 * li)
        lq = diff_lambda[li].astype(F32)
        lam = jnp.exp(jnp.sum(lq[0] * lq[1])) - jnp.exp(jnp.sum(lq[2] * lq[3])) + lam_init
        diff = (lam, 1.0 - lam_init, jnp.tile(diff_subnorm[li].astype(F32), 2))
        cmpw = (nsa_cmp_pos[li], nsa_cmp_w1[li], nsa_cmp_b1[li], nsa_cmp_w2[li], nsa_cmp_b2[li], nsa_knorm[li][0])
        mod = _rowmm(c_all, ada_w[li], ada_b[li], silu_in=True)
        modp = mod[:bp].reshape(bp, 6, d)
        wm, ws, bsm = _split_w_in(w_in[li], fox_fbias[li])
        gains = _layer_gains(fox_qnorm[li], fox_knorm[li], diff_qnorm[li], diff_knorm[li], moba_qnorm[li],
                             moba_knorm[li], nsa_qnorm[li], nsa_knorm[li])
        tm_p = 512
        hb, att, *kv_p, small, km = _inproj(
            xp, modp[:, 1], modp[:, 0], norm_attn[li], wm, ws, bsm, gains, pos_p, rows_per_mod=t, tm=tm_p,
            stacked=(li, depth, kv_p))
        nb = t // MOBA_BLOCK
        kmean = km[:, :tm_p // MOBA_BLOCK].reshape(bp, nb, BR_WIDTH)
        kmean = jnp.pad(kmean, ((0, 0), (0, -nb % 8), (0, 0)))
        branch = _prompt_attention(att, kmean, kv_p[3][li], small, diff, cmpw, bp, t)
        j = li // 2
        if li % 2 == 0:
            ffn = ("dense", ffn_w_gate[j], ffn_w_up[j], ffn_w_down[j])
        else:
            ffn = ("moe", moe_router[j], moe_router_b[j], moe_w_gate[j], moe_w_up[j], moe_w_down[j])
        xp = _trunk_tail(xp, hb, branch, small, modp[:, 2], modp[:, 4], modp[:, 3], modp[:, 5], norm_ffn[li],
                         w_branch[li], w_gate[li], b_gate[li], w_out[li], ffn, rows_per_mod=t)
        logf_p.append(small[:, :N_HEADS].reshape(bp, t, N_HEADS))
        mods = mod[bp:bp + bs].reshape(bs, 6, d)
        hb, att, kva, kvb, kvc, kvdc, kvds, kvdw, small = _inproj(
            xs, mods[:, 1], mods[:, 0], norm_attn[li], wm, ws, bsm, gains, pos_s, rows_per_mod=1, tm=bs)
        diff_s = (lam, 1.0 - lam_init, jnp.tile(diff_subnorm[li].astype(F32), N_HEADS))
        branch, win_s = _sample_attention(att, (kva, kvb, kvc, kvds, kvdw), small, caches_t, logf_t, win_t, cmp_t,
                                          page_table, li, diff_s, cmpw, past_len, win_s)
        xs = _trunk_tail(xs, hb, branch, small, mods[:, 2], mods[:, 4], mods[:, 3], mods[:, 5], norm_ffn[li],
                         w_branch[li], w_gate[li], b_gate[li], w_out[li], ffn, rows_per_mod=1)
        kv6 = lambda a: a.reshape(bs, 1, 2, N_HEADS, HEAD_DIM)
        for lst, val in zip(st_s, (kv6(kva), small[:, :N_HEADS].reshape(bs, 1, N_HEADS), kv6(kvb), kv6(kvc),
                                   kv6(kvdc), kv6(kvds))):
            lst.append(val)
    untr = lambda a: jnp.transpose(a.reshape(a.shape[:3] + (N_HEADS, HEAD_DIM, a.shape[-1])), (0, 1, 5, 2, 3, 4))
    win = min(NSA_WINDOW, t)
    kva_p, kvb_p, kvc_p, kvdc_p, kvds_p, kvdw_p = kv_p
    out_p = (untr(kva_p), jnp.stack(logf_p, axis=0), untr(kvb_p), untr(kvc_p), untr(kvdc_p), untr(kvds_p),
             untr(kvdw_p[..., t - win:]))
    out_s = tuple(jnp.stack(lst, axis=0) for lst in st_s) + (untr(win_s),)
    return (xp.reshape(bp, t, d), xs.reshape(bs, 1, d)) + out_p + out_s
```

```python
import functools
import math

import jax
import jax.numpy as jnp
import numpy as np
from jax import lax
from jax.experimental import pallas as pl
from jax.experimental.pallas import tpu as pltpu

F32 = jnp.float32
BF16 = jnp.bfloat16

N_HEADS = 4
HEAD_DIM = 64
BR_WIDTH = N_HEADS * HEAD_DIM
PAIR_LANES = 2 * HEAD_DIM
DIFF_DIM = HEAD_DIM // 2
ROPE_THETA = 500000.0
MOBA_BLOCK = 256
MOBA_TOPK = 3
NSA_CMP_LEN = 32
NSA_CMP_STRIDE = 16
NSA_SLC_BLOCK = 64
NSA_TOPK = 4
NSA_WINDOW = 512
N_EXPERTS = 8
EPS = 1e-6
NEG = -1e30
LOG2E = math.log2(math.e)
DECODE_SAMPLES_PER_STEP = 2
V7X_VMEM_BYTES = 64 * 1024 * 1024
VMEM_LIMIT = V7X_VMEM_BYTES - 16 * 1024 * 1024

(G_QA, G_KA, G_VA, G_QB, G_KB, G_VB, G_QC, G_KC, G_VC,
 G_QDN, G_QD, G_KDS, G_VDS, G_KDW, G_VDW) = range(15)
N_ATT_GROUPS = 15


def _cparams(sem):
    return pltpu.CompilerParams(dimension_semantics=sem, vmem_limit_bytes=VMEM_LIMIT)


def _const_spec(shape):
    nd = len(shape)
    return pl.BlockSpec(shape, lambda *_: (0,) * nd)


def _rowmm_kernel(x_ref, w_ref, b_ref, o_ref, *, silu_in):
    x = x_ref[...]
    if silu_in:
        x = x * jax.nn.sigmoid(x)
    o_ref[...] = jnp.dot(x.astype(BF16), w_ref[...].astype(BF16),
                         preferred_element_type=F32) + b_ref[...]


def _rowmm(x, w, b, *, silu_in=False, tn=1024):
    m, k = x.shape
    n = w.shape[1]
    tn = math.gcd(tn, n)
    return pl.pallas_call(
        functools.partial(_rowmm_kernel, silu_in=silu_in),
        grid=(n // tn,),
        in_specs=[pl.BlockSpec((m, k), lambda j: (0, 0)),
                  pl.BlockSpec((k, tn), lambda j: (0, j)),
                  pl.BlockSpec((1, tn), lambda j: (0, j))],
        out_specs=pl.BlockSpec((m, tn), lambda j: (0, j)),
        out_shape=jax.ShapeDtypeStruct((m, n), F32),
        compiler_params=_cparams(("parallel",)),
        name="rowmm",
    )(x, w, b.reshape(1, n))


def _rope_tables(pos, group):
    r = group // 4
    half = r // 2
    inv = ROPE_THETA ** (-(np.arange(half, dtype=np.float32) / half))
    lane = np.arange(BR_WIDTH)
    j = lane % group
    ang = pos.astype(F32)[:, None] * jnp.asarray(inv[j % half], F32)[None, :]
    cos = jnp.where(j[None, :] < r, jnp.cos(ang), 1.0)
    sin = jnp.sin(ang)
    sin_a = jnp.where(j[None, :] < half, -sin, 0.0)
    sin_b = jnp.where((j[None, :] >= half) & (j[None, :] < r), sin, 0.0)
    return cos.astype(F32), sin_a.astype(F32), sin_b.astype(F32)


def _group_mean_matrix(width, group):
    i = np.arange(width)
    return jnp.asarray((i[:, None] // group == i[None, :] // group).astype(np.float32) / group, BF16)


def _group_rms(a, bd, gain):
    ms = jnp.dot((a * a).astype(BF16), bd, preferred_element_type=F32)
    return a * lax.rsqrt(ms + EPS) * gain


def _rope(a, cos, sin_a, sin_b, half):
    w = a.shape[-1]
    return a * cos + pltpu.roll(a, w - half, 1) * sin_a + pltpu.roll(a, half, 1) * sin_b


def _log_sigmoid(x):
    return jnp.minimum(x, 0.0) - jnp.log1p(jnp.exp(-jnp.abs(x)))


def _inproj_kernel(x_ref, sc_ref, sh_ref, gn_ref, w_ref, ws_ref, bs_ref, gains_ref,
                   bd64_ref, bd32_ref, c64_ref, sa64_ref, sb64_ref, c32_ref, sa32_ref, sb32_ref,
                   *rest, transposed, n_alias):
    rest = rest[n_alias:]
    h_ref, att_ref, kva_ref, kvb_ref, kvc_ref, kvdc_ref, kvds_ref, kvdw_ref, small_ref = rest[:9]

    def put_kv(ref, k, v):
        if transposed:
            stage_ref = rest[-1]
            stage_ref[0] = k
            stage_ref[1] = v
            ref[0] = jnp.transpose(stage_ref[0])
            ref[1] = jnp.transpose(stage_ref[1])
        else:
            ref[:, :BR_WIDTH] = k
            ref[:, BR_WIDTH:] = v

    x = x_ref[...]
    ms = jnp.mean(x * x, axis=-1, keepdims=True)
    h = (x * lax.rsqrt(ms + EPS) * gn_ref[...]) * (1.0 + sc_ref[...]) + sh_ref[...]
    hb = h.astype(BF16)
    h_ref[...] = hb
    bd64 = bd64_ref[...]
    bd32 = bd32_ref[...]

    def proj(g):
        return jnp.dot(hb, w_ref[:, g * BR_WIDTH:(g + 1) * BR_WIDTH], preferred_element_type=F32)

    def gain(i):
        return gains_ref[i:i + 1, :]

    def rope64(a):
        return _rope(a, c64_ref[...], sa64_ref[...], sb64_ref[...], 8)

    def rope32(a):
        return _rope(a, c32_ref[...], sa32_ref[...], sb32_ref[...], 4)

    def att(g, a):
        att_ref[:, g * BR_WIDTH:(g + 1) * BR_WIDTH] = a.astype(BF16)

    sm_scale = HEAD_DIM ** -0.5 * LOG2E
    att(G_QA, _group_rms(proj(0), bd64, gain(0)) * sm_scale)
    k = _group_rms(proj(1), bd64, gain(1))
    v = proj(2)
    put_kv(kva_ref, k, v)
    att(G_KA, k)
    att(G_VA, v)
    att(G_QB, rope32(_group_rms(proj(3), bd32, gain(2))) * (DIFF_DIM ** -0.5 * LOG2E))
    k = rope32(_group_rms(proj(4), bd32, gain(3)))
    v = proj(5)
    put_kv(kvb_ref, k, v)
    att(G_KB, k)
    att(G_VB, v)
    att(G_QC, rope64(_group_rms(proj(6), bd64, gain(4))) * sm_scale)
    k = rope64(_group_rms(proj(7), bd64, gain(5)))
    v = proj(8)
    put_kv(kvc_ref, k, v)
    if transposed:
        km_ref = rest[9]
        nb = k.shape[0] // MOBA_BLOCK
        km_ref[...] = jnp.concatenate(
            [jnp.mean(k[b * MOBA_BLOCK:(b + 1) * MOBA_BLOCK], axis=0, keepdims=True) for b in range(nb)]
            + [jnp.zeros((8 - nb, BR_WIDTH), F32)], axis=0)
    att(G_KC, k)
    att(G_VC, v)
    qn = _group_rms(proj(9), bd64, gain(6))
    att(G_QDN, qn * sm_scale)
    att(G_QD, rope64(qn) * sm_scale)
    put_kv(kvdc_ref, proj(10), proj(11))
    k = rope64(_group_rms(proj(12), bd64, gain(7)))
    v = proj(13)
    put_kv(kvds_ref, k, v)
    att(G_KDS, k)
    att(G_VDS, v)
    k = rope64(_group_rms(proj(14), bd64, gain(8)))
    v = proj(15)
    put_kv(kvdw_ref, k, v)
    att(G_KDW, k)
    att(G_VDW, v)
    z = jnp.dot(hb, ws_ref[...], preferred_element_type=F32) + bs_ref[...]
    lane = lax.broadcasted_iota(jnp.int32, z.shape, 1)
    small_ref[...] = jnp.where(lane < N_HEADS, _log_sigmoid(z), jax.nn.sigmoid(z))


def _split_w_in(w_in, fox_fbias):
    d = w_in.shape[0]
    c0 = 3 * BR_WIDTH
    main = jnp.concatenate([w_in[:, :c0], w_in[:, c0 + N_HEADS:c0 + N_HEADS + 13 * BR_WIDTH]], axis=1)
    small = jnp.concatenate([w_in[:, c0:c0 + N_HEADS], w_in[:, c0 + N_HEADS + 13 * BR_WIDTH:],
                             jnp.zeros((d, 128 - 4 * N_HEADS), w_in.dtype)], axis=1)
    bias = jnp.concatenate([fox_fbias.astype(F32), jnp.zeros((128 - N_HEADS,), F32)]).reshape(1, 128)
    return main.astype(BF16), small.astype(BF16), bias


def _inproj(x2d, sc, sh, gn, w_main, w_small, b_small, gains, pos, *, rows_per_mod, tm, stacked=None):
    n, d = x2d.shape
    tm = min(tm, n)
    assert n % tm == 0
    per_tok = rows_per_mod == 1
    if not per_tok:
        assert rows_per_mod % tm == 0
    tabs = _rope_tables(pos, HEAD_DIM) + _rope_tables(pos, DIFF_DIM)
    single_pos = pos.shape[0] == 1
    tiles_per_seq = 1 if single_pos else pos.shape[0] // tm

    if per_tok:
        mod_spec = pl.BlockSpec((tm, d), lambda i: (i, 0))
    else:
        tpm = rows_per_mod // tm
        mod_spec = pl.BlockSpec((None, 1, d), lambda i: (i // tpm, 0, 0))
        sc, sh = sc.reshape(-1, 1, d), sh.reshape(-1, 1, d)
    if single_pos:
        tab_spec = pl.BlockSpec((1, BR_WIDTH), lambda i: (0, 0))
    else:
        tab_spec = pl.BlockSpec((tm, BR_WIDTH), lambda i: (i % tiles_per_seq, 0))

    row = lambda w: pl.BlockSpec((tm, w), lambda i: (i, 0))
    in_specs = [row(d), mod_spec, mod_spec, _const_spec((1, d)),
                _const_spec(w_main.shape), _const_spec(w_small.shape), _const_spec((1, 128)),
                _const_spec(gains.shape),
                _const_spec((BR_WIDTH, BR_WIDTH)), _const_spec((BR_WIDTH, BR_WIDTH))] + [tab_spec] * 6
    args = [x2d, sc, sh, gn.reshape(1, d), w_main, w_small, b_small, gains,
            _group_mean_matrix(BR_WIDTH, HEAD_DIM), _group_mean_matrix(BR_WIDTH, DIFF_DIM), *tabs]
    aliases = {}
    if stacked is None:
        kv_shapes = (jax.ShapeDtypeStruct((n, 2 * BR_WIDTH), F32),) * 6
        kv_specs = (row(2 * BR_WIDTH),) * 6
        extra_shapes, extra_specs = (), ()
    else:
        li, depth, prev = stacked
        seq = pos.shape[0]
        batch = n // seq
        assert tm % MOBA_BLOCK == 0
        kv_shapes = (jax.ShapeDtypeStruct((depth, batch, 2, BR_WIDTH, seq), F32),) * 6
        kv_specs = (pl.BlockSpec((None, None, 2, BR_WIDTH, tm),
                                 lambda i: (li, i // tiles_per_seq, 0, 0, i % tiles_per_seq)),) * 6
        extra_shapes = (jax.ShapeDtypeStruct((n // tm, 8, BR_WIDTH), F32),)
        extra_specs = (pl.BlockSpec((None, 8, BR_WIDTH), lambda i: (i, 0, 0)),)
        if prev is not None:
            aliases = {len(args) + k: 2 + k for k in range(6)}
            in_specs += [pl.BlockSpec(memory_space=pl.ANY)] * 6
            args += list(prev)
    out_shape = ((jax.ShapeDtypeStruct((n, d), BF16), jax.ShapeDtypeStruct((n, N_ATT_GROUPS * BR_WIDTH), BF16))
                 + kv_shapes + (jax.ShapeDtypeStruct((n, 128), F32),) + extra_shapes)
    out_specs = (row(d), row(N_ATT_GROUPS * BR_WIDTH)) + kv_specs + (row(128),) + extra_specs
    return pl.pallas_call(
        functools.partial(_inproj_kernel, transposed=stacked is not None, n_alias=len(aliases)),
        grid=(n // tm,),
        in_specs=in_specs,
        out_specs=out_specs,
        out_shape=out_shape,
        scratch_shapes=[pltpu.VMEM((2, tm, BR_WIDTH), F32)] if stacked is not None else [],
        input_output_aliases=aliases,
        compiler_params=_cparams(("parallel",)),
        name="inproj",
    )(*args)


def _tile_gain(g, group):
    return jnp.tile(g.astype(F32), BR_WIDTH // group)


def _cumsum_kernel(x_ref, tri_ref, ft_ref, carry_ref):
    @pl.when(pl.program_id(1) == 0)
    def _():
        carry_ref[...] = jnp.zeros_like(carry_ref)

    c = jnp.dot(tri_ref[...], x_ref[...], precision=lax.Precision.HIGHEST,
                preferred_element_type=F32) + carry_ref[...]
    carry_ref[...] = c[-1:, :]
    ft_ref[...] = jnp.transpose(c)[:8, :]


def _cumsum_t(small, batch, seq, *, tc=256):
    tc = min(tc, seq)
    nt = seq // tc
    tri = jnp.asarray(np.tril(np.ones((tc, tc), np.float32)))
    return pl.pallas_call(
        _cumsum_kernel,
        grid=(batch, nt),
        in_specs=[pl.BlockSpec((tc, 128), lambda b, t: (b * nt + t, 0)), _const_spec((tc, tc))],
        out_specs=pl.BlockSpec((None, 8, tc), lambda b, t: (b, 0, t)),
        out_shape=jax.ShapeDtypeStruct((batch, 8, seq), F32),
        scratch_shapes=[pltpu.VMEM((1, 128), F32)],
        compiler_params=_cparams(("parallel", "arbitrary")),
        name="cumsum_t",
    )(small, tri)


def _flash_steps(seq, tq, tk, window):
    assert tk % tq == 0 and seq % tk == 0
    qi_l, kj_l, fl_l = [], [], []
    for qi in range(seq // tq):
        t0, t1 = qi * tq, (qi + 1) * tq - 1
        js = []
        for j in range(t0 // tk, -1, -1):
            s0, s1 = j * tk, (j + 1) * tk - 1
            if window is not None and s1 <= t0 - window:
                break
            partial = s1 > t0 or (window is not None and s0 <= t1 - window)
            js.append((j, partial))
        for n, (j, partial) in enumerate(js):
            qi_l.append(qi)
            kj_l.append(j)
            fl_l.append((1 if n == 0 else 0) | (2 if n == len(js) - 1 else 0) | (4 if partial else 0))
    return (jnp.asarray(qi_l, jnp.int32), jnp.asarray(kj_l, jnp.int32), jnp.asarray(fl_l, jnp.int32))


def _dot_nt(a, b):
    return lax.dot_general(a, b, (((1,), (1,)), ((), ())), preferred_element_type=F32)


def _flash_kernel(qi_ref, kj_ref, fl_ref, *refs, aug, groups, tq, tk, fox, window, combine, lam_scale):
    refs = list(refs)
    q_ref, k_ref, v_ref = refs[:3]
    refs = refs[3:]
    mask_ref = None if aug else refs.pop(0)
    ft_ref = refs.pop(0) if fox else None
    if combine == "diff":
        lam_ref, bd_ref, gsub_ref = refs[:3]
        refs = refs[3:]
    o_ref, m_ref, acc_ref, qs_ref = refs

    step = pl.program_id(1)
    fl = fl_ref[step]
    qi = qi_ref[step]
    kj = kj_ref[step]
    pair = lambda ref, p: ref[:, p * PAIR_LANES:(p + 1) * PAIR_LANES]

    @pl.when((fl & 1) != 0)
    def _():
        m_ref[...] = jnp.full(m_ref.shape, NEG, F32)
        acc_ref[...] = jnp.zeros(acc_ref.shape, F32)
        if not aug:
            for p in range(2):
                q = pair(q_ref, p)
                for r in range(groups):
                    qs_ref[p * groups + r] = q * mask_ref[r:r + 1, :]

    def body(apply_mask):
        if apply_mask:
            row = qi * tq + lax.broadcasted_iota(jnp.int32, (tq, tk), 0)
            col = kj * tk + lax.broadcasted_iota(jnp.int32, (tq, tk), 1)
            valid = col <= row
            if window is not None:
                valid = jnp.logical_and(valid, col > row - window)
        if fox:
            ft = ft_ref[...] * LOG2E
        for p in range(2):
            v1 = jnp.concatenate([pair(v_ref, p), jnp.ones((tk, PAIR_LANES), BF16)], axis=1)
            for r in range(groups):
                g = p * groups + r
                if aug:
                    s = _dot_nt(q_ref[g], k_ref[g])
                else:
                    s = _dot_nt(qs_ref[g], pair(k_ref, p))
                if fox:
                    s = s - ft[g:g + 1, :]
                if apply_mask:
                    s = jnp.where(valid, s, NEG)
                m_prev = m_ref[g]
                m_new = jnp.maximum(m_prev, jnp.max(s, axis=-1, keepdims=True))
                alpha = jnp.exp2(m_prev - m_new)
                pexp = jnp.exp2(s - m_new)
                acc_ref[g] = alpha * acc_ref[g] + jnp.dot(pexp.astype(BF16), v1, preferred_element_type=F32)
                m_ref[g] = m_new

    masked = (fl & 4) != 0
    pl.when(masked)(lambda: body(True))
    pl.when(jnp.logical_not(masked))(lambda: body(False))

    @pl.when((fl & 2) != 0)
    def _():
        lane = lax.broadcasted_iota(jnp.int32, (tq, PAIR_LANES), 1)
        for p in range(2):
            outs = []
            for r in range(groups):
                a = acc_ref[p * groups + r]
                outs.append(a[:, :PAIR_LANES] / a[:, PAIR_LANES:])
            if combine == "diff":
                lam = lam_ref[0]
                o = jnp.where(lane < HEAD_DIM, outs[0] - lam * outs[1], outs[2] - lam * outs[3])
                o = _group_rms(o, bd_ref[...], gsub_ref[...]) * lam_scale
            else:
                o = jnp.where(lane < HEAD_DIM, outs[0], outs[1])
            o_ref[:, p * PAIR_LANES:(p + 1) * PAIR_LANES] = o.astype(o_ref.dtype)


def _lane_masks(bounds):
    lane = np.arange(PAIR_LANES)
    return jnp.asarray(np.stack([(lane >= a) & (lane < b) for a, b in bounds]).astype(np.float32), BF16)


PAIR_MASKS = ((0, 64), (64, 128))
DIFF_MASKS = ((0, 32), (32, 64), (64, 96), (96, 128))


def _flash(q_src, k_src, v_src, *, batch, seq, tq, tk, q_col=None, k_col=None, v_col, aug=False,
           fox_t=None, window=None, diff=None, out_dtype=BF16):
    tq, tk = min(tq, seq), min(tk, seq)
    nq, nk = seq // tq, seq // tk
    qi_t, kj_t, fl_t = _flash_steps(seq, tq, tk, window)
    n_steps = int(qi_t.shape[0]) // 1
    groups = 4 if diff is not None else 2
    combine = "diff" if diff is not None else "pair"

    def tok(use_q, col):
        tile, n_t = (tq, nq) if use_q else (tk, nk)
        return pl.BlockSpec((tile, BR_WIDTH),
                            lambda b, s, qi, kj, fl: (b * n_t + (qi if use_q else kj)[s], col))

    in_specs, args = [], []
    if aug:
        in_specs += [pl.BlockSpec((None, N_HEADS, tq, PAIR_LANES), lambda b, s, qi, kj, fl: (b, 0, qi[s], 0)),
                     pl.BlockSpec((None, N_HEADS, tk, PAIR_LANES), lambda b, s, qi, kj, fl: (b, 0, kj[s], 0))]
    else:
        in_specs += [tok(True, q_col), tok(False, k_col)]
    in_specs.append(tok(False, v_col))
    args += [q_src, k_src, v_src]
    if not aug:
        in_specs.append(_const_spec((groups, PAIR_LANES)))
        args.append(_lane_masks(DIFF_MASKS if diff is not None else PAIR_MASKS))
    if fox_t is not None:
        in_specs.append(pl.BlockSpec((None, 8, tk), lambda b, s, qi, kj, fl: (b, 0, kj[s])))
        args.append(fox_t)
    lam_scale = 1.0
    if diff is not None:
        lam, lam_scale, gsub = diff
        in_specs += [pl.BlockSpec(memory_space=pltpu.SMEM), _const_spec((PAIR_LANES, PAIR_LANES)),
                     _const_spec((1, PAIR_LANES))]
        args += [lam.reshape(1).astype(F32), _group_mean_matrix(PAIR_LANES, HEAD_DIM), gsub.reshape(1, PAIR_LANES)]

    kern = functools.partial(_flash_kernel, aug=aug, groups=groups, tq=tq, tk=tk, fox=fox_t is not None,
                             window=window, combine=combine, lam_scale=lam_scale)
    return pl.pallas_call(
        kern,
        grid_spec=pltpu.PrefetchScalarGridSpec(
            num_scalar_prefetch=3,
            grid=(batch, n_steps),
            in_specs=in_specs,
            out_specs=pl.BlockSpec((tq, BR_WIDTH), lambda b, s, qi, kj, fl: (b * nq + qi[s], 0)),
            scratch_shapes=[pltpu.VMEM((2 * groups, tq, 1), F32),
                            pltpu.VMEM((2 * groups, tq, 2 * PAIR_LANES), F32),
                            pltpu.VMEM((2 * groups, tq, PAIR_LANES), BF16)]),
        out_shape=jax.ShapeDtypeStruct((batch * seq, BR_WIDTH), out_dtype),
        compiler_params=_cparams(("parallel", "arbitrary")),
        name="flash_" + ("aug" if aug else combine) + ("_fox" if fox_t is not None else "")
             + ("_win" if window is not None else ""),
    )(qi_t, kj_t, fl_t, *args)


def _select_bias(rel, j, own, topk):
    big = jnp.int32(1 << 20)
    val = jnp.where(j < own, jnp.where(j >= 0, rel, NEG), NEG)
    bias = jnp.where(j == own, 0.0, NEG)
    for _ in range(topk):
        mx = jnp.max(val, axis=-1, keepdims=True)
        idx = jnp.min(jnp.where(val == mx, j, big), axis=-1, keepdims=True)
        pick = j == jnp.where(mx > 0.5 * NEG, idx, big)
        bias = jnp.where(pick, 0.0, bias)
        val = jnp.where(pick, NEG, val)
    return bias


def _blockmean_kernel(x_ref, o_ref):
    j = pl.program_id(1)
    o_ref[pl.ds(j, 1), :] = jnp.mean(x_ref[...], axis=0, keepdims=True)


def _blockmean(kv, batch, seq, blk, nb_pad):
    nb = seq // blk
    out = pl.pallas_call(
        _blockmean_kernel,
        grid=(batch, nb),
        in_specs=[pl.BlockSpec((blk, BR_WIDTH), lambda b, j: (b * nb + j, 0))],
        out_specs=pl.BlockSpec((None, nb, BR_WIDTH), lambda b, j: (b, 0, 0)),
        out_shape=jax.ShapeDtypeStruct((batch, nb, BR_WIDTH), F32),
        compiler_params=_cparams(("parallel", "arbitrary")),
        name="blockmean",
    )(kv)
    return jnp.pad(out, ((0, 0), (0, nb_pad - nb), (0, 0)))


def _augment(x, bias_or_onehot, r):
    lane = lax.broadcasted_iota(jnp.int32, x.shape, 1)
    mine = (lane < HEAD_DIM) if r == 0 else (lane >= HEAD_DIM)
    return jnp.where(mine, x.astype(F32), bias_or_onehot).astype(BF16)


def _payload_block_id(shape, r):
    lane = lax.broadcasted_iota(jnp.int32, shape, 1)
    return (lane - HEAD_DIM) if r == 0 else jnp.where(lane < HEAD_DIM, lane, -1)


def _moba_router_kernel(q_ref, k_ref, km_ref, mask_ref, qa_ref, ka_ref, *, tq, blk, nb_pad):
    qi = pl.program_id(2)
    q = q_ref[...]
    k = k_ref[...]
    row = qi * tq + lax.broadcasted_iota(jnp.int32, (tq, 1), 0)
    own = row // blk
    for r in range(2):
        km = (km_ref[...] * mask_ref[r:r + 1, :].astype(F32)).astype(BF16)
        lo = HEAD_DIM if r == 0 else 0
        pieces = [km, jnp.zeros((PAIR_LANES - nb_pad, PAIR_LANES), BF16)]
        if lo:
            pieces = [jnp.zeros((lo, PAIR_LANES), BF16), km, jnp.zeros((PAIR_LANES - lo - nb_pad, PAIR_LANES), BF16)]
        rel = _dot_nt(q * mask_ref[r:r + 1, :], jnp.concatenate(pieces, axis=0))
        j = _payload_block_id((tq, PAIR_LANES), r)
        bias = _select_bias(rel, j, own, MOBA_TOPK)
        qa_ref[r] = _augment(q, bias, r)
        ka_ref[r] = _augment(k, (j == own).astype(F32), r)


def _moba_router(att, kmean, batch, seq, *, tq):
    tq = min(tq, seq)
    nq = seq // tq
    nb_pad = kmean.shape[1]
    out = jax.ShapeDtypeStruct((batch, N_HEADS, seq, PAIR_LANES), BF16)
    aug_spec = pl.BlockSpec((None, 2, tq, PAIR_LANES), lambda b, p, i: (b, p, i, 0))
    return pl.pallas_call(
        functools.partial(_moba_router_kernel, tq=tq, blk=MOBA_BLOCK, nb_pad=nb_pad),
        grid=(batch, 2, nq),
        in_specs=[pl.BlockSpec((tq, PAIR_LANES), lambda b, p, i: (b * nq + i, G_QC * 2 + p)),
                  pl.BlockSpec((tq, PAIR_LANES), lambda b, p, i: (b * nq + i, G_KC * 2 + p)),
                  pl.BlockSpec((None, nb_pad, PAIR_LANES), lambda b, p, i: (b, 0, p)),
                  _const_spec((2, PAIR_LANES))],
        out_specs=(aug_spec, aug_spec),
        out_shape=(out, out),
        compiler_params=_cparams(("parallel", "parallel", "parallel")),
        name="moba_router",
    )(att, att, kmean, _lane_masks(PAIR_MASKS))


def _compress_kernel(c_ref, w1_ref, b1_ref, w2_ref, b2_ref, gain_ref, o_ref, *, rows):
    kv = pl.program_id(0)
    uv = jnp.dot(c_ref[...].astype(BF16), w1_ref[...], preferred_element_type=F32)
    hid = uv.shape[1] // 2
    pre = uv[:, :hid] + pltpu.roll(uv[:, hid:], rows - 1, 0) + b1_ref[...]
    y = jnp.dot(jax.nn.gelu(pre, approximate=True).astype(BF16), w2_ref[...],
                preferred_element_type=F32) + b2_ref[...]
    ms = jnp.mean(y * y, axis=-1, keepdims=True)
    yn = y * lax.rsqrt(ms + EPS) * gain_ref[...]
    o_ref[...] = jnp.where(kv == 0, yn, y)


def _compress(chunks, pos_emb, w1, b1, w2, b2, k_gain, *, chunks_per_seq, seqs_per_tile):
    _, r, cw = chunks.shape
    hid = w1.shape[-1]
    rows = chunks_per_seq * seqs_per_tile
    assert r % rows == 0
    w1ab, bias1, w2, b2, gain = _compress_weights(pos_emb, w1, b1, w2, b2, k_gain)
    return pl.pallas_call(
        functools.partial(_compress_kernel, rows=rows),
        grid=(2, r // rows),
        in_specs=[pl.BlockSpec((None, rows, cw), lambda kv, i: (kv, i, 0)),
                  pl.BlockSpec((None, cw, 2 * hid), lambda kv, i: (kv, 0, 0)),
                  pl.BlockSpec((None, 1, hid), lambda kv, i: (kv, 0, 0)),
                  pl.BlockSpec((None, hid, HEAD_DIM), lambda kv, i: (kv, 0, 0)),
                  pl.BlockSpec((None, 1, HEAD_DIM), lambda kv, i: (kv, 0, 0)),
                  _const_spec((1, HEAD_DIM))],
        out_specs=pl.BlockSpec((None, rows, HEAD_DIM), lambda kv, i: (kv, i, 0)),
        out_shape=jax.ShapeDtypeStruct((2, r, HEAD_DIM), F32),
        compiler_params=_cparams(("parallel", "parallel")),
        name="nsa_compress",
    )(chunks, w1ab, bias1, w2, b2, gain)


def _compress_weights(pos_emb, w1, b1, w2, b2, k_gain):
    cw = w1.shape[1] // 2
    hid = w1.shape[-1]
    w1ab = jnp.concatenate([w1[:, :cw], w1[:, cw:]], axis=-1).astype(BF16)
    bias1 = (jnp.einsum('kf,kfh->kh', pos_emb.reshape(2, -1).astype(F32), w1.astype(F32),
                        precision=lax.Precision.HIGHEST) + b1.astype(F32)).reshape(2, 1, hid)
    return (w1ab, bias1, w2.astype(BF16), b2.astype(F32).reshape(2, 1, HEAD_DIM),
            k_gain.astype(F32).reshape(1, HEAD_DIM))


def _cmp_to_slc_map(n_cmp_pad, n_cmp):
    start = np.arange(n_cmp_pad)[:, None] * NSA_CMP_STRIDE
    blk = np.arange(HEAD_DIM)[None, :] * NSA_SLC_BLOCK
    ov = np.minimum(start + NSA_CMP_LEN, blk + NSA_SLC_BLOCK) - np.maximum(start, blk)
    m = np.maximum(ov, 0).astype(np.float32) / NSA_CMP_LEN
    m[n_cmp:] = 0.0
    z = np.zeros_like(m)
    return jnp.asarray(np.stack([np.concatenate([z, m], 1), np.concatenate([m, z], 1)]), BF16)


def _nsa_cmp_kernel(qn_ref, q_ref, k_ref, kc_ref, vc_ref, map_ref, mask_ref,
                    oc_ref, qa_ref, ka_ref, *, tq, n_pad):
    qi = pl.program_id(2)
    qn = qn_ref[...]
    q = q_ref[...]
    k = k_ref[...]
    kc = kc_ref[...].astype(BF16)
    vc = vc_ref[...].astype(BF16)
    row = qi * tq + lax.broadcasted_iota(jnp.int32, (tq, 1), 0)
    own = row // NSA_SLC_BLOCK
    cmp_end = lax.broadcasted_iota(jnp.int32, (tq, n_pad), 1) * NSA_CMP_STRIDE + (NSA_CMP_LEN - 1)
    visible = cmp_end <= row
    outs = []
    for r in range(2):
        s = jnp.where(visible, _dot_nt(qn * mask_ref[r:r + 1, :], kc), NEG)
        e = jnp.where(visible, jnp.exp2(s - jnp.max(s, axis=-1, keepdims=True)), 0.0)
        p = e / jnp.maximum(jnp.sum(e, axis=-1, keepdims=True), 1e-30)
        p_hi = p.astype(BF16)
        p_lo = (p - p_hi.astype(F32)).astype(BF16)
        outs.append(jnp.dot(p_hi, vc, preferred_element_type=F32))
        rel = (jnp.dot(p_hi, map_ref[r], preferred_element_type=F32)
               + jnp.dot(p_lo, map_ref[r], preferred_element_type=F32))
        j = _payload_block_id((tq, PAIR_LANES), r)
        bias = _select_bias(rel, j, own, NSA_TOPK)
        qa_ref[r] = _augment(q, bias, r)
        ka_ref[r] = _augment(k, (j == own).astype(F32), r)
    lane = lax.broadcasted_iota(jnp.int32, (tq, PAIR_LANES), 1)
    oc_ref[...] = jnp.where(lane < HEAD_DIM, outs[0], outs[1])


def _nsa_cmp(att, cmp_kv, batch, seq, n_cmp, *, tq):
    tq = min(tq, seq)
    nq = seq // tq
    n_pad = cmp_kv.shape[1]
    aug = jax.ShapeDtypeStruct((batch, N_HEADS, seq, PAIR_LANES), BF16)
    aug_spec = pl.BlockSpec((None, 2, tq, PAIR_LANES), lambda b, p, i: (b, p, i, 0))
    tok = lambda g: pl.BlockSpec((tq, PAIR_LANES), lambda b, p, i: (b * nq + i, g * 2 + p))
    return pl.pallas_call(
        functools.partial(_nsa_cmp_kernel, tq=tq, n_pad=n_pad),
        grid=(batch, 2, nq),
        in_specs=[tok(G_QDN), tok(G_QD), tok(G_KDS),
                  pl.BlockSpec((None, n_pad, PAIR_LANES), lambda b, p, i: (b, 0, p)),
                  pl.BlockSpec((None, n_pad, PAIR_LANES), lambda b, p, i: (b, 0, 2 + p)),
                  _const_spec((2, n_pad, PAIR_LANES)), _const_spec((2, PAIR_LANES))],
        out_specs=(pl.BlockSpec((tq, PAIR_LANES), lambda b, p, i: (b * nq + i, p)), aug_spec, aug_spec),
        out_shape=(jax.ShapeDtypeStruct((batch * seq, BR_WIDTH), F32), aug, aug),
        compiler_params=_cparams(("parallel", "parallel", "parallel")),
        name="nsa_cmp",
    )(att, att, att, cmp_kv, cmp_kv, _cmp_to_slc_map(n_pad, n_cmp), _lane_masks(PAIR_MASKS))


def _resident_spec(shape):
    nd = len(shape)
    return pl.BlockSpec(shape, lambda *_: (0,) * nd, pipeline_mode=pl.Buffered(1))


def _gate_expand_matrix():
    m = np.zeros((3, 128, BR_WIDTH), np.float32)
    for c in range(3):
        for h in range(N_HEADS):
            m[c, N_HEADS + 3 * h + c, h * HEAD_DIM:(h + 1) * HEAD_DIM] = 1.0
    return jnp.asarray(m, BF16)


def _merge_kernel(x_ref, hb_ref, oa_ref, ob_ref, oc_ref, odc_ref, ods_ref, odw_ref, small_ref,
                  g1_ref, sc2_ref, sh2_ref, gn2_ref, ex_ref, wbr_ref, wg_ref, bg_ref, wo_ref,
                  xo_ref, h2_ref):
    d = x_ref.shape[1]
    sm = small_ref[...]
    sm_hi = sm.astype(BF16)
    sm_lo = (sm - sm_hi.astype(F32)).astype(BF16)
    o_d = jnp.zeros(odc_ref.shape, F32)
    for c, ref in enumerate((odc_ref, ods_ref, odw_ref)):
        gate = (jnp.dot(sm_hi, ex_ref[c], preferred_element_type=F32)
                + jnp.dot(sm_lo, ex_ref[c], preferred_element_type=F32))
        o_d = o_d + gate * ref[...]
    hb = hb_ref[...]
    branches = (oa_ref[...], ob_ref[...], oc_ref[...], o_d.astype(BF16))
    merged = jnp.zeros((x_ref.shape[0], d), F32)
    for g, o in enumerate(branches):
        br = jnp.dot(o, wbr_ref[g], preferred_element_type=F32)
        z = jnp.dot(hb, wg_ref[:, g * d:(g + 1) * d], preferred_element_type=F32) + bg_ref[:, g * d:(g + 1) * d]
        merged = merged + jax.nn.sigmoid(z) * br
    y = jnp.dot(merged.astype(BF16), wo_ref[...], preferred_element_type=F32)
    x = x_ref[...] + g1_ref[...] * y
    xo_ref[...] = x
    ms = jnp.mean(x * x, axis=-1, keepdims=True)
    h2 = (x * lax.rsqrt(ms + EPS) * gn2_ref[...]) * (1.0 + sc2_ref[...]) + sh2_ref[...]
    h2_ref[...] = h2.astype(BF16)


def _mod_spec(mod, d, tm, rows_per_mod):
    if rows_per_mod == 1:
        return pl.BlockSpec((tm, d), lambda i: (i, 0)), mod
    tpm = rows_per_mod // tm
    return pl.BlockSpec((None, 1, d), lambda i: (i // tpm, 0, 0)), mod.reshape(-1, 1, d)


def _merge(x2d, hb, o_a, o_b, o_c, o_dc, o_ds, o_dw, small, g1, sc2, sh2, gn2,
           w_branch, w_gate, b_gate, w_out, *, rows_per_mod, tm):
    n, d = x2d.shape
    tm = min(tm, n)
    row = lambda w: pl.BlockSpec((tm, w), lambda i: (i, 0))
    mods = [_mod_spec(m, d, tm, rows_per_mod) for m in (g1, sc2, sh2)]
    return pl.pallas_call(
        _merge_kernel,
        grid=(n // tm,),
        in_specs=[row(d), row(d)] + [row(BR_WIDTH)] * 6 + [row(128)] + [m[0] for m in mods]
                 + [_const_spec((1, d)), _const_spec((3, 128, BR_WIDTH)),
                    _resident_spec(w_branch.shape), _resident_spec(w_gate.shape),
                    _const_spec((1, 4 * d)), _resident_spec(w_out.shape)],
        out_specs=(row(d), row(d)),
        out_shape=(jax.ShapeDtypeStruct((n, d), F32), jax.ShapeDtypeStruct((n, d), BF16)),
        compiler_params=_cparams(("parallel",)),
        name="merge",
    )(x2d, hb, o_a, o_b, o_c, o_dc, o_ds, o_dw, small, *[m[1] for m in mods],
      gn2.reshape(1, d).astype(F32), _gate_expand_matrix(),
      w_branch.astype(BF16), w_gate.astype(BF16), b_gate.reshape(1, 4 * d).astype(F32), w_out.astype(BF16))


def _swiglu_acc(h, wg_ref, wu_ref, wd_ref, chunk):
    ff = wg_ref.shape[-1]
    acc = None
    for c0 in range(0, ff, chunk):
        g = jnp.dot(h, wg_ref[:, c0:c0 + chunk], preferred_element_type=F32)
        u = jnp.dot(h, wu_ref[:, c0:c0 + chunk], preferred_element_type=F32)
        part = jnp.dot((g * jax.nn.sigmoid(g) * u).astype(BF16), wd_ref[c0:c0 + chunk, :],
                       preferred_element_type=F32)
        acc = part if acc is None else acc + part
    return acc


def _ffn_kernel(x_ref, h_ref, g2_ref, wg_ref, wu_ref, wd_ref, o_ref, *, chunk):
    o_ref[...] = x_ref[...] + g2_ref[...] * _swiglu_acc(h_ref[...], wg_ref, wu_ref, wd_ref, chunk)


def _ff_chunk(ff):
    for c in (1408, 1024, 896, 512, 256, 128):
        if ff % c == 0:
            return c
    return ff


def _ffn(x2d, h2, g2, wg, wu, wd, *, rows_per_mod, tm):
    n, d = x2d.shape
    tm = min(tm, n)
    row = lambda w: pl.BlockSpec((tm, w), lambda i: (i, 0))
    g2_spec, g2 = _mod_spec(g2, d, tm, rows_per_mod)
    return pl.pallas_call(
        functools.partial(_ffn_kernel, chunk=_ff_chunk(wg.shape[1])),
        grid=(n // tm,),
        in_specs=[row(d), row(d), g2_spec, _resident_spec(wg.shape), _resident_spec(wu.shape),
                  _resident_spec(wd.shape)],
        out_specs=row(d),
        out_shape=jax.ShapeDtypeStruct((n, d), F32),
        compiler_params=_cparams(("parallel",)),
        name="ffn",
    )(x2d, h2, g2, wg.astype(BF16), wu.astype(BF16), wd.astype(BF16))


def _route_top2(logits, n_exp):
    lane = lax.broadcasted_iota(jnp.int32, logits.shape, 1)
    real = lane < n_exp
    logits = jnp.where(real, logits, NEG)
    ex = jnp.exp(logits - jnp.max(logits, axis=-1, keepdims=True))
    prob = ex / jnp.sum(ex, axis=-1, keepdims=True)
    avail = real
    comb = jnp.zeros(prob.shape, F32)
    for _ in range(2):
        val = jnp.where(avail, prob, -1.0)
        mx = jnp.max(val, axis=-1, keepdims=True)
        idx = jnp.min(jnp.where(jnp.logical_and(avail, val == mx), lane, 1 << 20), axis=-1, keepdims=True)
        pick = lane == idx
        comb = jnp.where(pick, prob, comb)
        avail = jnp.logical_and(avail, jnp.logical_not(pick))
    return comb / jnp.sum(comb, axis=-1, keepdims=True)


def _moe_sparse_kernel(h_ref, wr_ref, br_ref, ltri_ref, wg_ref, wu_ref, wd_ref, o_ref,
                       comb_ref, rank_ref, rank_t_ref, comb_t_ref, *, chunk, n_exp, cap):
    e = pl.program_id(1)
    tm = h_ref.shape[0]
    h = h_ref[...]

    @pl.when(e == 0)
    def _():
        comb = _route_top2(jnp.dot(h, wr_ref[...], preferred_element_type=F32) + br_ref[...], n_exp)
        routed = jnp.where(comb > 0.0, 1.0, 0.0)
        rank = jnp.dot(ltri_ref[...], routed.astype(BF16), preferred_element_type=F32)
        comb_ref[...] = comb
        rank_ref[...] = rank
        blk = min(tm, 128)
        for i in range(tm // blk):
            rows = slice(i * blk, (i + 1) * blk)
            rank_t_ref[:, rows] = jnp.transpose(rank[rows, :])
            comb_t_ref[:, rows] = jnp.transpose(comb[rows, :])
        o_ref[...] = jnp.zeros(o_ref.shape, F32)

    lane = lax.broadcasted_iota(jnp.int32, (tm, 128), 1)
    mine = lane == e
    w_col = jnp.sum(jnp.where(mine, comb_ref[...], 0.0), axis=-1, keepdims=True)
    key_col = jnp.where(w_col > 0.0, jnp.sum(jnp.where(mine, rank_ref[...], 0.0), axis=-1, keepdims=True), -1.0)
    key_row = jnp.where(comb_t_ref[pl.ds(e, 1), :] > 0.0, rank_t_ref[pl.ds(e, 1), :], -1.0)
    count = jnp.sum(jnp.where(w_col > 0.0, 1.0, 0.0))

    for c in range(-(-tm // cap)):
        @pl.when(count > c * cap)
        def _(c=c):
            slot_r = (lax.broadcasted_iota(jnp.int32, (cap, tm), 0) + c * cap).astype(F32)
            gather = jnp.where(key_row == slot_r, 1.0, 0.0).astype(BF16)
            hc = jnp.dot(gather, h, preferred_element_type=F32).astype(BF16)
            y = _swiglu_acc(hc, wg_ref, wu_ref, wd_ref, chunk)
            slot_c = (lax.broadcasted_iota(jnp.int32, (tm, cap), 1) + c * cap).astype(F32)
            scatter = jnp.where(key_col == slot_c, 1.0, 0.0).astype(BF16)
            o_ref[...] += w_col * jnp.dot(scatter, y.astype(BF16), preferred_element_type=F32)


def _moe_sparse(h2, router, router_b, wg, wu, wd, *, tm, cap):
    n, d = h2.shape
    n_exp = wg.shape[0]
    tm = min(tm, n)
    cap = min(cap, tm)
    assert n % tm == 0 and tm % min(tm, 128) == 0
    wr = jnp.pad(router.astype(BF16), ((0, 0), (0, 128 - n_exp)))
    br = jnp.pad(router_b.astype(F32), (0, 128 - n_exp)).reshape(1, 128)
    ltri = jnp.asarray(np.tril(np.ones((tm, tm), np.float32), -1), BF16)
    exp_spec = lambda a: pl.BlockSpec((None,) + a.shape[1:], lambda i, e: (e, 0, 0))
    cst = lambda shape: pl.BlockSpec(shape, lambda i, e: (0,) * len(shape))
    return pl.pallas_call(
        functools.partial(_moe_sparse_kernel, chunk=_ff_chunk(wg.shape[2]), n_exp=n_exp, cap=cap),
        grid=(n // tm, n_exp),
        in_specs=[pl.BlockSpec((tm, d), lambda i, e: (i, 0)), cst((d, 128)), cst((1, 128)), cst((tm, tm)),
                  exp_spec(wg), exp_spec(wu), exp_spec(wd)],
        out_specs=pl.BlockSpec((tm, d), lambda i, e: (i, 0)),
        out_shape=jax.ShapeDtypeStruct((n, d), F32),
        scratch_shapes=[pltpu.VMEM((tm, 128), F32), pltpu.VMEM((tm, 128), F32),
                        pltpu.VMEM((128, tm), F32), pltpu.VMEM((128, tm), F32)],
        compiler_params=_cparams(("parallel", "arbitrary")),
        name="moe_sparse",
    )(h2, wr, br, ltri, wg.astype(BF16), wu.astype(BF16), wd.astype(BF16))


def _residual_kernel(x_ref, y_ref, g_ref, o_ref):
    o_ref[...] = x_ref[...] + g_ref[...] * y_ref[...]


def _residual(x2d, y, g, *, rows_per_mod, tm):
    n, d = x2d.shape
    tm = min(tm, n)
    row = pl.BlockSpec((tm, d), lambda i: (i, 0))
    g_spec, g = _mod_spec(g, d, tm, rows_per_mod)
    return pl.pallas_call(
        _residual_kernel, grid=(n // tm,), in_specs=[row, row, g_spec], out_specs=row,
        out_shape=jax.ShapeDtypeStruct((n, d), F32), compiler_params=_cparams(("parallel",)),
        name="residual",
    )(x2d, y, g)


def _moe_kernel(x_ref, h_ref, g2_ref, wr_ref, br_ref, wg_ref, wu_ref, wd_ref, o_ref,
                comb_ref, acc_ref, *, chunk, n_exp):
    e = pl.program_id(1)
    h = h_ref[...]

    @pl.when(e == 0)
    def _():
        logits = jnp.dot(h, wr_ref[...], preferred_element_type=F32) + br_ref[...]
        lane = lax.broadcasted_iota(jnp.int32, logits.shape, 1)
        real = lane < n_exp
        logits = jnp.where(real, logits, NEG)
        ex = jnp.exp(logits - jnp.max(logits, axis=-1, keepdims=True))
        prob = ex / jnp.sum(ex, axis=-1, keepdims=True)
        avail = real
        comb = jnp.zeros(prob.shape, F32)
        for _ in range(2):
            val = jnp.where(avail, prob, -1.0)
            mx = jnp.max(val, axis=-1, keepdims=True)
            idx = jnp.min(jnp.where(jnp.logical_and(avail, val == mx), lane, 1 << 20), axis=-1, keepdims=True)
            pick = lane == idx
            comb = jnp.where(pick, prob, comb)
            avail = jnp.logical_and(avail, jnp.logical_not(pick))
        comb_ref[...] = comb / jnp.sum(comb, axis=-1, keepdims=True)
        acc_ref[...] = jnp.zeros(acc_ref.shape, F32)

    lane = lax.broadcasted_iota(jnp.int32, comb_ref.shape, 1)
    w_e = jnp.sum(jnp.where(lane == e, comb_ref[...], 0.0), axis=-1, keepdims=True)

    @pl.when(jnp.max(w_e) > 0.0)
    def _():
        acc_ref[...] += w_e * _swiglu_acc(h, wg_ref, wu_ref, wd_ref, chunk)

    @pl.when(e == n_exp - 1)
    def _():
        o_ref[...] = x_ref[...] + g2_ref[...] * acc_ref[...]


def _moe(x2d, h2, g2, router, router_b, wg, wu, wd, *, rows_per_mod, tm):
    n, d = x2d.shape
    n_exp = wg.shape[0]
    tm = min(tm, n)
    row = lambda w: pl.BlockSpec((tm, w), lambda i, e: (i, 0))
    if rows_per_mod == 1:
        g2_spec = row(d)
    else:
        tpm = rows_per_mod // tm
        g2_spec = pl.BlockSpec((None, 1, d), lambda i, e: (i // tpm, 0, 0))
        g2 = g2.reshape(-1, 1, d)
    wr = jnp.pad(router.astype(BF16), ((0, 0), (0, 128 - n_exp)))
    br = jnp.pad(router_b.astype(F32), (0, 128 - n_exp)).reshape(1, 128)
    exp_spec = lambda a: pl.BlockSpec((None,) + a.shape[1:], lambda i, e: (e, 0, 0))
    return pl.pallas_call(
        functools.partial(_moe_kernel, chunk=_ff_chunk(wg.shape[2]), n_exp=n_exp),
        grid=(n // tm, n_exp),
        in_specs=[row(d), row(d), g2_spec,
                  pl.BlockSpec((d, 128), lambda i, e: (0, 0)), pl.BlockSpec((1, 128), lambda i, e: (0, 0)),
                  exp_spec(wg), exp_spec(wu), exp_spec(wd)],
        out_specs=row(d),
        out_shape=jax.ShapeDtypeStruct((n, d), F32),
        scratch_shapes=[pltpu.VMEM((tm, 128), F32), pltpu.VMEM((tm, d), F32)],
        compiler_params=_cparams(("parallel", "arbitrary")),
        name="moe",
    )(x2d, h2, g2, wr, br, wg.astype(BF16), wu.astype(BF16), wd.astype(BF16))


def _decode_row_masks(diff):
    lane = np.arange(BR_WIDTH)
    qm = np.zeros((8, BR_WIDTH), np.float32)
    om = np.zeros((8, BR_WIDTH), np.float32)
    for r in range(8 if diff else N_HEADS):
        h = r // 2 if diff else r
        lo, hi = (h * HEAD_DIM + (r % 2) * DIFF_DIM, h * HEAD_DIM + (r % 2 + 1) * DIFF_DIM) if diff \
            else (h * HEAD_DIM, (h + 1) * HEAD_DIM)
        qm[r] = (lane >= lo) & (lane < hi)
        om[r] = (lane >= h * HEAD_DIM) & (lane < (h + 1) * HEAD_DIM)
    return jnp.asarray(qm, BF16), jnp.asarray(om, F32)


def _segment_matrices(n_keys, blk):
    seg = (np.arange(n_keys)[:, None] // blk == np.arange(128)[None, :]).astype(np.float32)
    return jnp.asarray(seg / blk, F32), jnp.asarray(seg.T, BF16)


def _decode_kernel(pt_ref, *refs, n_pg, spb, dense, mode, fox, diff, lam_scale, skip_first, shift_out, n_alias):
    del pt_ref
    refs = list(refs)
    q_ref, kvn_ref = refs[:2]
    n_in = 1 if dense else spb * n_pg
    pages = refs[2:2 + n_in]
    dense_ref = pages[0] if dense else None
    refs = refs[2 + n_in:]
    if fox:
        lf_pages = refs[:n_in]
        smalln_ref, diag_ref = refs[n_in:n_in + 2]
        refs = refs[n_in + 2:]
    if mode == "sel":
        sel_ref = refs.pop(0)
    qm_ref, om_ref = refs[:2]
    refs = refs[2:]
    if mode == "sel":
        segt_ref = refs.pop(0)
    if diff:
        lam_ref, bd_ref, gsub_ref = refs[:3]
        refs = refs[3:]
    if shift_out:
        newt_ref = refs.pop(0)
        refs = refs[n_alias:]
        o_ref, win_ref = refs
    else:
        (o_ref,) = refs

    for i in range(spb):
        sample_pages = [dense_ref.at[i]] if dense else pages[i * n_pg:(i + 1) * n_pg]
        rows = sample_pages[0].shape[-1]
        if shift_out:
            lane_s = lax.broadcasted_iota(jnp.int32, newt_ref.shape, 1)
            col = jnp.sum(jnp.where(lane_s == pl.program_id(0) * spb + i, newt_ref[...], 0.0),
                          axis=-1, keepdims=True)
            lane = lax.broadcasted_iota(jnp.int32, (BR_WIDTH, rows), 1)
            for kv in range(2):
                win_ref[i, kv] = jnp.where(lane == rows - 1, col[kv * BR_WIDTH:(kv + 1) * BR_WIDTH],
                                           pltpu.roll(sample_pages[0][kv], rows - 1, 1))

        qrows = q_ref[i:i + 1, :] * qm_ref[...]
        kn = kvn_ref[i:i + 1, :BR_WIDTH].astype(BF16).astype(F32)
        vn = kvn_ref[i:i + 1, BR_WIDTH:].astype(BF16).astype(F32)
        s_new = jnp.sum(qrows.astype(F32) * kn, axis=-1, keepdims=True)
        s = jnp.concatenate([jnp.dot(qrows, pg[0].astype(BF16), preferred_element_type=F32)
                             for pg in sample_pages], axis=1)
        n_keys = s.shape[1]
        if fox:
            carry = jnp.sum(diag_ref[...] * smalln_ref[i:i + 1, :], axis=-1, keepdims=True)
            lane_p = lax.broadcasted_iota(jnp.int32, (8, rows), 1)
            parts = []
            for lf_ref in reversed(lf_pages[i * n_pg:(i + 1) * n_pg]):
                lf = jnp.concatenate([lf_ref[...], jnp.zeros((8 - N_HEADS, rows), F32)], axis=0)
                suf = lf
                k = 1
                while k < rows:
                    suf = suf + jnp.where(lane_p < rows - k, pltpu.roll(suf, rows - k, 1), 0.0)
                    k *= 2
                parts.append(suf - lf + carry)
                carry = carry + jnp.sum(lf, axis=-1, keepdims=True)
            s = s + jnp.concatenate(parts[::-1], axis=1) * LOG2E
        keep = None
        if mode == "moba":
            nb = n_keys // MOBA_BLOCK
            lane = lax.broadcasted_iota(jnp.int32, (8, 128), 1)
            rel = jnp.zeros((8, 128), F32)
            for b in range(nb):
                tot = jnp.sum(s[:, b * MOBA_BLOCK:(b + 1) * MOBA_BLOCK], axis=-1, keepdims=True)
                rel = jnp.where(lane == b, tot, rel)
            picked = _select_bias(rel, jnp.where(lane < nb, lane, -1), jnp.full((8, 1), nb, jnp.int32), MOBA_TOPK)
            chosen = jnp.where(jnp.logical_and(picked == 0.0, lane < nb), 1.0, 0.0)
            s = jnp.concatenate(
                [jnp.where(jnp.sum(jnp.where(lane == b, chosen, 0.0), axis=-1, keepdims=True) > 0.5,
                           s[:, b * MOBA_BLOCK:(b + 1) * MOBA_BLOCK], NEG) for b in range(nb)], axis=1)
        elif mode == "sel":
            keep = jnp.dot(sel_ref[i].astype(BF16), segt_ref[...], preferred_element_type=F32) > 0.5
        if skip_first:
            fresh = lax.broadcasted_iota(jnp.int32, s.shape, 1) >= skip_first
            keep = fresh if keep is None else jnp.logical_and(keep, fresh)
        if keep is not None:
            s = jnp.where(keep, s, NEG)
        m = jnp.maximum(jnp.max(s, axis=-1, keepdims=True), s_new)
        e = jnp.exp2(s - m)
        e_new = jnp.exp2(s_new - m)
        denom = jnp.sum(e, axis=-1, keepdims=True) + e_new
        eb = e.astype(BF16)
        o = e_new.astype(BF16).astype(F32) * vn
        for j, pg in enumerate(sample_pages):
            o = o + _dot_nt(eb[:, j * rows:(j + 1) * rows], pg[1].astype(BF16))
        o = o / denom
        if diff:
            row = lax.broadcasted_iota(jnp.int32, (8, 1), 0)
            o = o * jnp.where(row % 2 == 0, 1.0, -lam_ref[0])
        out = jnp.sum(o * om_ref[...], axis=0, keepdims=True)
        if diff:
            out = _group_rms(out, bd_ref[...], gsub_ref[...]) * lam_scale
        o_ref[i:i + 1, :] = out


def _decode_attn(att3, q_col, kvn3, cache_t, li, page_table, *, mode="plain", fox=None, diff=None,
                 sel=None, blk=None, dense=False, skip_first=0, shift_out=None):
    s_n = att3.shape[0]
    spb = DECODE_SAMPLES_PER_STEP if s_n % DECODE_SAMPLES_PER_STEP == 0 else 1
    grp = lambda a: a.reshape((s_n // spb, spb) + a.shape[2:])
    rows = cache_t.shape[-1]
    n_pg = 1 if dense else page_table.shape[1]
    n_keys = n_pg * rows
    pt = page_table.reshape(-1).astype(jnp.int32)
    slots = [(i, j) for i in range(spb) for j in range(n_pg)]
    if dense:
        page_specs = [pl.BlockSpec((None, spb, 2, BR_WIDTH, rows), lambda b, pt: (li, b, 0, 0, 0))]
    else:
        page_specs = [pl.BlockSpec((None, None, 2, BR_WIDTH, rows),
                                   lambda b, pt, i=i, j=j: (li, pt[(b * spb + i) * n_pg + j], 0, 0, 0))
                      for i, j in slots]
    in_specs = [pl.BlockSpec((None, spb, BR_WIDTH), lambda b, pt: (b, 0, q_col)),
                pl.BlockSpec((None, spb, 2 * BR_WIDTH), lambda b, pt: (b, 0, 0))] + page_specs
    args = [grp(att3), grp(kvn3)] + [cache_t] * len(page_specs)
    cst = lambda a: (pl.BlockSpec(a.shape, lambda b, pt: (0,) * a.ndim), a)
    consts = []
    if fox is not None:
        logf_t, small3 = fox
        in_specs += [pl.BlockSpec((None, None, N_HEADS, rows),
                                  lambda b, pt, i=i, j=j: (li, pt[(b * spb + i) * n_pg + j], 0, 0)) for i, j in slots]
        args += [logf_t] * len(slots)
        in_specs.append(pl.BlockSpec((None, spb, 128), lambda b, pt: (b, 0, 0)))
        args.append(grp(small3))
        consts += [cst(jnp.asarray(np.eye(8, 128, dtype=np.float32) * (np.arange(8)[:, None] < N_HEADS)))]
    qm, om = _decode_row_masks(diff is not None)
    consts += [cst(qm), cst(om)]
    if mode == "sel":
        in_specs.append(pl.BlockSpec((spb, 8, 128), lambda b, pt: (b, 0, 0)))
        args.append(sel)
        consts.append(cst(_segment_matrices(n_keys, blk)[1]))
    in_specs += [c[0] for c in consts]
    args += [c[1] for c in consts]
    lam_scale = 1.0
    if diff is not None:
        lam, lam_scale, gsub = diff
        bd = _group_mean_matrix(BR_WIDTH, HEAD_DIM)
        in_specs += [pl.BlockSpec(memory_space=pltpu.SMEM), pl.BlockSpec(bd.shape, lambda b, pt: (0, 0)),
                     pl.BlockSpec((1, BR_WIDTH), lambda b, pt: (0, 0))]
        args += [lam.reshape(1).astype(F32), bd, gsub.reshape(1, BR_WIDTH)]
    out_specs = pl.BlockSpec((None, spb, BR_WIDTH), lambda b, pt: (b, 0, 0))
    out_shape = jax.ShapeDtypeStruct((s_n // spb, spb, BR_WIDTH), F32)
    aliases = {}
    if shift_out is not None:
        new_t, prev, depth = shift_out
        in_specs.append(pl.BlockSpec(new_t.shape, lambda b, pt: (0, 0)))
        args.append(new_t)
        if prev is not None:
            aliases = {len(args) + 1: 1}
            in_specs.append(pl.BlockSpec(memory_space=pl.ANY))
            args.append(prev)
        out_specs = (out_specs, pl.BlockSpec((None, spb, 2, BR_WIDTH, rows), lambda b, pt: (li, b, 0, 0, 0)))
        out_shape = (out_shape, jax.ShapeDtypeStruct((depth, s_n, 2, BR_WIDTH, rows), F32))
    kern = functools.partial(_decode_kernel, n_pg=n_pg, spb=spb, dense=dense, mode=mode, fox=fox is not None,
                             diff=diff is not None, lam_scale=lam_scale, skip_first=skip_first,
                             shift_out=shift_out is not None, n_alias=len(aliases))
    res = pl.pallas_call(
        kern,
        grid_spec=pltpu.PrefetchScalarGridSpec(
            num_scalar_prefetch=1, grid=(s_n // spb,), in_specs=in_specs, out_specs=out_specs),
        out_shape=out_shape,
        input_output_aliases=aliases,
        compiler_params=_cparams(("parallel",)),
        name="decode_" + mode + ("_fox" if fox is not None else "") + ("_diff" if diff is not None else "")
             + ("_dense" if dense else ""),
    )(pt, *args)
    if shift_out is not None:
        return res[0].reshape(s_n, 1, BR_WIDTH), res[1]
    return res.reshape(s_n, 1, BR_WIDTH)


def _decode_cmp_kernel(q_ref, kc_ref, vc_ref, map_ref, qm_ref, om_ref, oc_ref, sel_ref, *, n_cmp, own):
    qrows = q_ref[...] * qm_ref[...]
    s = _dot_nt(qrows, kc_ref[...].astype(BF16))
    visible = lax.broadcasted_iota(jnp.int32, s.shape, 1) < n_cmp
    s = jnp.where(visible, s, NEG)
    e = jnp.where(visible, jnp.exp2(s - jnp.max(s, axis=-1, keepdims=True)), 0.0)
    p = e / jnp.maximum(jnp.sum(e, axis=-1, keepdims=True), 1e-30)
    p_hi = p.astype(BF16)
    p_lo = (p - p_hi.astype(F32)).astype(BF16)
    o = jnp.dot(p_hi, vc_ref[...].astype(BF16), preferred_element_type=F32)
    oc_ref[...] = jnp.sum(o * om_ref[...], axis=0, keepdims=True)
    rel = (jnp.dot(p_hi, map_ref[...], preferred_element_type=F32)
           + jnp.dot(p_lo, map_ref[...], preferred_element_type=F32))
    lane = lax.broadcasted_iota(jnp.int32, rel.shape, 1)
    picked = _select_bias(rel, lane, jnp.full((8, 1), own, jnp.int32), NSA_TOPK)
    sel_ref[...] = jnp.where(jnp.logical_and(picked == 0.0, lane < own), 1.0, 0.0)


def _decode_cmp(att3, cmp_kv, n_cmp, own):
    s_n, n_pad, _ = cmp_kv.shape
    assert own <= 128
    start = np.arange(n_pad)[:, None] * NSA_CMP_STRIDE
    blk = np.arange(128)[None, :] * NSA_SLC_BLOCK
    ov = np.minimum(start + NSA_CMP_LEN, blk + NSA_SLC_BLOCK) - np.maximum(start, blk)
    cmap = np.maximum(ov, 0).astype(np.float32) / NSA_CMP_LEN
    cmap[n_cmp:] = 0.0
    qm, om = _decode_row_masks(False)
    return pl.pallas_call(
        functools.partial(_decode_cmp_kernel, n_cmp=n_cmp, own=own),
        grid=(s_n,),
        in_specs=[pl.BlockSpec((None, 1, BR_WIDTH), lambda b: (b, 0, G_QDN)),
                  pl.BlockSpec((None, n_pad, BR_WIDTH), lambda b: (b, 0, 0)),
                  pl.BlockSpec((None, n_pad, BR_WIDTH), lambda b: (b, 0, 1)),
                  _const_spec((n_pad, 128)), _const_spec((8, BR_WIDTH)), _const_spec((8, BR_WIDTH))],
        out_specs=(pl.BlockSpec((None, 1, BR_WIDTH), lambda b: (b, 0, 0)),
                   pl.BlockSpec((None, 8, 128), lambda b: (b, 0, 0))),
        out_shape=(jax.ShapeDtypeStruct((s_n, 1, BR_WIDTH), F32), jax.ShapeDtypeStruct((s_n, 8, 128), F32)),
        compiler_params=_cparams(("parallel",)),
        name="decode_cmp",
    )(att3, cmp_kv, cmp_kv, jnp.asarray(cmap, BF16), qm, om)


def _decode_nsa_cmp_kernel(pt_ref, *refs, n_pg, n_cmp, own):
    del pt_ref
    q_ref = refs[0]
    pages = refs[1:1 + n_pg]
    (w1_ref, b1_ref, w2_ref, b2_ref, gain_ref, map_ref, oc_ref, sel_ref, xs_ref, cst_ref) = refs[1 + n_pg:]
    rows = pages[0].shape[-1]
    n_ch = n_pg * rows // NSA_CMP_STRIDE
    for j, pg in enumerate(pages):
        for kv in range(2):
            for p in range(2):
                xs_ref[kv, p, j * rows:(j + 1) * rows, :] = jnp.transpose(pg[kv, p * PAIR_LANES:(p + 1) * PAIR_LANES, :])
    lane = lax.broadcasted_iota(jnp.int32, (n_ch, PAIR_LANES), 1)
    for kv in range(2):
        for p in range(2):
            for i in range(NSA_CMP_STRIDE // 2):
                x0 = xs_ref[kv, p, pl.ds(2 * i, n_ch, stride=NSA_CMP_STRIDE), :]
                x1 = xs_ref[kv, p, pl.ds(2 * i + 1, n_ch, stride=NSA_CMP_STRIDE), :]
                even = jnp.where(lane < HEAD_DIM, x0, pltpu.roll(x1, HEAD_DIM, 1))
                odd = jnp.where(lane < HEAD_DIM, pltpu.roll(x0, HEAD_DIM, 1), x1)
                cols = slice(i * PAIR_LANES, (i + 1) * PAIR_LANES)
                cst_ref[kv, (2 * p) * n_ch:(2 * p + 1) * n_ch, cols] = even.astype(BF16)
                cst_ref[kv, (2 * p + 1) * n_ch:(2 * p + 2) * n_ch, cols] = odd.astype(BF16)
    toks = []
    for kv in range(2):
        uv = jnp.dot(cst_ref[kv], w1_ref[kv], preferred_element_type=F32)
        hid = uv.shape[1] // 2
        pre = uv[:, :hid] + pltpu.roll(uv[:, hid:], N_HEADS * n_ch - 1, 0) + b1_ref[kv]
        y = jnp.dot(jax.nn.gelu(pre, approximate=True).astype(BF16), w2_ref[kv],
                    preferred_element_type=F32) + b2_ref[kv]
        if kv == 0:
            y = y * lax.rsqrt(jnp.mean(y * y, axis=-1, keepdims=True) + EPS) * gain_ref[...]
        toks.append(y.astype(BF16))
    yk, yv = toks
    q = q_ref[...].astype(F32)
    q8 = jnp.concatenate([q[:, h * HEAD_DIM:(h + 1) * HEAD_DIM] for h in range(N_HEADS)]
                         + [jnp.zeros((8 - N_HEADS, HEAD_DIM), F32)], axis=0).astype(BF16)
    s_all = _dot_nt(q8, yk)
    row = lax.broadcasted_iota(jnp.int32, (8, n_ch), 0)
    s = jnp.zeros((8, n_ch), F32)
    for h in range(N_HEADS):
        s = s + jnp.where(row == h, s_all[:, h * n_ch:(h + 1) * n_ch], 0.0)
    visible = lax.broadcasted_iota(jnp.int32, (8, n_ch), 1) < n_cmp
    s = jnp.where(visible, s, NEG)
    e = jnp.where(visible, jnp.exp2(s - jnp.max(s, axis=-1, keepdims=True)), 0.0)
    p = e / jnp.maximum(jnp.sum(e, axis=-1, keepdims=True), 1e-30)
    p_hi = p.astype(BF16)
    p_lo = (p - p_hi.astype(F32)).astype(BF16)
    p_all = jnp.concatenate([jnp.where(row == h, p_hi, jnp.zeros_like(p_hi)) for h in range(N_HEADS)], axis=1)
    o4 = jnp.dot(p_all, yv, preferred_element_type=F32)
    oc_ref[...] = jnp.concatenate([o4[h:h + 1, :] for h in range(N_HEADS)], axis=1)
    rel = (jnp.dot(p_hi, map_ref[...], preferred_element_type=F32)
           + jnp.dot(p_lo, map_ref[...], preferred_element_type=F32))
    lane_b = lax.broadcasted_iota(jnp.int32, rel.shape, 1)
    picked = _select_bias(rel, lane_b, jnp.full((8, 1), own, jnp.int32), NSA_TOPK)
    sel_ref[...] = jnp.where(jnp.logical_and(picked == 0.0, lane_b < own), 1.0, 0.0)


def _decode_nsa_cmp(att3, cache_t, li, page_table, cmpw, n_cmp, own):
    s_n = att3.shape[0]
    rows = cache_t.shape[-1]
    n_pg = page_table.shape[1]
    n_ch = n_pg * rows // NSA_CMP_STRIDE
    assert own <= 128
    w1ab, bias1, w2, b2, gain = _compress_weights(*cmpw)
    start = np.arange(n_ch)[:, None] * NSA_CMP_STRIDE
    blk = np.arange(128)[None, :] * NSA_SLC_BLOCK
    ov = np.minimum(start + NSA_CMP_LEN, blk + NSA_SLC_BLOCK) - np.maximum(start, blk)
    cmap = np.maximum(ov, 0).astype(np.float32) / NSA_CMP_LEN
    cmap[n_cmp:] = 0.0
    pt = page_table.reshape(-1).astype(jnp.int32)
    cst = lambda a: pl.BlockSpec(a.shape, lambda b, pt: (0,) * a.ndim)
    consts = [w1ab, bias1, w2, b2, gain, jnp.asarray(cmap, BF16)]
    return pl.pallas_call(
        functools.partial(_decode_nsa_cmp_kernel, n_pg=n_pg, n_cmp=n_cmp, own=own),
        grid_spec=pltpu.PrefetchScalarGridSpec(
            num_scalar_prefetch=1, grid=(s_n,),
            in_specs=[pl.BlockSpec((None, 1, BR_WIDTH), lambda b, pt: (b, 0, G_QDN))]
                     + [pl.BlockSpec((None, None, 2, BR_WIDTH, rows),
                                     lambda b, pt, j=j: (li, pt[b * n_pg + j], 0, 0, 0)) for j in range(n_pg)]
                     + [cst(a) for a in consts],
            out_specs=(pl.BlockSpec((None, 1, BR_WIDTH), lambda b, pt: (b, 0, 0)),
                       pl.BlockSpec((None, 8, 128), lambda b, pt: (b, 0, 0))),
            scratch_shapes=[pltpu.VMEM((2, 2, n_pg * rows, PAIR_LANES), F32),
                            pltpu.VMEM((2, N_HEADS * n_ch, NSA_CMP_STRIDE * HEAD_DIM), BF16)]),
        out_shape=(jax.ShapeDtypeStruct((s_n, 1, BR_WIDTH), F32), jax.ShapeDtypeStruct((s_n, 8, 128), F32)),
        compiler_params=_cparams(("parallel",)),
        name="decode_nsa_cmp",
    )(pt, att3, *([cache_t] * n_pg), *consts)


def _sample_attention(att, kvs, small, caches_t, logf_t, win_t, cmp_t, page_table, li, diff, cmpw, past_len,
                      win_prev):
    s_n = att.shape[0]
    kva, kvb, kvc, kvds, kvdw = (a.reshape(s_n, 1, 2 * BR_WIDTH) for a in kvs)
    att3 = att.reshape(s_n, 1, -1)
    ca, cb, cc, cds = caches_t
    dec = functools.partial(_decode_attn, att3, li=li, page_table=page_table)
    o_a = dec(G_QA, kva, ca, fox=(logf_t, small.reshape(s_n, 1, 128)))
    o_b = dec(G_QB, kvb, cb, diff=diff)
    o_c = dec(G_QC, kvc, cc, mode="moba")
    n_cmp = (past_len + 1 - NSA_CMP_LEN) // NSA_CMP_STRIDE + 1
    o_dc, sel = _decode_nsa_cmp(att3, cmp_t, li, page_table, cmpw, n_cmp, past_len // NSA_SLC_BLOCK)
    o_ds = dec(G_QD, kvds, cds, mode="sel", sel=sel, blk=NSA_SLC_BLOCK)
    buf_len = win_t.shape[-1]
    o_dw, win_next = dec(G_QD, kvdw, win_t, dense=True, skip_first=max(0, buf_len - NSA_WINDOW + 1),
                         shift_out=(jnp.transpose(kvs[4]), win_prev, win_t.shape[0]))
    flat = lambda a, dt: a.reshape(s_n, BR_WIDTH).astype(dt)
    return (flat(o_a, BF16), flat(o_b, BF16), flat(o_c, BF16), flat(o_dc, F32), flat(o_ds, F32),
            flat(o_dw, F32)), win_next


def _layer_gains(fox_qnorm, fox_knorm, diff_qnorm, diff_knorm, moba_qnorm, moba_knorm, nsa_qnorm, nsa_knorm):
    return jnp.stack([_tile_gain(fox_qnorm, HEAD_DIM), _tile_gain(fox_knorm, HEAD_DIM),
                      _tile_gain(diff_qnorm, DIFF_DIM), _tile_gain(diff_knorm, DIFF_DIM),
                      _tile_gain(moba_qnorm, HEAD_DIM), _tile_gain(moba_knorm, HEAD_DIM),
                      _tile_gain(nsa_qnorm, HEAD_DIM), _tile_gain(nsa_knorm[1], HEAD_DIM),
                      _tile_gain(nsa_knorm[2], HEAD_DIM)])


def _chunk_rows(kv5):
    b, l = kv5.shape[:2]
    m = l // NSA_CMP_STRIDE
    c = kv5.reshape(b, m, NSA_CMP_STRIDE, 2, N_HEADS, HEAD_DIM)
    return jnp.transpose(c, (3, 0, 4, 1, 2, 5)).reshape(2, b * N_HEADS * m, NSA_CMP_STRIDE * HEAD_DIM)


def _unchunk_tokens(tok, b, m):
    return jnp.transpose(tok.reshape(2, b, N_HEADS, m, HEAD_DIM), (1, 3, 0, 2, 4)).reshape(b, m, 2 * BR_WIDTH)


def _chunk_rows_t(kvt):
    b, _, _, l = kvt.shape
    m = l // NSA_CMP_STRIDE
    c = kvt.reshape(b, 2, N_HEADS, HEAD_DIM, m, NSA_CMP_STRIDE)
    return jnp.transpose(c, (1, 0, 2, 4, 5, 3)).reshape(2, b * N_HEADS * m, NSA_CMP_STRIDE * HEAD_DIM)


def _prompt_attention(att, kmean, kvdc_t, small, diff, cmpw, batch, seq, *, tq=512, tk=512):
    fl = functools.partial(_flash, batch=batch, seq=seq, tq=tq, tk=tk)
    ft8 = _cumsum_t(small, batch, seq)
    o_a = fl(att, att, att, q_col=G_QA, k_col=G_KA, v_col=G_VA, fox_t=ft8)
    o_b = fl(att, att, att, q_col=G_QB, k_col=G_KB, v_col=G_VB, diff=diff)
    qa_c, ka_c = _moba_router(att, kmean, batch, seq, tq=tq)
    o_c = fl(qa_c, ka_c, att, v_col=G_VC, aug=True)
    m = seq // NSA_CMP_STRIDE
    pos_emb, w1, b1, w2, b2, k_gain = cmpw
    tok = _compress(_chunk_rows_t(kvdc_t), pos_emb, w1, b1, w2, b2, k_gain,
                    chunks_per_seq=m, seqs_per_tile=max(1, 512 // m))
    cmp_kv = _unchunk_tokens(tok, batch, m)
    o_dc, qa_d, ka_d = _nsa_cmp(att, cmp_kv, batch, seq, m - 1, tq=tq)
    o_ds = fl(qa_d, ka_d, att, v_col=G_VDS, aug=True, out_dtype=F32)
    o_dw = fl(att, att, att, q_col=G_QD, k_col=G_KDW, v_col=G_VDW, window=NSA_WINDOW, out_dtype=F32)
    return o_a, o_b, o_c, o_dc, o_ds, o_dw


def _trunk_tail(x2d, hb, branch, small, g1, sc2, sh2, g2, gn2, w_branch, w_gate, b_gate, w_out, ffn,
                *, rows_per_mod, tm=512):
    x2d, h2 = _merge(x2d, hb, *branch, small, g1, sc2, sh2, gn2, w_branch, w_gate, b_gate, w_out,
                     rows_per_mod=rows_per_mod, tm=tm)
    if ffn[0] == "dense":
        return _ffn(x2d, h2, g2, *ffn[1:], rows_per_mod=rows_per_mod, tm=tm)
    y = _moe_sparse(h2, *ffn[1:], tm=2 * tm, cap=320)
    return _residual(x2d, y, g2, rows_per_mod=rows_per_mod, tm=tm)


def kernel(x_prompt, x_sample, c_prompt, c_sample, cache_a_kv, cache_a_logf, cache_b_kv, cache_c_kv, cache_d_cmp_kv, cache_d_slc_kv, state_d_win_kv, page_table, ada_w, ada_b, norm_attn, norm_ffn, w_in, fox_fbias, fox_qnorm, fox_knorm, diff_qnorm, diff_knorm, diff_lambda, diff_subnorm, moba_qnorm, moba_knorm, nsa_qnorm, nsa_knorm, nsa_cmp_pos, nsa_cmp_w1, nsa_cmp_b1, nsa_cmp_w2, nsa_cmp_b2, w_branch, w_gate, b_gate, w_out, ffn_w_gate, ffn_w_up, ffn_w_down, moe_router, moe_router_b, moe_w_gate, moe_w_up, moe_w_down):
    bp, t, d = x_prompt.shape
    bs = x_sample.shape[0]
    depth = w_in.shape[0]
    n_pages = page_table.shape[1]
    past_len = n_pages * cache_a_kv.shape[2]
    pos_p = jnp.arange(t, dtype=jnp.int32)
    pos_s = jnp.full((1,), past_len, jnp.int32)
    xp = x_prompt.reshape(bp * t, d)
    xs = x_sample.reshape(bs, d)
    c_all = jnp.concatenate([c_prompt, c_sample, jnp.zeros((-(bp + bs) % 8, d), F32)], axis=0)
    page_t = lambda c: jnp.transpose(c, (0, 1, 3, 4, 5, 2)).reshape(c.shape[:2] + (2, BR_WIDTH, c.shape[2]))
    caches_t = tuple(page_t(c) for c in (cache_a_kv, cache_b_kv, cache_c_kv, cache_d_slc_kv))
    logf_t = jnp.swapaxes(cache_a_logf, 2, 3)
    win_t = page_t(state_d_win_kv)
    cmp_t = page_t(cache_d_cmp_kv)
    kv_p = None
    logf_p = []
    st_s = [[] for _ in range(6)]
    win_s = None
    for li in range(depth):
        lam_init = 0.8 - 0.6 * math.exp(-0.3 * li)
        lq = diff_lambda[li].astype(F32)
        lam = jnp.exp(jnp.sum(lq[0] * lq[1])) - jnp.exp(jnp.sum(lq[2] * lq[3])) + lam_init
        diff = (lam, 1.0 - lam_init, jnp.tile(diff_subnorm[li].astype(F32), 2))
        cmpw = (nsa_cmp_pos[li], nsa_cmp_w1[li], nsa_cmp_b1[li], nsa_cmp_w2[li], nsa_cmp_b2[li], nsa_knorm[li][0])
        mod = _rowmm(c_all, ada_w[li], ada_b[li], silu_in=True)
        modp = mod[:bp].reshape(bp, 6, d)
        wm, ws, bsm = _split_w_in(w_in[li], fox_fbias[li])
        gains = _layer_gains(fox_qnorm[li], fox_knorm[li], diff_qnorm[li], diff_knorm[li], moba_qnorm[li],
                             moba_knorm[li], nsa_qnorm[li], nsa_knorm[li])
        tm_p = 512
        hb, att, *kv_p, small, km = _inproj(
            xp, modp[:, 1], modp[:, 0], norm_attn[li], wm, ws, bsm, gains, pos_p, rows_per_mod=t, tm=tm_p,
            stacked=(li, depth, kv_p))
        nb = t // MOBA_BLOCK
        kmean = km[:, :tm_p // MOBA_BLOCK].reshape(bp, nb, BR_WIDTH)
        kmean = jnp.pad(kmean, ((0, 0), (0, -nb % 8), (0, 0)))
        branch = _prompt_attention(att, kmean, kv_p[3][li], small, diff, cmpw, bp, t)
        j = li // 2
        if li % 2 == 0:
            ffn = ("dense", ffn_w_gate[j], ffn_w_up[j], ffn_w_down[j])
        else:
            ffn = ("moe", moe_router[j], moe_router_b[j], moe_w_gate[j], moe_w_up[j], moe_w_down[j])
        xp = _trunk_tail(xp, hb, branch, small, modp[:, 2], modp[:, 4], modp[:, 3], modp[:, 5], norm_ffn[li],
                         w_branch[li], w_gate[li], b_gate[li], w_out[li], ffn, rows_per_mod=t)
        logf_p.append(small[:, :N_HEADS].reshape(bp, t, N_HEADS))
        mods = mod[bp:bp + bs].reshape(bs, 6, d)
        hb, att, kva, kvb, kvc, kvdc, kvds, kvdw, small = _inproj(
            xs, mods[:, 1], mods[:, 0], norm_attn[li], wm, ws, bsm, gains, pos_s, rows_per_mod=1, tm=bs)
        diff_s = (lam, 1.0 - lam_init, jnp.tile(diff_subnorm[li].astype(F32), N_HEADS))
        branch, win_s = _sample_attention(att, (kva, kvb, kvc, kvds, kvdw), small, caches_t, logf_t, win_t, cmp_t,
                                          page_table, li, diff_s, cmpw, past_len, win_s)
        xs = _trunk_tail(xs, hb, branch, small, mods[:, 2], mods[:, 4], mods[:, 3], mods[:, 5], norm_ffn[li],
                         w_branch[li], w_gate[li], b_gate[li], w_out[li], ffn, rows_per_mod=1)
        kv6 = lambda a: a.reshape(bs, 1, 2, N_HEADS, HEAD_DIM)
        for lst, val in zip(st_s, (kv6(kva), small[:, :N_HEADS].reshape(bs, 1, N_HEADS), kv6(kvb), kv6(kvc),
                                   kv6(kvdc), kv6(kvds))):
            lst.append(val)
    untr = lambda a: jnp.transpose(a.reshape(a.shape[:3] + (N_HEADS, HEAD_DIM, a.shape[-1])), (0, 1, 5, 2, 3, 4))
    win = min(NSA_WINDOW, t)
    kva_p, kvb_p, kvc_p, kvdc_p, kvds_p, kvdw_p = kv_p
    out_p = (untr(kva_p), jnp.stack(logf_p, axis=0), untr(kvb_p), untr(kvc_p), untr(kvdc_p), untr(kvds_p),
             untr(kvdw_p[..., t - win:]))
    out_s = tuple(jnp.stack(lst, axis=0) for lst in st_s) + (untr(win_s),)
    return (xp.reshape(bp, t, d), xs.reshape(bs, 1, d)) + out_p + out_s
```

```python
import functools
import math

import jax
import jax.numpy as jnp
import numpy as np
from jax import lax
from jax.experimental import pallas as pl
from jax.experimental.pallas import tpu as pltpu

F32 = jnp.float32
BF16 = jnp.bfloat16

N_HEADS = 4
HEAD_DIM = 64
BR_WIDTH = N_HEADS * HEAD_DIM
PAIR_LANES = 2 * HEAD_DIM
DIFF_DIM = HEAD_DIM // 2
ROPE_THETA = 500000.0
MOBA_BLOCK = 256
MOBA_TOPK = 3
NSA_CMP_LEN = 32
NSA_CMP_STRIDE = 16
NSA_SLC_BLOCK = 64
NSA_TOPK = 4
NSA_WINDOW = 512
N_EXPERTS = 8
EPS = 1e-6
NEG = -1e30
LOG2E = math.log2(math.e)
DECODE_SAMPLES_PER_STEP = 2
CHUNK_PITCH = 24
V7X_VMEM_BYTES = 64 * 1024 * 1024
VMEM_LIMIT = V7X_VMEM_BYTES - 16 * 1024 * 1024

(G_QA, G_KA, G_VA, G_QB, G_KB, G_VB, G_QC, G_KC, G_VC,
 G_QDN, G_QD, G_KDS, G_VDS, G_KDW, G_VDW) = range(15)
N_ATT_GROUPS = 15


def _cparams(sem):
    return pltpu.CompilerParams(dimension_semantics=sem, vmem_limit_bytes=VMEM_LIMIT)


def _const_spec(shape):
    nd = len(shape)
    return pl.BlockSpec(shape, lambda *_: (0,) * nd)


def _rowmm_kernel(x_ref, w_ref, b_ref, o_ref, *, silu_in):
    x = x_ref[...]
    if silu_in:
        x = x * jax.nn.sigmoid(x)
    o_ref[...] = jnp.dot(x.astype(BF16), w_ref[...].astype(BF16),
                         preferred_element_type=F32) + b_ref[...]


def _rowmm(x, w, b, *, silu_in=False, tn=1024):
    m, k = x.shape
    n = w.shape[1]
    tn = math.gcd(tn, n)
    return pl.pallas_call(
        functools.partial(_rowmm_kernel, silu_in=silu_in),
        grid=(n // tn,),
        in_specs=[pl.BlockSpec((m, k), lambda j: (0, 0)),
                  pl.BlockSpec((k, tn), lambda j: (0, j)),
                  pl.BlockSpec((1, tn), lambda j: (0, j))],
        out_specs=pl.BlockSpec((m, tn), lambda j: (0, j)),
        out_shape=jax.ShapeDtypeStruct((m, n), F32),
        compiler_params=_cparams(("parallel",)),
        name="rowmm",
    )(x, w, b.reshape(1, n))


def _rope_tables(pos, group):
    r = group // 4
    half = r // 2
    inv = ROPE_THETA ** (-(np.arange(half, dtype=np.float32) / half))
    lane = np.arange(BR_WIDTH)
    j = lane % group
    ang = pos.astype(F32)[:, None] * jnp.asarray(inv[j % half], F32)[None, :]
    cos = jnp.where(j[None, :] < r, jnp.cos(ang), 1.0)
    sin = jnp.sin(ang)
    sin_a = jnp.where(j[None, :] < half, -sin, 0.0)
    sin_b = jnp.where((j[None, :] >= half) & (j[None, :] < r), sin, 0.0)
    return cos.astype(F32), sin_a.astype(F32), sin_b.astype(F32)


def _group_mean_matrix(width, group):
    i = np.arange(width)
    return jnp.asarray((i[:, None] // group == i[None, :] // group).astype(np.float32) / group, BF16)


def _group_rms(a, bd, gain):
    ms = jnp.dot((a * a).astype(BF16), bd, preferred_element_type=F32)
    return a * lax.rsqrt(ms + EPS) * gain


def _rope(a, cos, sin_a, sin_b, half):
    w = a.shape[-1]
    return a * cos + pltpu.roll(a, w - half, 1) * sin_a + pltpu.roll(a, half, 1) * sin_b


def _log_sigmoid(x):
    return jnp.minimum(x, 0.0) - jnp.log1p(jnp.exp(-jnp.abs(x)))


def _inproj_kernel(x_ref, sc_ref, sh_ref, gn_ref, w_ref, ws_ref, bs_ref, gains_ref,
                   bd64_ref, bd32_ref, c64_ref, sa64_ref, sb64_ref, c32_ref, sa32_ref, sb32_ref,
                   *rest, transposed, n_alias):
    rest = rest[n_alias:]
    h_ref, att_ref, kva_ref, kvb_ref, kvc_ref, kvdc_ref, kvds_ref, kvdw_ref, small_ref = rest[:9]

    def put_kv(ref, k, v):
        if transposed:
            stage_ref = rest[-1]
            stage_ref[0] = k
            stage_ref[1] = v
            ref[0] = jnp.transpose(stage_ref[0])
            ref[1] = jnp.transpose(stage_ref[1])
        else:
            ref[:, :BR_WIDTH] = k
            ref[:, BR_WIDTH:] = v

    x = x_ref[...]
    ms = jnp.mean(x * x, axis=-1, keepdims=True)
    h = (x * lax.rsqrt(ms + EPS) * gn_ref[...]) * (1.0 + sc_ref[...]) + sh_ref[...]
    hb = h.astype(BF16)
    h_ref[...] = hb
    bd64 = bd64_ref[...]
    bd32 = bd32_ref[...]

    def proj(g):
        return jnp.dot(hb, w_ref[:, g * BR_WIDTH:(g + 1) * BR_WIDTH], preferred_element_type=F32)

    def gain(i):
        return gains_ref[i:i + 1, :]

    def rope64(a):
        return _rope(a, c64_ref[...], sa64_ref[...], sb64_ref[...], 8)

    def rope32(a):
        return _rope(a, c32_ref[...], sa32_ref[...], sb32_ref[...], 4)

    def att(g, a):
        att_ref[:, g * BR_WIDTH:(g + 1) * BR_WIDTH] = a.astype(BF16)

    sm_scale = HEAD_DIM ** -0.5 * LOG2E
    att(G_QA, _group_rms(proj(0), bd64, gain(0)) * sm_scale)
    k = _group_rms(proj(1), bd64, gain(1))
    v = proj(2)
    put_kv(kva_ref, k, v)
    att(G_KA, k)
    att(G_VA, v)
    att(G_QB, rope32(_group_rms(proj(3), bd32, gain(2))) * (DIFF_DIM ** -0.5 * LOG2E))
    k = rope32(_group_rms(proj(4), bd32, gain(3)))
    v = proj(5)
    put_kv(kvb_ref, k, v)
    att(G_KB, k)
    att(G_VB, v)
    att(G_QC, rope64(_group_rms(proj(6), bd64, gain(4))) * sm_scale)
    k = rope64(_group_rms(proj(7), bd64, gain(5)))
    v = proj(8)
    put_kv(kvc_ref, k, v)
    if transposed:
        km_ref = rest[9]
        nb = k.shape[0] // MOBA_BLOCK
        km_ref[...] = jnp.concatenate(
            [jnp.mean(k[b * MOBA_BLOCK:(b + 1) * MOBA_BLOCK], axis=0, keepdims=True) for b in range(nb)]
            + [jnp.zeros((8 - nb, BR_WIDTH), F32)], axis=0)
    att(G_KC, k)
    att(G_VC, v)
    qn = _group_rms(proj(9), bd64, gain(6))
    att(G_QDN, qn * sm_scale)
    att(G_QD, rope64(qn) * sm_scale)
    put_kv(kvdc_ref, proj(10), proj(11))
    if transposed:
        chunk_ref, pair_ref, stage_ref = rest[10], rest[-2], rest[-1]
        n_ch = x.shape[0] // NSA_CMP_STRIDE
        lane_c = lax.broadcasted_iota(jnp.int32, (n_ch, PAIR_LANES), 1)
        for kv in range(2):
            for p in range(2):
                pair_ref[kv, p] = stage_ref[kv, :, p * PAIR_LANES:(p + 1) * PAIR_LANES]
                for i in range(NSA_CMP_STRIDE // 2):
                    x0 = pair_ref[kv, p, pl.ds(2 * i, n_ch, stride=NSA_CMP_STRIDE), :]
                    x1 = pair_ref[kv, p, pl.ds(2 * i + 1, n_ch, stride=NSA_CMP_STRIDE), :]
                    even = jnp.where(lane_c < HEAD_DIM, x0, pltpu.roll(x1, HEAD_DIM, 1))
                    odd = jnp.where(lane_c < HEAD_DIM, pltpu.roll(x0, HEAD_DIM, 1), x1)
                    cols = slice(i * PAIR_LANES, (i + 1) * PAIR_LANES)
                    chunk_ref[kv, 2 * p, :, cols] = even.astype(BF16)
                    chunk_ref[kv, 2 * p + 1, :, cols] = odd.astype(BF16)
    k = rope64(_group_rms(proj(12), bd64, gain(7)))
    v = proj(13)
    put_kv(kvds_ref, k, v)
    att(G_KDS, k)
    att(G_VDS, v)
    k = rope64(_group_rms(proj(14), bd64, gain(8)))
    v = proj(15)
    put_kv(kvdw_ref, k, v)
    att(G_KDW, k)
    att(G_VDW, v)
    z = jnp.dot(hb, ws_ref[...], preferred_element_type=F32) + bs_ref[...]
    lane = lax.broadcasted_iota(jnp.int32, z.shape, 1)
    small_ref[...] = jnp.where(lane < N_HEADS, _log_sigmoid(z), jax.nn.sigmoid(z))


def _split_w_in(w_in, fox_fbias):
    d = w_in.shape[0]
    c0 = 3 * BR_WIDTH
    main = jnp.concatenate([w_in[:, :c0], w_in[:, c0 + N_HEADS:c0 + N_HEADS + 13 * BR_WIDTH]], axis=1)
    small = jnp.concatenate([w_in[:, c0:c0 + N_HEADS], w_in[:, c0 + N_HEADS + 13 * BR_WIDTH:],
                             jnp.zeros((d, 128 - 4 * N_HEADS), w_in.dtype)], axis=1)
    bias = jnp.concatenate([fox_fbias.astype(F32), jnp.zeros((128 - N_HEADS,), F32)]).reshape(1, 128)
    return main.astype(BF16), small.astype(BF16), bias


def _inproj(x2d, sc, sh, gn, w_main, w_small, b_small, gains, pos, *, rows_per_mod, tm, stacked=None):
    n, d = x2d.shape
    tm = min(tm, n)
    assert n % tm == 0
    per_tok = rows_per_mod == 1
    if not per_tok:
        assert rows_per_mod % tm == 0
    tabs = _rope_tables(pos, HEAD_DIM) + _rope_tables(pos, DIFF_DIM)
    single_pos = pos.shape[0] == 1
    tiles_per_seq = 1 if single_pos else pos.shape[0] // tm

    if per_tok:
        mod_spec = pl.BlockSpec((tm, d), lambda i: (i, 0))
    else:
        tpm = rows_per_mod // tm
        mod_spec = pl.BlockSpec((None, 1, d), lambda i: (i // tpm, 0, 0))
        sc, sh = sc.reshape(-1, 1, d), sh.reshape(-1, 1, d)
    if single_pos:
        tab_spec = pl.BlockSpec((1, BR_WIDTH), lambda i: (0, 0))
    else:
        tab_spec = pl.BlockSpec((tm, BR_WIDTH), lambda i: (i % tiles_per_seq, 0))

    row = lambda w: pl.BlockSpec((tm, w), lambda i: (i, 0))
    in_specs = [row(d), mod_spec, mod_spec, _const_spec((1, d)),
                _const_spec(w_main.shape), _const_spec(w_small.shape), _const_spec((1, 128)),
                _const_spec(gains.shape),
                _const_spec((BR_WIDTH, BR_WIDTH)), _const_spec((BR_WIDTH, BR_WIDTH))] + [tab_spec] * 6
    args = [x2d, sc, sh, gn.reshape(1, d), w_main, w_small, b_small, gains,
            _group_mean_matrix(BR_WIDTH, HEAD_DIM), _group_mean_matrix(BR_WIDTH, DIFF_DIM), *tabs]
    aliases = {}
    if stacked is None:
        kv_shapes = (jax.ShapeDtypeStruct((n, 2 * BR_WIDTH), F32),) * 6
        kv_specs = (row(2 * BR_WIDTH),) * 6
        extra_shapes, extra_specs = (), ()
    else:
        li, depth, prev = stacked
        seq = pos.shape[0]
        batch = n // seq
        assert tm % MOBA_BLOCK == 0
        kv_shapes = (jax.ShapeDtypeStruct((depth, batch, 2, BR_WIDTH, seq), F32),) * 6
        kv_specs = (pl.BlockSpec((None, None, 2, BR_WIDTH, tm),
                                 lambda i: (li, i // tiles_per_seq, 0, 0, i % tiles_per_seq)),) * 6
        n_ch = tm // NSA_CMP_STRIDE
        extra_shapes = (jax.ShapeDtypeStruct((n // tm, 8, BR_WIDTH), F32),
                        jax.ShapeDtypeStruct((2, batch, N_HEADS, seq // NSA_CMP_STRIDE, NSA_CMP_STRIDE * HEAD_DIM), BF16))
        extra_specs = (pl.BlockSpec((None, 8, BR_WIDTH), lambda i: (i, 0, 0)),
                       pl.BlockSpec((2, None, N_HEADS, n_ch, NSA_CMP_STRIDE * HEAD_DIM),
                                    lambda i: (0, i // tiles_per_seq, 0, i % tiles_per_seq, 0)))
        if prev is not None:
            aliases = {len(args) + k: 2 + k for k in range(6)}
            in_specs += [pl.BlockSpec(memory_space=pl.ANY)] * 6
            args += list(prev)
    out_shape = ((jax.ShapeDtypeStruct((n, d), BF16), jax.ShapeDtypeStruct((n, N_ATT_GROUPS * BR_WIDTH), BF16))
                 + kv_shapes + (jax.ShapeDtypeStruct((n, 128), F32),) + extra_shapes)
    out_specs = (row(d), row(N_ATT_GROUPS * BR_WIDTH)) + kv_specs + (row(128),) + extra_specs
    return pl.pallas_call(
        functools.partial(_inproj_kernel, transposed=stacked is not None, n_alias=len(aliases)),
        grid=(n // tm,),
        in_specs=in_specs,
        out_specs=out_specs,
        out_shape=out_shape,
        scratch_shapes=([pltpu.VMEM((2, 2, tm, PAIR_LANES), F32), pltpu.VMEM((2, tm, BR_WIDTH), F32)]
                        if stacked is not None else []),
        input_output_aliases=aliases,
        compiler_params=_cparams(("parallel",)),
        name="inproj",
    )(*args)


def _tile_gain(g, group):
    return jnp.tile(g.astype(F32), BR_WIDTH // group)


def _cumsum_kernel(x_ref, tri_ref, ft_ref, carry_ref):
    @pl.when(pl.program_id(1) == 0)
    def _():
        carry_ref[...] = jnp.zeros_like(carry_ref)

    c = jnp.dot(tri_ref[...], x_ref[...], precision=lax.Precision.HIGHEST,
                preferred_element_type=F32) + carry_ref[...]
    carry_ref[...] = c[-1:, :]
    ft_ref[...] = jnp.transpose(c)[:8, :]


def _cumsum_t(small, batch, seq, *, tc=256):
    tc = min(tc, seq)
    nt = seq // tc
    tri = jnp.asarray(np.tril(np.ones((tc, tc), np.float32)))
    return pl.pallas_call(
        _cumsum_kernel,
        grid=(batch, nt),
        in_specs=[pl.BlockSpec((tc, 128), lambda b, t: (b * nt + t, 0)), _const_spec((tc, tc))],
        out_specs=pl.BlockSpec((None, 8, tc), lambda b, t: (b, 0, t)),
        out_shape=jax.ShapeDtypeStruct((batch, 8, seq), F32),
        scratch_shapes=[pltpu.VMEM((1, 128), F32)],
        compiler_params=_cparams(("parallel", "arbitrary")),
        name="cumsum_t",
    )(small, tri)


def _flash_steps(seq, tq, tk, window):
    assert tk % tq == 0 and seq % tk == 0
    qi_l, kj_l, fl_l = [], [], []
    for qi in range(seq // tq):
        t0, t1 = qi * tq, (qi + 1) * tq - 1
        js = []
        for j in range(t0 // tk, -1, -1):
            s0, s1 = j * tk, (j + 1) * tk - 1
            if window is not None and s1 <= t0 - window:
                break
            partial = s1 > t0 or (window is not None and s0 <= t1 - window)
            js.append((j, partial))
        for n, (j, partial) in enumerate(js):
            qi_l.append(qi)
            kj_l.append(j)
            fl_l.append((1 if n == 0 else 0) | (2 if n == len(js) - 1 else 0) | (4 if partial else 0))
    return (jnp.asarray(qi_l, jnp.int32), jnp.asarray(kj_l, jnp.int32), jnp.asarray(fl_l, jnp.int32))


def _dot_nt(a, b):
    return lax.dot_general(a, b, (((1,), (1,)), ((), ())), preferred_element_type=F32)


def _flash_kernel(qi_ref, kj_ref, fl_ref, *refs, aug, groups, tq, tk, fox, window, combine, lam_scale):
    refs = list(refs)
    q_ref, k_ref, v_ref = refs[:3]
    refs = refs[3:]
    mask_ref = None if aug else refs.pop(0)
    ft_ref = refs.pop(0) if fox else None
    if combine == "diff":
        lam_ref, bd_ref, gsub_ref = refs[:3]
        refs = refs[3:]
    o_ref, m_ref, acc_ref, qs_ref = refs

    step = pl.program_id(1)
    fl = fl_ref[step]
    qi = qi_ref[step]
    kj = kj_ref[step]
    pair = lambda ref, p: ref[:, p * PAIR_LANES:(p + 1) * PAIR_LANES]

    @pl.when((fl & 1) != 0)
    def _():
        m_ref[...] = jnp.full(m_ref.shape, NEG, F32)
        acc_ref[...] = jnp.zeros(acc_ref.shape, F32)
        if not aug:
            for p in range(2):
                q = pair(q_ref, p)
                for r in range(groups):
                    qs_ref[p * groups + r] = q * mask_ref[r:r + 1, :]

    def body(apply_mask):
        if apply_mask:
            row = qi * tq + lax.broadcasted_iota(jnp.int32, (tq, tk), 0)
            col = kj * tk + lax.broadcasted_iota(jnp.int32, (tq, tk), 1)
            valid = col <= row
            if window is not None:
                valid = jnp.logical_and(valid, col > row - window)
        if fox:
            ft = ft_ref[...] * LOG2E
        for p in range(2):
            v1 = jnp.concatenate([pair(v_ref, p), jnp.ones((tk, PAIR_LANES), BF16)], axis=1)
            for r in range(groups):
                g = p * groups + r
                if aug:
                    s = _dot_nt(q_ref[g], k_ref[g])
                else:
                    s = _dot_nt(qs_ref[g], pair(k_ref, p))
                if fox:
                    s = s - ft[g:g + 1, :]
                if apply_mask:
                    s = jnp.where(valid, s, NEG)
                m_prev = m_ref[g]
                m_new = jnp.maximum(m_prev, jnp.max(s, axis=-1, keepdims=True))
                alpha = jnp.exp2(m_prev - m_new)
                pexp = jnp.exp2(s - m_new)
                acc_ref[g] = alpha * acc_ref[g] + jnp.dot(pexp.astype(BF16), v1, preferred_element_type=F32)
                m_ref[g] = m_new

    masked = (fl & 4) != 0
    pl.when(masked)(lambda: body(True))
    pl.when(jnp.logical_not(masked))(lambda: body(False))

    @pl.when((fl & 2) != 0)
    def _():
        lane = lax.broadcasted_iota(jnp.int32, (tq, PAIR_LANES), 1)
        for p in range(2):
            outs = []
            for r in range(groups):
                a = acc_ref[p * groups + r]
                outs.append(a[:, :PAIR_LANES] / a[:, PAIR_LANES:])
            if combine == "diff":
                lam = lam_ref[0]
                o = jnp.where(lane < HEAD_DIM, outs[0] - lam * outs[1], outs[2] - lam * outs[3])
                o = _group_rms(o, bd_ref[...], gsub_ref[...]) * lam_scale
            else:
                o = jnp.where(lane < HEAD_DIM, outs[0], outs[1])
            o_ref[:, p * PAIR_LANES:(p + 1) * PAIR_LANES] = o.astype(o_ref.dtype)


def _lane_masks(bounds):
    lane = np.arange(PAIR_LANES)
    return jnp.asarray(np.stack([(lane >= a) & (lane < b) for a, b in bounds]).astype(np.float32), BF16)


PAIR_MASKS = ((0, 64), (64, 128))
DIFF_MASKS = ((0, 32), (32, 64), (64, 96), (96, 128))


def _flash(q_src, k_src, v_src, *, batch, seq, tq, tk, q_col=None, k_col=None, v_col, aug=False,
           fox_t=None, window=None, diff=None, out_dtype=BF16):
    tq, tk = min(tq, seq), min(tk, seq)
    nq, nk = seq // tq, seq // tk
    qi_t, kj_t, fl_t = _flash_steps(seq, tq, tk, window)
    n_steps = int(qi_t.shape[0]) // 1
    groups = 4 if diff is not None else 2
    combine = "diff" if diff is not None else "pair"

    def tok(use_q, col):
        tile, n_t = (tq, nq) if use_q else (tk, nk)
        return pl.BlockSpec((tile, BR_WIDTH),
                            lambda b, s, qi, kj, fl: (b * n_t + (qi if use_q else kj)[s], col))

    in_specs, args = [], []
    if aug:
        in_specs += [pl.BlockSpec((None, N_HEADS, tq, PAIR_LANES), lambda b, s, qi, kj, fl: (b, 0, qi[s], 0)),
                     pl.BlockSpec((None, N_HEADS, tk, PAIR_LANES), lambda b, s, qi, kj, fl: (b, 0, kj[s], 0))]
    else:
        in_specs += [tok(True, q_col), tok(False, k_col)]
    in_specs.append(tok(False, v_col))
    args += [q_src, k_src, v_src]
    if not aug:
        in_specs.append(_const_spec((groups, PAIR_LANES)))
        args.append(_lane_masks(DIFF_MASKS if diff is not None else PAIR_MASKS))
    if fox_t is not None:
        in_specs.append(pl.BlockSpec((None, 8, tk), lambda b, s, qi, kj, fl: (b, 0, kj[s])))
        args.append(fox_t)
    lam_scale = 1.0
    if diff is not None:
        lam, lam_scale, gsub = diff
        in_specs += [pl.BlockSpec(memory_space=pltpu.SMEM), _const_spec((PAIR_LANES, PAIR_LANES)),
                     _const_spec((1, PAIR_LANES))]
        args += [lam.reshape(1).astype(F32), _group_mean_matrix(PAIR_LANES, HEAD_DIM), gsub.reshape(1, PAIR_LANES)]

    kern = functools.partial(_flash_kernel, aug=aug, groups=groups, tq=tq, tk=tk, fox=fox_t is not None,
                             window=window, combine=combine, lam_scale=lam_scale)
    return pl.pallas_call(
        kern,
        grid_spec=pltpu.PrefetchScalarGridSpec(
            num_scalar_prefetch=3,
            grid=(batch, n_steps),
            in_specs=in_specs,
            out_specs=pl.BlockSpec((tq, BR_WIDTH), lambda b, s, qi, kj, fl: (b * nq + qi[s], 0)),
            scratch_shapes=[pltpu.VMEM((2 * groups, tq, 1), F32),
                            pltpu.VMEM((2 * groups, tq, 2 * PAIR_LANES), F32),
                            pltpu.VMEM((2 * groups, tq, PAIR_LANES), BF16)]),
        out_shape=jax.ShapeDtypeStruct((batch * seq, BR_WIDTH), out_dtype),
        compiler_params=_cparams(("parallel", "arbitrary")),
        name="flash_" + ("aug" if aug else combine) + ("_fox" if fox_t is not None else "")
             + ("_win" if window is not None else ""),
    )(qi_t, kj_t, fl_t, *args)


def _select_bias(rel, j, own, topk):
    big = jnp.int32(1 << 20)
    val = jnp.where(j < own, jnp.where(j >= 0, rel, NEG), NEG)
    bias = jnp.where(j == own, 0.0, NEG)
    for _ in range(topk):
        mx = jnp.max(val, axis=-1, keepdims=True)
        idx = jnp.min(jnp.where(val == mx, j, big), axis=-1, keepdims=True)
        pick = j == jnp.where(mx > 0.5 * NEG, idx, big)
        bias = jnp.where(pick, 0.0, bias)
        val = jnp.where(pick, NEG, val)
    return bias


def _blockmean_kernel(x_ref, o_ref):
    j = pl.program_id(1)
    o_ref[pl.ds(j, 1), :] = jnp.mean(x_ref[...], axis=0, keepdims=True)


def _blockmean(kv, batch, seq, blk, nb_pad):
    nb = seq // blk
    out = pl.pallas_call(
        _blockmean_kernel,
        grid=(batch, nb),
        in_specs=[pl.BlockSpec((blk, BR_WIDTH), lambda b, j: (b * nb + j, 0))],
        out_specs=pl.BlockSpec((None, nb, BR_WIDTH), lambda b, j: (b, 0, 0)),
        out_shape=jax.ShapeDtypeStruct((batch, nb, BR_WIDTH), F32),
        compiler_params=_cparams(("parallel", "arbitrary")),
        name="blockmean",
    )(kv)
    return jnp.pad(out, ((0, 0), (0, nb_pad - nb), (0, 0)))


def _augment(x, bias_or_onehot, r):
    lane = lax.broadcasted_iota(jnp.int32, x.shape, 1)
    mine = (lane < HEAD_DIM) if r == 0 else (lane >= HEAD_DIM)
    return jnp.where(mine, x.astype(F32), bias_or_onehot).astype(BF16)


def _payload_block_id(shape, r):
    lane = lax.broadcasted_iota(jnp.int32, shape, 1)
    return (lane - HEAD_DIM) if r == 0 else jnp.where(lane < HEAD_DIM, lane, -1)


def _moba_router_kernel(q_ref, k_ref, km_ref, mask_ref, qa_ref, ka_ref, *, tq, blk, nb_pad):
    qi = pl.program_id(2)
    q = q_ref[...]
    k = k_ref[...]
    row = qi * tq + lax.broadcasted_iota(jnp.int32, (tq, 1), 0)
    own = row // blk
    for r in range(2):
        km = (km_ref[...] * mask_ref[r:r + 1, :].astype(F32)).astype(BF16)
        lo = HEAD_DIM if r == 0 else 0
        pieces = [km, jnp.zeros((PAIR_LANES - nb_pad, PAIR_LANES), BF16)]
        if lo:
            pieces = [jnp.zeros((lo, PAIR_LANES), BF16), km, jnp.zeros((PAIR_LANES - lo - nb_pad, PAIR_LANES), BF16)]
        rel = _dot_nt(q * mask_ref[r:r + 1, :], jnp.concatenate(pieces, axis=0))
        j = _payload_block_id((tq, PAIR_LANES), r)
        bias = _select_bias(rel, j, own, MOBA_TOPK)
        qa_ref[r] = _augment(q, bias, r)
        ka_ref[r] = _augment(k, (j == own).astype(F32), r)


def _moba_router(att, kmean, batch, seq, *, tq):
    tq = min(tq, seq)
    nq = seq // tq
    nb_pad = kmean.shape[1]
    out = jax.ShapeDtypeStruct((batch, N_HEADS, seq, PAIR_LANES), BF16)
    aug_spec = pl.BlockSpec((None, 2, tq, PAIR_LANES), lambda b, p, i: (b, p, i, 0))
    return pl.pallas_call(
        functools.partial(_moba_router_kernel, tq=tq, blk=MOBA_BLOCK, nb_pad=nb_pad),
        grid=(batch, 2, nq),
        in_specs=[pl.BlockSpec((tq, PAIR_LANES), lambda b, p, i: (b * nq + i, G_QC * 2 + p)),
                  pl.BlockSpec((tq, PAIR_LANES), lambda b, p, i: (b * nq + i, G_KC * 2 + p)),
                  pl.BlockSpec((None, nb_pad, PAIR_LANES), lambda b, p, i: (b, 0, p)),
                  _const_spec((2, PAIR_LANES))],
        out_specs=(aug_spec, aug_spec),
        out_shape=(out, out),
        compiler_params=_cparams(("parallel", "parallel", "parallel")),
        name="moba_router",
    )(att, att, kmean, _lane_masks(PAIR_MASKS))


def _compress_kernel(c_ref, w1_ref, b1_ref, w2_ref, b2_ref, gain_ref, o_ref, *, rows):
    kv = pl.program_id(0)
    uv = jnp.dot(c_ref[...].astype(BF16), w1_ref[...], preferred_element_type=F32)
    hid = uv.shape[1] // 2
    pre = uv[:, :hid] + pltpu.roll(uv[:, hid:], rows - 1, 0) + b1_ref[...]
    y = jnp.dot(jax.nn.gelu(pre, approximate=True).astype(BF16), w2_ref[...],
                preferred_element_type=F32) + b2_ref[...]
    ms = jnp.mean(y * y, axis=-1, keepdims=True)
    yn = y * lax.rsqrt(ms + EPS) * gain_ref[...]
    o_ref[...] = jnp.where(kv == 0, yn, y)


def _compress(chunks, pos_emb, w1, b1, w2, b2, k_gain, *, chunks_per_seq, seqs_per_tile):
    _, r, cw = chunks.shape
    hid = w1.shape[-1]
    rows = chunks_per_seq * seqs_per_tile
    assert r % rows == 0
    w1ab, bias1, w2, b2, gain = _compress_weights(pos_emb, w1, b1, w2, b2, k_gain)
    return pl.pallas_call(
        functools.partial(_compress_kernel, rows=rows),
        grid=(2, r // rows),
        in_specs=[pl.BlockSpec((None, rows, cw), lambda kv, i: (kv, i, 0)),
                  pl.BlockSpec((None, cw, 2 * hid), lambda kv, i: (kv, 0, 0)),
                  pl.BlockSpec((None, 1, hid), lambda kv, i: (kv, 0, 0)),
                  pl.BlockSpec((None, hid, HEAD_DIM), lambda kv, i: (kv, 0, 0)),
                  pl.BlockSpec((None, 1, HEAD_DIM), lambda kv, i: (kv, 0, 0)),
                  _const_spec((1, HEAD_DIM))],
        out_specs=pl.BlockSpec((None, rows, HEAD_DIM), lambda kv, i: (kv, i, 0)),
        out_shape=jax.ShapeDtypeStruct((2, r, HEAD_DIM), F32),
        compiler_params=_cparams(("parallel", "parallel")),
        name="nsa_compress",
    )(chunks, w1ab, bias1, w2, b2, gain)


def _compress_weights(pos_emb, w1, b1, w2, b2, k_gain):
    cw = w1.shape[1] // 2
    hid = w1.shape[-1]
    w1ab = jnp.concatenate([w1[:, :cw], w1[:, cw:]], axis=-1).astype(BF16)
    bias1 = (jnp.einsum('kf,kfh->kh', pos_emb.reshape(2, -1).astype(F32), w1.astype(F32),
                        precision=lax.Precision.HIGHEST) + b1.astype(F32)).reshape(2, 1, hid)
    return (w1ab, bias1, w2.astype(BF16), b2.astype(F32).reshape(2, 1, HEAD_DIM),
            k_gain.astype(F32).reshape(1, HEAD_DIM))


def _cmp_to_slc_map(n_cmp_pad, n_cmp):
    start = np.arange(n_cmp_pad)[:, None] * NSA_CMP_STRIDE
    blk = np.arange(HEAD_DIM)[None, :] * NSA_SLC_BLOCK
    ov = np.minimum(start + NSA_CMP_LEN, blk + NSA_SLC_BLOCK) - np.maximum(start, blk)
    m = np.maximum(ov, 0).astype(np.float32) / NSA_CMP_LEN
    m[n_cmp:] = 0.0
    z = np.zeros_like(m)
    return jnp.asarray(np.stack([np.concatenate([z, m], 1), np.concatenate([m, z], 1)]), BF16)


def _nsa_cmp_kernel(qn_ref, q_ref, k_ref, kc_ref, vc_ref, map_ref, mask_ref,
                    oc_ref, qa_ref, ka_ref, *, tq, n_pad):
    qi = pl.program_id(2)
    qn = qn_ref[...]
    q = q_ref[...]
    k = k_ref[...]
    kc = kc_ref[...].astype(BF16)
    vc = vc_ref[...].astype(BF16)
    row = qi * tq + lax.broadcasted_iota(jnp.int32, (tq, 1), 0)
    own = row // NSA_SLC_BLOCK
    cmp_end = lax.broadcasted_iota(jnp.int32, (tq, n_pad), 1) * NSA_CMP_STRIDE + (NSA_CMP_LEN - 1)
    visible = cmp_end <= row
    outs = []
    for r in range(2):
        s = jnp.where(visible, _dot_nt(qn * mask_ref[r:r + 1, :], kc), NEG)
        e = jnp.where(visible, jnp.exp2(s - jnp.max(s, axis=-1, keepdims=True)), 0.0)
        p = e / jnp.maximum(jnp.sum(e, axis=-1, keepdims=True), 1e-30)
        p_hi = p.astype(BF16)
        p_lo = (p - p_hi.astype(F32)).astype(BF16)
        outs.append(jnp.dot(p_hi, vc, preferred_element_type=F32))
        rel = (jnp.dot(p_hi, map_ref[r], preferred_element_type=F32)
               + jnp.dot(p_lo, map_ref[r], preferred_element_type=F32))
        j = _payload_block_id((tq, PAIR_LANES), r)
        bias = _select_bias(rel, j, own, NSA_TOPK)
        qa_ref[r] = _augment(q, bias, r)
        ka_ref[r] = _augment(k, (j == own).astype(F32), r)
    lane = lax.broadcasted_iota(jnp.int32, (tq, PAIR_LANES), 1)
    oc_ref[...] = jnp.where(lane < HEAD_DIM, outs[0], outs[1])


def _nsa_cmp(att, cmp_kv, batch, seq, n_cmp, *, tq):
    tq = min(tq, seq)
    nq = seq // tq
    n_pad = cmp_kv.shape[1]
    aug = jax.ShapeDtypeStruct((batch, N_HEADS, seq, PAIR_LANES), BF16)
    aug_spec = pl.BlockSpec((None, 2, tq, PAIR_LANES), lambda b, p, i: (b, p, i, 0))
    tok = lambda g: pl.BlockSpec((tq, PAIR_LANES), lambda b, p, i: (b * nq + i, g * 2 + p))
    return pl.pallas_call(
        functools.partial(_nsa_cmp_kernel, tq=tq, n_pad=n_pad),
        grid=(batch, 2, nq),
        in_specs=[tok(G_QDN), tok(G_QD), tok(G_KDS),
                  pl.BlockSpec((None, n_pad, PAIR_LANES), lambda b, p, i: (b, 0, p)),
                  pl.BlockSpec((None, n_pad, PAIR_LANES), lambda b, p, i: (b, 0, 2 + p)),
                  _const_spec((2, n_pad, PAIR_LANES)), _const_spec((2, PAIR_LANES))],
        out_specs=(pl.BlockSpec((tq, PAIR_LANES), lambda b, p, i: (b * nq + i, p)), aug_spec, aug_spec),
        out_shape=(jax.ShapeDtypeStruct((batch * seq, BR_WIDTH), F32), aug, aug),
        compiler_params=_cparams(("parallel", "parallel", "parallel")),
        name="nsa_cmp",
    )(att, att, att, cmp_kv, cmp_kv, _cmp_to_slc_map(n_pad, n_cmp), _lane_masks(PAIR_MASKS))


def _resident_spec(shape):
    nd = len(shape)
    return pl.BlockSpec(shape, lambda *_: (0,) * nd, pipeline_mode=pl.Buffered(1))


def _gate_expand_matrix():
    m = np.zeros((3, 128, BR_WIDTH), np.float32)
    for c in range(3):
        for h in range(N_HEADS):
            m[c, N_HEADS + 3 * h + c, h * HEAD_DIM:(h + 1) * HEAD_DIM] = 1.0
    return jnp.asarray(m, BF16)


def _merge_kernel(x_ref, hb_ref, oa_ref, ob_ref, oc_ref, odc_ref, ods_ref, odw_ref, small_ref,
                  g1_ref, sc2_ref, sh2_ref, gn2_ref, ex_ref, wbr_ref, wg_ref, bg_ref, wo_ref,
                  xo_ref, h2_ref):
    d = x_ref.shape[1]
    sm = small_ref[...]
    sm_hi = sm.astype(BF16)
    sm_lo = (sm - sm_hi.astype(F32)).astype(BF16)
    o_d = jnp.zeros(odc_ref.shape, F32)
    for c, ref in enumerate((odc_ref, ods_ref, odw_ref)):
        gate = (jnp.dot(sm_hi, ex_ref[c], preferred_element_type=F32)
                + jnp.dot(sm_lo, ex_ref[c], preferred_element_type=F32))
        o_d = o_d + gate * ref[...]
    hb = hb_ref[...]
    branches = (oa_ref[...], ob_ref[...], oc_ref[...], o_d.astype(BF16))
    merged = jnp.zeros((x_ref.shape[0], d), F32)
    for g, o in enumerate(branches):
        br = jnp.dot(o, wbr_ref[g], preferred_element_type=F32)
        z = jnp.dot(hb, wg_ref[:, g * d:(g + 1) * d], preferred_element_type=F32) + bg_ref[:, g * d:(g + 1) * d]
        merged = merged + jax.nn.sigmoid(z) * br
    y = jnp.dot(merged.astype(BF16), wo_ref[...], preferred_element_type=F32)
    x = x_ref[...] + g1_ref[...] * y
    xo_ref[...] = x
    ms = jnp.mean(x * x, axis=-1, keepdims=True)
    h2 = (x * lax.rsqrt(ms + EPS) * gn2_ref[...]) * (1.0 + sc2_ref[...]) + sh2_ref[...]
    h2_ref[...] = h2.astype(BF16)


def _mod_spec(mod, d, tm, rows_per_mod):
    if rows_per_mod == 1:
        return pl.BlockSpec((tm, d), lambda i: (i, 0)), mod
    tpm = rows_per_mod // tm
    return pl.BlockSpec((None, 1, d), lambda i: (i // tpm, 0, 0)), mod.reshape(-1, 1, d)


def _merge(x2d, hb, o_a, o_b, o_c, o_dc, o_ds, o_dw, small, g1, sc2, sh2, gn2,
           w_branch, w_gate, b_gate, w_out, *, rows_per_mod, tm):
    n, d = x2d.shape
    tm = min(tm, n)
    row = lambda w: pl.BlockSpec((tm, w), lambda i: (i, 0))
    mods = [_mod_spec(m, d, tm, rows_per_mod) for m in (g1, sc2, sh2)]
    return pl.pallas_call(
        _merge_kernel,
        grid=(n // tm,),
        in_specs=[row(d), row(d)] + [row(BR_WIDTH)] * 6 + [row(128)] + [m[0] for m in mods]
                 + [_const_spec((1, d)), _const_spec((3, 128, BR_WIDTH)),
                    _resident_spec(w_branch.shape), _resident_spec(w_gate.shape),
                    _const_spec((1, 4 * d)), _resident_spec(w_out.shape)],
        out_specs=(row(d), row(d)),
        out_shape=(jax.ShapeDtypeStruct((n, d), F32), jax.ShapeDtypeStruct((n, d), BF16)),
        compiler_params=_cparams(("parallel",)),
        name="merge",
    )(x2d, hb, o_a, o_b, o_c, o_dc, o_ds, o_dw, small, *[m[1] for m in mods],
      gn2.reshape(1, d).astype(F32), _gate_expand_matrix(),
      w_branch.astype(BF16), w_gate.astype(BF16), b_gate.reshape(1, 4 * d).astype(F32), w_out.astype(BF16))


def _swiglu_acc(h, wg_ref, wu_ref, wd_ref, chunk):
    ff = wg_ref.shape[-1]
    acc = None
    for c0 in range(0, ff, chunk):
        g = jnp.dot(h, wg_ref[:, c0:c0 + chunk], preferred_element_type=F32)
        u = jnp.dot(h, wu_ref[:, c0:c0 + chunk], preferred_element_type=F32)
        part = jnp.dot((g * jax.nn.sigmoid(g) * u).astype(BF16), wd_ref[c0:c0 + chunk, :],
                       preferred_element_type=F32)
        acc = part if acc is None else acc + part
    return acc


def _ffn_kernel(x_ref, h_ref, g2_ref, wg_ref, wu_ref, wd_ref, o_ref, *, chunk):
    o_ref[...] = x_ref[...] + g2_ref[...] * _swiglu_acc(h_ref[...], wg_ref, wu_ref, wd_ref, chunk)


def _ff_chunk(ff):
    for c in (1408, 1024, 896, 512, 256, 128):
        if ff % c == 0:
            return c
    return ff


def _ffn(x2d, h2, g2, wg, wu, wd, *, rows_per_mod, tm):
    n, d = x2d.shape
    tm = min(tm, n)
    row = lambda w: pl.BlockSpec((tm, w), lambda i: (i, 0))
    g2_spec, g2 = _mod_spec(g2, d, tm, rows_per_mod)
    return pl.pallas_call(
        functools.partial(_ffn_kernel, chunk=_ff_chunk(wg.shape[1])),
        grid=(n // tm,),
        in_specs=[row(d), row(d), g2_spec, _resident_spec(wg.shape), _resident_spec(wu.shape),
                  _resident_spec(wd.shape)],
        out_specs=row(d),
        out_shape=jax.ShapeDtypeStruct((n, d), F32),
        compiler_params=_cparams(("parallel",)),
        name="ffn",
    )(x2d, h2, g2, wg.astype(BF16), wu.astype(BF16), wd.astype(BF16))


def _route_top2(logits, n_exp):
    lane = lax.broadcasted_iota(jnp.int32, logits.shape, 1)
    real = lane < n_exp
    logits = jnp.where(real, logits, NEG)
    ex = jnp.exp(logits - jnp.max(logits, axis=-1, keepdims=True))
    prob = ex / jnp.sum(ex, axis=-1, keepdims=True)
    avail = real
    comb = jnp.zeros(prob.shape, F32)
    for _ in range(2):
        val = jnp.where(avail, prob, -1.0)
        mx = jnp.max(val, axis=-1, keepdims=True)
        idx = jnp.min(jnp.where(jnp.logical_and(avail, val == mx), lane, 1 << 20), axis=-1, keepdims=True)
        pick = lane == idx
        comb = jnp.where(pick, prob, comb)
        avail = jnp.logical_and(avail, jnp.logical_not(pick))
    return comb / jnp.sum(comb, axis=-1, keepdims=True)


def _moe_sparse_kernel(h_ref, wr_ref, br_ref, ltri_ref, wg_ref, wu_ref, wd_ref, o_ref,
                       comb_ref, rank_ref, rank_t_ref, comb_t_ref, *, chunk, n_exp, cap):
    e = pl.program_id(1)
    tm = h_ref.shape[0]
    h = h_ref[...]

    @pl.when(e == 0)
    def _():
        comb = _route_top2(jnp.dot(h, wr_ref[...], preferred_element_type=F32) + br_ref[...], n_exp)
        routed = jnp.where(comb > 0.0, 1.0, 0.0)
        rank = jnp.dot(ltri_ref[...], routed.astype(BF16), preferred_element_type=F32)
        comb_ref[...] = comb
        rank_ref[...] = rank
        blk = min(tm, 128)
        for i in range(tm // blk):
            rows = slice(i * blk, (i + 1) * blk)
            rank_t_ref[:, rows] = jnp.transpose(rank[rows, :])
            comb_t_ref[:, rows] = jnp.transpose(comb[rows, :])
        o_ref[...] = jnp.zeros(o_ref.shape, F32)

    lane = lax.broadcasted_iota(jnp.int32, (tm, 128), 1)
    mine = lane == e
    w_col = jnp.sum(jnp.where(mine, comb_ref[...], 0.0), axis=-1, keepdims=True)
    key_col = jnp.where(w_col > 0.0, jnp.sum(jnp.where(mine, rank_ref[...], 0.0), axis=-1, keepdims=True), -1.0)
    key_row = jnp.where(comb_t_ref[pl.ds(e, 1), :] > 0.0, rank_t_ref[pl.ds(e, 1), :], -1.0)
    count = jnp.sum(jnp.where(w_col > 0.0, 1.0, 0.0))

    for c in range(-(-tm // cap)):
        @pl.when(count > c * cap)
        def _(c=c):
            slot_r = (lax.broadcasted_iota(jnp.int32, (cap, tm), 0) + c * cap).astype(F32)
            gather = jnp.where(key_row == slot_r, 1.0, 0.0).astype(BF16)
            hc = jnp.dot(gather, h, preferred_element_type=F32).astype(BF16)
            y = _swiglu_acc(hc, wg_ref, wu_ref, wd_ref, chunk)
            slot_c = (lax.broadcasted_iota(jnp.int32, (tm, cap), 1) + c * cap).astype(F32)
            scatter = jnp.where(key_col == slot_c, 1.0, 0.0).astype(BF16)
            o_ref[...] += w_col * jnp.dot(scatter, y.astype(BF16), preferred_element_type=F32)


def _moe_sparse(h2, router, router_b, wg, wu, wd, *, tm, cap):
    n, d = h2.shape
    n_exp = wg.shape[0]
    tm = min(tm, n)
    cap = min(cap, tm)
    assert n % tm == 0 and tm % min(tm, 128) == 0
    wr = jnp.pad(router.astype(BF16), ((0, 0), (0, 128 - n_exp)))
    br = jnp.pad(router_b.astype(F32), (0, 128 - n_exp)).reshape(1, 128)
    ltri = jnp.asarray(np.tril(np.ones((tm, tm), np.float32), -1), BF16)
    exp_spec = lambda a: pl.BlockSpec((None,) + a.shape[1:], lambda i, e: (e, 0, 0))
    cst = lambda shape: pl.BlockSpec(shape, lambda i, e: (0,) * len(shape))
    return pl.pallas_call(
        functools.partial(_moe_sparse_kernel, chunk=_ff_chunk(wg.shape[2]), n_exp=n_exp, cap=cap),
        grid=(n // tm, n_exp),
        in_specs=[pl.BlockSpec((tm, d), lambda i, e: (i, 0)), cst((d, 128)), cst((1, 128)), cst((tm, tm)),
                  exp_spec(wg), exp_spec(wu), exp_spec(wd)],
        out_specs=pl.BlockSpec((tm, d), lambda i, e: (i, 0)),
        out_shape=jax.ShapeDtypeStruct((n, d), F32),
        scratch_shapes=[pltpu.VMEM((tm, 128), F32), pltpu.VMEM((tm, 128), F32),
                        pltpu.VMEM((128, tm), F32), pltpu.VMEM((128, tm), F32)],
        compiler_params=_cparams(("parallel", "arbitrary")),
        name="moe_sparse",
    )(h2, wr, br, ltri, wg.astype(BF16), wu.astype(BF16), wd.astype(BF16))


def _residual_kernel(x_ref, y_ref, g_ref, o_ref):
    o_ref[...] = x_ref[...] + g_ref[...] * y_ref[...]


def _residual(x2d, y, g, *, rows_per_mod, tm):
    n, d = x2d.shape
    tm = min(tm, n)
    row = pl.BlockSpec((tm, d), lambda i: (i, 0))
    g_spec, g = _mod_spec(g, d, tm, rows_per_mod)
    return pl.pallas_call(
        _residual_kernel, grid=(n // tm,), in_specs=[row, row, g_spec], out_specs=row,
        out_shape=jax.ShapeDtypeStruct((n, d), F32), compiler_params=_cparams(("parallel",)),
        name="residual",
    )(x2d, y, g)


def _moe_kernel(x_ref, h_ref, g2_ref, wr_ref, br_ref, wg_ref, wu_ref, wd_ref, o_ref,
                comb_ref, acc_ref, *, chunk, n_exp):
    e = pl.program_id(1)
    h = h_ref[...]

    @pl.when(e == 0)
    def _():
        logits = jnp.dot(h, wr_ref[...], preferred_element_type=F32) + br_ref[...]
        lane = lax.broadcasted_iota(jnp.int32, logits.shape, 1)
        real = lane < n_exp
        logits = jnp.where(real, logits, NEG)
        ex = jnp.exp(logits - jnp.max(logits, axis=-1, keepdims=True))
        prob = ex / jnp.sum(ex, axis=-1, keepdims=True)
        avail = real
        comb = jnp.zeros(prob.shape, F32)
        for _ in range(2):
            val = jnp.where(avail, prob, -1.0)
            mx = jnp.max(val, axis=-1, keepdims=True)
            idx = jnp.min(jnp.where(jnp.logical_and(avail, val == mx), lane, 1 << 20), axis=-1, keepdims=True)
            pick = lane == idx
            comb = jnp.where(pick, prob, comb)
            avail = jnp.logical_and(avail, jnp.logical_not(pick))
        comb_ref[...] = comb / jnp.sum(comb, axis=-1, keepdims=True)
        acc_ref[...] = jnp.zeros(acc_ref.shape, F32)

    lane = lax.broadcasted_iota(jnp.int32, comb_ref.shape, 1)
    w_e = jnp.sum(jnp.where(lane == e, comb_ref[...], 0.0), axis=-1, keepdims=True)

    @pl.when(jnp.max(w_e) > 0.0)
    def _():
        acc_ref[...] += w_e * _swiglu_acc(h, wg_ref, wu_ref, wd_ref, chunk)

    @pl.when(e == n_exp - 1)
    def _():
        o_ref[...] = x_ref[...] + g2_ref[...] * acc_ref[...]


def _moe(x2d, h2, g2, router, router_b, wg, wu, wd, *, rows_per_mod, tm):
    n, d = x2d.shape
    n_exp = wg.shape[0]
    tm = min(tm, n)
    row = lambda w: pl.BlockSpec((tm, w), lambda i, e: (i, 0))
    if rows_per_mod == 1:
        g2_spec = row(d)
    else:
        tpm = rows_per_mod // tm
        g2_spec = pl.BlockSpec((None, 1, d), lambda i, e: (i // tpm, 0, 0))
        g2 = g2.reshape(-1, 1, d)
    wr = jnp.pad(router.astype(BF16), ((0, 0), (0, 128 - n_exp)))
    br = jnp.pad(router_b.astype(F32), (0, 128 - n_exp)).reshape(1, 128)
    exp_spec = lambda a: pl.BlockSpec((None,) + a.shape[1:], lambda i, e: (e, 0, 0))
    return pl.pallas_call(
        functools.partial(_moe_kernel, chunk=_ff_chunk(wg.shape[2]), n_exp=n_exp),
        grid=(n // tm, n_exp),
        in_specs=[row(d), row(d), g2_spec,
                  pl.BlockSpec((d, 128), lambda i, e: (0, 0)), pl.BlockSpec((1, 128), lambda i, e: (0, 0)),
                  exp_spec(wg), exp_spec(wu), exp_spec(wd)],
        out_specs=row(d),
        out_shape=jax.ShapeDtypeStruct((n, d), F32),
        scratch_shapes=[pltpu.VMEM((tm, 128), F32), pltpu.VMEM((tm, d), F32)],
        compiler_params=_cparams(("parallel", "arbitrary")),
        name="moe",
    )(x2d, h2, g2, wr, br, wg.astype(BF16), wu.astype(BF16), wd.astype(BF16))


def _decode_row_masks(diff):
    lane = np.arange(BR_WIDTH)
    qm = np.zeros((8, BR_WIDTH), np.float32)
    om = np.zeros((8, BR_WIDTH), np.float32)
    for r in range(8 if diff else N_HEADS):
        h = r // 2 if diff else r
        lo, hi = (h * HEAD_DIM + (r % 2) * DIFF_DIM, h * HEAD_DIM + (r % 2 + 1) * DIFF_DIM) if diff \
            else (h * HEAD_DIM, (h + 1) * HEAD_DIM)
        qm[r] = (lane >= lo) & (lane < hi)
        om[r] = (lane >= h * HEAD_DIM) & (lane < (h + 1) * HEAD_DIM)
    return jnp.asarray(qm, BF16), jnp.asarray(om, F32)


def _segment_matrices(n_keys, blk):
    seg = (np.arange(n_keys)[:, None] // blk == np.arange(128)[None, :]).astype(np.float32)
    return jnp.asarray(seg / blk, F32), jnp.asarray(seg.T, BF16)


def _decode_kernel(pt_ref, *refs, n_pg, spb, dense, mode, fox, diff, lam_scale, skip_first, shift_out, n_alias):
    del pt_ref
    refs = list(refs)
    q_ref, kvn_ref = refs[:2]
    n_in = 1 if dense else spb * n_pg
    pages = refs[2:2 + n_in]
    dense_ref = pages[0] if dense else None
    refs = refs[2 + n_in:]
    if fox:
        lf_pages = refs[:n_in]
        smalln_ref, diag_ref = refs[n_in:n_in + 2]
        refs = refs[n_in + 2:]
    if mode == "sel":
        sel_ref = refs.pop(0)
    qm_ref, om_ref = refs[:2]
    refs = refs[2:]
    if mode == "sel":
        segt_ref = refs.pop(0)
    if diff:
        lam_ref, bd_ref, gsub_ref = refs[:3]
        refs = refs[3:]
    if shift_out:
        newt_ref = refs.pop(0)
        refs = refs[n_alias:]
        o_ref, win_ref = refs
    else:
        (o_ref,) = refs

    for i in range(spb):
        sample_pages = [dense_ref.at[i]] if dense else pages[i * n_pg:(i + 1) * n_pg]
        rows = sample_pages[0].shape[-1]
        if shift_out:
            lane_s = lax.broadcasted_iota(jnp.int32, newt_ref.shape, 1)
            col = jnp.sum(jnp.where(lane_s == pl.program_id(0) * spb + i, newt_ref[...], 0.0),
                          axis=-1, keepdims=True)
            lane = lax.broadcasted_iota(jnp.int32, (BR_WIDTH, rows), 1)
            for kv in range(2):
                win_ref[i, kv] = jnp.where(lane == rows - 1, col[kv * BR_WIDTH:(kv + 1) * BR_WIDTH],
                                           pltpu.roll(sample_pages[0][kv], rows - 1, 1))

        qrows = q_ref[i:i + 1, :] * qm_ref[...]
        kn = kvn_ref[i:i + 1, :BR_WIDTH].astype(BF16).astype(F32)
        vn = kvn_ref[i:i + 1, BR_WIDTH:].astype(BF16).astype(F32)
        s_new = jnp.sum(qrows.astype(F32) * kn, axis=-1, keepdims=True)
        s = jnp.concatenate([jnp.dot(qrows, pg[0].astype(BF16), preferred_element_type=F32)
                             for pg in sample_pages], axis=1)
        n_keys = s.shape[1]
        if fox:
            carry = jnp.sum(diag_ref[...] * smalln_ref[i:i + 1, :], axis=-1, keepdims=True)
            lane_p = lax.broadcasted_iota(jnp.int32, (8, rows), 1)
            parts = []
            for lf_ref in reversed(lf_pages[i * n_pg:(i + 1) * n_pg]):
                lf = jnp.concatenate([lf_ref[...], jnp.zeros((8 - N_HEADS, rows), F32)], axis=0)
                suf = lf
                k = 1
                while k < rows:
                    suf = suf + jnp.where(lane_p < rows - k, pltpu.roll(suf, rows - k, 1), 0.0)
                    k *= 2
                parts.append(suf - lf + carry)
                carry = carry + jnp.sum(lf, axis=-1, keepdims=True)
            s = s + jnp.concatenate(parts[::-1], axis=1) * LOG2E
        keep = None
        if mode == "moba":
            nb = n_keys // MOBA_BLOCK
            lane = lax.broadcasted_iota(jnp.int32, (8, 128), 1)
            rel = jnp.zeros((8, 128), F32)
            for b in range(nb):
                tot = jnp.sum(s[:, b * MOBA_BLOCK:(b + 1) * MOBA_BLOCK], axis=-1, keepdims=True)
                rel = jnp.where(lane == b, tot, rel)
            picked = _select_bias(rel, jnp.where(lane < nb, lane, -1), jnp.full((8, 1), nb, jnp.int32), MOBA_TOPK)
            chosen = jnp.where(jnp.logical_and(picked == 0.0, lane < nb), 1.0, 0.0)
            s = jnp.concatenate(
                [jnp.where(jnp.sum(jnp.where(lane == b, chosen, 0.0), axis=-1, keepdims=True) > 0.5,
                           s[:, b * MOBA_BLOCK:(b + 1) * MOBA_BLOCK], NEG) for b in range(nb)], axis=1)
        elif mode == "sel":
            keep = jnp.dot(sel_ref[i].astype(BF16), segt_ref[...], preferred_element_type=F32) > 0.5
        if skip_first:
            fresh = lax.broadcasted_iota(jnp.int32, s.shape, 1) >= skip_first
            keep = fresh if keep is None else jnp.logical_and(keep, fresh)
        if keep is not None:
            s = jnp.where(keep, s, NEG)
        m = jnp.maximum(jnp.max(s, axis=-1, keepdims=True), s_new)
        e = jnp.exp2(s - m)
        e_new = jnp.exp2(s_new - m)
        denom = jnp.sum(e, axis=-1, keepdims=True) + e_new
        eb = e.astype(BF16)
        o = e_new.astype(BF16).astype(F32) * vn
        for j, pg in enumerate(sample_pages):
            o = o + _dot_nt(eb[:, j * rows:(j + 1) * rows], pg[1].astype(BF16))
        o = o / denom
        if diff:
            row = lax.broadcasted_iota(jnp.int32, (8, 1), 0)
            o = o * jnp.where(row % 2 == 0, 1.0, -lam_ref[0])
        out = jnp.sum(o * om_ref[...], axis=0, keepdims=True)
        if diff:
            out = _group_rms(out, bd_ref[...], gsub_ref[...]) * lam_scale
        o_ref[i:i + 1, :] = out


def _decode_attn(att3, q_col, kvn3, cache_t, li, page_table, *, mode="plain", fox=None, diff=None,
                 sel=None, blk=None, dense=False, skip_first=0, shift_out=None):
    s_n = att3.shape[0]
    spb = DECODE_SAMPLES_PER_STEP if s_n % DECODE_SAMPLES_PER_STEP == 0 else 1
    grp = lambda a: a.reshape((s_n // spb, spb) + a.shape[2:])
    rows = cache_t.shape[-1]
    n_pg = 1 if dense else page_table.shape[1]
    n_keys = n_pg * rows
    pt = page_table.reshape(-1).astype(jnp.int32)
    slots = [(i, j) for i in range(spb) for j in range(n_pg)]
    if dense:
        page_specs = [pl.BlockSpec((None, spb, 2, BR_WIDTH, rows), lambda b, pt: (li, b, 0, 0, 0))]
    else:
        page_specs = [pl.BlockSpec((None, None, 2, BR_WIDTH, rows),
                                   lambda b, pt, i=i, j=j: (li, pt[(b * spb + i) * n_pg + j], 0, 0, 0))
                      for i, j in slots]
    in_specs = [pl.BlockSpec((None, spb, BR_WIDTH), lambda b, pt: (b, 0, q_col)),
                pl.BlockSpec((None, spb, 2 * BR_WIDTH), lambda b, pt: (b, 0, 0))] + page_specs
    args = [grp(att3), grp(kvn3)] + [cache_t] * len(page_specs)
    cst = lambda a: (pl.BlockSpec(a.shape, lambda b, pt: (0,) * a.ndim), a)
    consts = []
    if fox is not None:
        logf_t, small3 = fox
        in_specs += [pl.BlockSpec((None, None, N_HEADS, rows),
                                  lambda b, pt, i=i, j=j: (li, pt[(b * spb + i) * n_pg + j], 0, 0)) for i, j in slots]
        args += [logf_t] * len(slots)
        in_specs.append(pl.BlockSpec((None, spb, 128), lambda b, pt: (b, 0, 0)))
        args.append(grp(small3))
        consts += [cst(jnp.asarray(np.eye(8, 128, dtype=np.float32) * (np.arange(8)[:, None] < N_HEADS)))]
    qm, om = _decode_row_masks(diff is not None)
    consts += [cst(qm), cst(om)]
    if mode == "sel":
        in_specs.append(pl.BlockSpec((spb, 8, 128), lambda b, pt: (b, 0, 0)))
        args.append(sel)
        consts.append(cst(_segment_matrices(n_keys, blk)[1]))
    in_specs += [c[0] for c in consts]
    args += [c[1] for c in consts]
    lam_scale = 1.0
    if diff is not None:
        lam, lam_scale, gsub = diff
        bd = _group_mean_matrix(BR_WIDTH, HEAD_DIM)
        in_specs += [pl.BlockSpec(memory_space=pltpu.SMEM), pl.BlockSpec(bd.shape, lambda b, pt: (0, 0)),
                     pl.BlockSpec((1, BR_WIDTH), lambda b, pt: (0, 0))]
        args += [lam.reshape(1).astype(F32), bd, gsub.reshape(1, BR_WIDTH)]
    out_specs = pl.BlockSpec((None, spb, BR_WIDTH), lambda b, pt: (b, 0, 0))
    out_shape = jax.ShapeDtypeStruct((s_n // spb, spb, BR_WIDTH), F32)
    aliases = {}
    if shift_out is not None:
        new_t, prev, depth = shift_out
        in_specs.append(pl.BlockSpec(new_t.shape, lambda b, pt: (0, 0)))
        args.append(new_t)
        if prev is not None:
            aliases = {len(args) + 1: 1}
            in_specs.append(pl.BlockSpec(memory_space=pl.ANY))
            args.append(prev)
        out_specs = (out_specs, pl.BlockSpec((None, spb, 2, BR_WIDTH, rows), lambda b, pt: (li, b, 0, 0, 0)))
        out_shape = (out_shape, jax.ShapeDtypeStruct((depth, s_n, 2, BR_WIDTH, rows), F32))
    kern = functools.partial(_decode_kernel, n_pg=n_pg, spb=spb, dense=dense, mode=mode, fox=fox is not None,
                             diff=diff is not None, lam_scale=lam_scale, skip_first=skip_first,
                             shift_out=shift_out is not None, n_alias=len(aliases))
    res = pl.pallas_call(
        kern,
        grid_spec=pltpu.PrefetchScalarGridSpec(
            num_scalar_prefetch=1, grid=(s_n // spb,), in_specs=in_specs, out_specs=out_specs),
        out_shape=out_shape,
        input_output_aliases=aliases,
        compiler_params=_cparams(("parallel",)),
        name="decode_" + mode + ("_fox" if fox is not None else "") + ("_diff" if diff is not None else "")
             + ("_dense" if dense else ""),
    )(pt, *args)
    if shift_out is not None:
        return res[0].reshape(s_n, 1, BR_WIDTH), res[1]
    return res.reshape(s_n, 1, BR_WIDTH)


def _decode_cmp_kernel(q_ref, kc_ref, vc_ref, map_ref, qm_ref, om_ref, oc_ref, sel_ref, *, n_cmp, own):
    qrows = q_ref[...] * qm_ref[...]
    s = _dot_nt(qrows, kc_ref[...].astype(BF16))
    visible = lax.broadcasted_iota(jnp.int32, s.shape, 1) < n_cmp
    s = jnp.where(visible, s, NEG)
    e = jnp.where(visible, jnp.exp2(s - jnp.max(s, axis=-1, keepdims=True)), 0.0)
    p = e / jnp.maximum(jnp.sum(e, axis=-1, keepdims=True), 1e-30)
    p_hi = p.astype(BF16)
    p_lo = (p - p_hi.astype(F32)).astype(BF16)
    o = jnp.dot(p_hi, vc_ref[...].astype(BF16), preferred_element_type=F32)
    oc_ref[...] = jnp.sum(o * om_ref[...], axis=0, keepdims=True)
    rel = (jnp.dot(p_hi, map_ref[...], preferred_element_type=F32)
           + jnp.dot(p_lo, map_ref[...], preferred_element_type=F32))
    lane = lax.broadcasted_iota(jnp.int32, rel.shape, 1)
    picked = _select_bias(rel, lane, jnp.full((8, 1), own, jnp.int32), NSA_TOPK)
    sel_ref[...] = jnp.where(jnp.logical_and(picked == 0.0, lane < own), 1.0, 0.0)


def _decode_cmp(att3, cmp_kv, n_cmp, own):
    s_n, n_pad, _ = cmp_kv.shape
    assert own <= 128
    start = np.arange(n_pad)[:, None] * NSA_CMP_STRIDE
    blk = np.arange(128)[None, :] * NSA_SLC_BLOCK
    ov = np.minimum(start + NSA_CMP_LEN, blk + NSA_SLC_BLOCK) - np.maximum(start, blk)
    cmap = np.maximum(ov, 0).astype(np.float32) / NSA_CMP_LEN
    cmap[n_cmp:] = 0.0
    qm, om = _decode_row_masks(False)
    return pl.pallas_call(
        functools.partial(_decode_cmp_kernel, n_cmp=n_cmp, own=own),
        grid=(s_n,),
        in_specs=[pl.BlockSpec((None, 1, BR_WIDTH), lambda b: (b, 0, G_QDN)),
                  pl.BlockSpec((None, n_pad, BR_WIDTH), lambda b: (b, 0, 0)),
                  pl.BlockSpec((None, n_pad, BR_WIDTH), lambda b: (b, 0, 1)),
                  _const_spec((n_pad, 128)), _const_spec((8, BR_WIDTH)), _const_spec((8, BR_WIDTH))],
        out_specs=(pl.BlockSpec((None, 1, BR_WIDTH), lambda b: (b, 0, 0)),
                   pl.BlockSpec((None, 8, 128), lambda b: (b, 0, 0))),
        out_shape=(jax.ShapeDtypeStruct((s_n, 1, BR_WIDTH), F32), jax.ShapeDtypeStruct((s_n, 8, 128), F32)),
        compiler_params=_cparams(("parallel",)),
        name="decode_cmp",
    )(att3, cmp_kv, cmp_kv, jnp.asarray(cmap, BF16), qm, om)


def _decode_nsa_cmp_kernel(pt_ref, *refs, n_pg, n_cmp, own):
    del pt_ref
    q_ref = refs[0]
    pages = refs[1:1 + n_pg]
    (w1_ref, b1_ref, w2_ref, b2_ref, gain_ref, map_ref, oc_ref, sel_ref, xs_ref, cst_ref) = refs[1 + n_pg:]
    rows = pages[0].shape[-1]
    n_ch = n_pg * rows // NSA_CMP_STRIDE
    for j, pg in enumerate(pages):
        for kv in range(2):
            for p in range(2):
                t = jnp.transpose(pg[kv, p * PAIR_LANES:(p + 1) * PAIR_LANES, :])
                for c in range(rows // NSA_CMP_STRIDE):
                    r0 = (j * (rows // NSA_CMP_STRIDE) + c) * CHUNK_PITCH
                    xs_ref[kv, p, r0:r0 + NSA_CMP_STRIDE, :] = t[c * NSA_CMP_STRIDE:(c + 1) * NSA_CMP_STRIDE, :]
    lane = lax.broadcasted_iota(jnp.int32, (n_ch, PAIR_LANES), 1)
    for kv in range(2):
        for p in range(2):
            for i in range(NSA_CMP_STRIDE // 2):
                x0 = xs_ref[kv, p, pl.ds(2 * i, n_ch, stride=CHUNK_PITCH), :]
                x1 = xs_ref[kv, p, pl.ds(2 * i + 1, n_ch, stride=CHUNK_PITCH), :]
                even = jnp.where(lane < HEAD_DIM, x0, pltpu.roll(x1, HEAD_DIM, 1))
                odd = jnp.where(lane < HEAD_DIM, pltpu.roll(x0, HEAD_DIM, 1), x1)
                cols = slice(i * PAIR_LANES, (i + 1) * PAIR_LANES)
                cst_ref[kv, (2 * p) * n_ch:(2 * p + 1) * n_ch, cols] = even.astype(BF16)
                cst_ref[kv, (2 * p + 1) * n_ch:(2 * p + 2) * n_ch, cols] = odd.astype(BF16)
    toks = []
    for kv in range(2):
        uv = jnp.dot(cst_ref[kv], w1_ref[kv], preferred_element_type=F32)
        hid = uv.shape[1] // 2
        pre = uv[:, :hid] + pltpu.roll(uv[:, hid:], N_HEADS * n_ch - 1, 0) + b1_ref[kv]
        y = jnp.dot(jax.nn.gelu(pre, approximate=True).astype(BF16), w2_ref[kv],
                    preferred_element_type=F32) + b2_ref[kv]
        if kv == 0:
            y = y * lax.rsqrt(jnp.mean(y * y, axis=-1, keepdims=True) + EPS) * gain_ref[...]
        toks.append(y.astype(BF16))
    yk, yv = toks
    q = q_ref[...].astype(F32)
    q8 = jnp.concatenate([q[:, h * HEAD_DIM:(h + 1) * HEAD_DIM] for h in range(N_HEADS)]
                         + [jnp.zeros((8 - N_HEADS, HEAD_DIM), F32)], axis=0).astype(BF16)
    s_all = _dot_nt(q8, yk)
    row = lax.broadcasted_iota(jnp.int32, (8, n_ch), 0)
    s = jnp.zeros((8, n_ch), F32)
    for h in range(N_HEADS):
        s = s + jnp.where(row == h, s_all[:, h * n_ch:(h + 1) * n_ch], 0.0)
    visible = lax.broadcasted_iota(jnp.int32, (8, n_ch), 1) < n_cmp
    s = jnp.where(visible, s, NEG)
    e = jnp.where(visible, jnp.exp2(s - jnp.max(s, axis=-1, keepdims=True)), 0.0)
    p = e / jnp.maximum(jnp.sum(e, axis=-1, keepdims=True), 1e-30)
    p_hi = p.astype(BF16)
    p_lo = (p - p_hi.astype(F32)).astype(BF16)
    p_all = jnp.concatenate([jnp.where(row == h, p_hi, jnp.zeros_like(p_hi)) for h in range(N_HEADS)], axis=1)
    o4 = jnp.dot(p_all, yv, preferred_element_type=F32)
    oc_ref[...] = jnp.concatenate([o4[h:h + 1, :] for h in range(N_HEADS)], axis=1)
    rel = (jnp.dot(p_hi, map_ref[...], preferred_element_type=F32)
           + jnp.dot(p_lo, map_ref[...], preferred_element_type=F32))
    lane_b = lax.broadcasted_iota(jnp.int32, rel.shape, 1)
    picked = _select_bias(rel, lane_b, jnp.full((8, 1), own, jnp.int32), NSA_TOPK)
    sel_ref[...] = jnp.where(jnp.logical_and(picked == 0.0, lane_b < own), 1.0, 0.0)


def _decode_nsa_cmp(att3, cache_t, li, page_table, cmpw, n_cmp, own):
    s_n = att3.shape[0]
    rows = cache_t.shape[-1]
    n_pg = page_table.shape[1]
    n_ch = n_pg * rows // NSA_CMP_STRIDE
    assert own <= 128
    w1ab, bias1, w2, b2, gain = _compress_weights(*cmpw)
    start = np.arange(n_ch)[:, None] * NSA_CMP_STRIDE
    blk = np.arange(128)[None, :] * NSA_SLC_BLOCK
    ov = np.minimum(start + NSA_CMP_LEN, blk + NSA_SLC_BLOCK) - np.maximum(start, blk)
    cmap = np.maximum(ov, 0).astype(np.float32) / NSA_CMP_LEN
    cmap[n_cmp:] = 0.0
    pt = page_table.reshape(-1).astype(jnp.int32)
    cst = lambda a: pl.BlockSpec(a.shape, lambda b, pt: (0,) * a.ndim)
    consts = [w1ab, bias1, w2, b2, gain, jnp.asarray(cmap, BF16)]
    return pl.pallas_call(
        functools.partial(_decode_nsa_cmp_kernel, n_pg=n_pg, n_cmp=n_cmp, own=own),
        grid_spec=pltpu.PrefetchScalarGridSpec(
            num_scalar_prefetch=1, grid=(s_n,),
            in_specs=[pl.BlockSpec((None, 1, BR_WIDTH), lambda b, pt: (b, 0, G_QDN))]
                     + [pl.BlockSpec((None, None, 2, BR_WIDTH, rows),
                                     lambda b, pt, j=j: (li, pt[b * n_pg + j], 0, 0, 0)) for j in range(n_pg)]
                     + [cst(a) for a in consts],
            out_specs=(pl.BlockSpec((None, 1, BR_WIDTH), lambda b, pt: (b, 0, 0)),
                       pl.BlockSpec((None, 8, 128), lambda b, pt: (b, 0, 0))),
            scratch_shapes=[pltpu.VMEM((2, 2, n_ch * CHUNK_PITCH, PAIR_LANES), F32),
                            pltpu.VMEM((2, N_HEADS * n_ch, NSA_CMP_STRIDE * HEAD_DIM), BF16)]),
        out_shape=(jax.ShapeDtypeStruct((s_n, 1, BR_WIDTH), F32), jax.ShapeDtypeStruct((s_n, 8, 128), F32)),
        compiler_params=_cparams(("parallel",)),
        name="decode_nsa_cmp",
    )(pt, att3, *([cache_t] * n_pg), *consts)


def _sample_attention(att, kvs, small, caches_t, logf_t, win_t, cmp_t, page_table, li, diff, cmpw, past_len,
                      win_prev):
    s_n = att.shape[0]
    kva, kvb, kvc, kvds, kvdw = (a.reshape(s_n, 1, 2 * BR_WIDTH) for a in kvs)
    att3 = att.reshape(s_n, 1, -1)
    ca, cb, cc, cds = caches_t
    dec = functools.partial(_decode_attn, att3, li=li, page_table=page_table)
    o_a = dec(G_QA, kva, ca, fox=(logf_t, small.reshape(s_n, 1, 128)))
    o_b = dec(G_QB, kvb, cb, diff=diff)
    o_c = dec(G_QC, kvc, cc, mode="moba")
    n_cmp = (past_len + 1 - NSA_CMP_LEN) // NSA_CMP_STRIDE + 1
    o_dc, sel = _decode_nsa_cmp(att3, cmp_t, li, page_table, cmpw, n_cmp, past_len // NSA_SLC_BLOCK)
    o_ds = dec(G_QD, kvds, cds, mode="sel", sel=sel, blk=NSA_SLC_BLOCK)
    buf_len = win_t.shape[-1]
    o_dw, win_next = dec(G_QD, kvdw, win_t, dense=True, skip_first=max(0, buf_len - NSA_WINDOW + 1),
                         shift_out=(jnp.transpose(kvs[4]), win_prev, win_t.shape[0]))
    flat = lambda a, dt: a.reshape(s_n, BR_WIDTH).astype(dt)
    return (flat(o_a, BF16), flat(o_b, BF16), flat(o_c, BF16), flat(o_dc, F32), flat(o_ds, F32),
            flat(o_dw, F32)), win_next


def _layer_gains(fox_qnorm, fox_knorm, diff_qnorm, diff_knorm, moba_qnorm, moba_knorm, nsa_qnorm, nsa_knorm):
    return jnp.stack([_tile_gain(fox_qnorm, HEAD_DIM), _tile_gain(fox_knorm, HEAD_DIM),
                      _tile_gain(diff_qnorm, DIFF_DIM), _tile_gain(diff_knorm, DIFF_DIM),
                      _tile_gain(moba_qnorm, HEAD_DIM), _tile_gain(moba_knorm, HEAD_DIM),
                      _tile_gain(nsa_qnorm, HEAD_DIM), _tile_gain(nsa_knorm[1], HEAD_DIM),
                      _tile_gain(nsa_knorm[2], HEAD_DIM)])


def _unchunk_tokens(tok, b, m):
    return jnp.transpose(tok.reshape(2, b, N_HEADS, m, HEAD_DIM), (1, 3, 0, 2, 4)).reshape(b, m, 2 * BR_WIDTH)


def _prompt_attention(att, kmean, chunks, small, diff, cmpw, batch, seq, *, tq=512, tk=512):
    fl = functools.partial(_flash, batch=batch, seq=seq, tq=tq, tk=tk)
    ft8 = _cumsum_t(small, batch, seq)
    o_a = fl(att, att, att, q_col=G_QA, k_col=G_KA, v_col=G_VA, fox_t=ft8)
    o_b = fl(att, att, att, q_col=G_QB, k_col=G_KB, v_col=G_VB, diff=diff)
    qa_c, ka_c = _moba_router(att, kmean, batch, seq, tq=tq)
    o_c = fl(qa_c, ka_c, att, v_col=G_VC, aug=True)
    m = seq // NSA_CMP_STRIDE
    pos_emb, w1, b1, w2, b2, k_gain = cmpw
    tok = _compress(chunks, pos_emb, w1, b1, w2, b2, k_gain,
                    chunks_per_seq=m, seqs_per_tile=max(1, 512 // m))
    cmp_kv = _unchunk_tokens(tok, batch, m)
    o_dc, qa_d, ka_d = _nsa_cmp(att, cmp_kv, batch, seq, m - 1, tq=tq)
    o_ds = fl(qa_d, ka_d, att, v_col=G_VDS, aug=True, out_dtype=F32)
    o_dw = fl(att, att, att, q_col=G_QD, k_col=G_KDW, v_col=G_VDW, window=NSA_WINDOW, out_dtype=F32)
    return o_a, o_b, o_c, o_dc, o_ds, o_dw


def _trunk_tail(x2d, hb, branch, small, g1, sc2, sh2, g2, gn2, w_branch, w_gate, b_gate, w_out, ffn,
                *, rows_per_mod, tm=512):
    x2d, h2 = _merge(x2d, hb, *branch, small, g1, sc2, sh2, gn2, w_branch, w_gate, b_gate, w_out,
                     rows_per_mod=rows_per_mod, tm=tm)
    if ffn[0] == "dense":
        return _ffn(x2d, h2, g2, *ffn[1:], rows_per_mod=rows_per_mod, tm=tm)
    y = _moe_sparse(h2, *ffn[1:], tm=2 * tm, cap=288)
    return _residual(x2d, y, g2, rows_per_mod=rows_per_mod, tm=tm)


def kernel(x_prompt, x_sample, c_prompt, c_sample, cache_a_kv, cache_a_logf, cache_b_kv, cache_c_kv, cache_d_cmp_kv, cache_d_slc_kv, state_d_win_kv, page_table, ada_w, ada_b, norm_attn, norm_ffn, w_in, fox_fbias, fox_qnorm, fox_knorm, diff_qnorm, diff_knorm, diff_lambda, diff_subnorm, moba_qnorm, moba_knorm, nsa_qnorm, nsa_knorm, nsa_cmp_pos, nsa_cmp_w1, nsa_cmp_b1, nsa_cmp_w2, nsa_cmp_b2, w_branch, w_gate, b_gate, w_out, ffn_w_gate, ffn_w_up, ffn_w_down, moe_router, moe_router_b, moe_w_gate, moe_w_up, moe_w_down):
    bp, t, d = x_prompt.shape
    bs = x_sample.shape[0]
    depth = w_in.shape[0]
    n_pages = page_table.shape[1]
    past_len = n_pages * cache_a_kv.shape[2]
    pos_p = jnp.arange(t, dtype=jnp.int32)
    pos_s = jnp.full((1,), past_len, jnp.int32)
    xp = x_prompt.reshape(bp * t, d)
    xs = x_sample.reshape(bs, d)
    c_all = jnp.concatenate([c_prompt, c_sample, jnp.zeros((-(bp + bs) % 8, d), F32)], axis=0)
    page_t = lambda c: jnp.transpose(c, (0, 1, 3, 4, 5, 2)).reshape(c.shape[:2] + (2, BR_WIDTH, c.shape[2]))
    caches_t = tuple(page_t(c) for c in (cache_a_kv, cache_b_kv, cache_c_kv, cache_d_slc_kv))
    logf_t = jnp.swapaxes(cache_a_logf, 2, 3)
    win_t = page_t(state_d_win_kv)
    cmp_t = page_t(cache_d_cmp_kv)
    kv_p = None
    logf_p = []
    st_s = [[] for _ in range(6)]
    win_s = None
    for li in range(depth):
        lam_init = 0.8 - 0.6 * math.exp(-0.3 * li)
        lq = diff_lambda[li].astype(F32)
        lam = jnp.exp(jnp.sum(lq[0] * lq[1])) - jnp.exp(jnp.sum(lq[2] * lq[3])) + lam_init
        diff = (lam, 1.0 - lam_init, jnp.tile(diff_subnorm[li].astype(F32), 2))
        cmpw = (nsa_cmp_pos[li], nsa_cmp_w1[li], nsa_cmp_b1[li], nsa_cmp_w2[li], nsa_cmp_b2[li], nsa_knorm[li][0])
        mod = _rowmm(c_all, ada_w[li], ada_b[li], silu_in=True)
        modp = mod[:bp].reshape(bp, 6, d)
        wm, ws, bsm = _split_w_in(w_in[li], fox_fbias[li])
        gains = _layer_gains(fox_qnorm[li], fox_knorm[li], diff_qnorm[li], diff_knorm[li], moba_qnorm[li],
                             moba_knorm[li], nsa_qnorm[li], nsa_knorm[li])
        tm_p = 512
        hb, att, *kv_p, small, km, chunks = _inproj(
            xp, modp[:, 1], modp[:, 0], norm_attn[li], wm, ws, bsm, gains, pos_p, rows_per_mod=t, tm=tm_p,
            stacked=(li, depth, kv_p))
        nb = t // MOBA_BLOCK
        kmean = km[:, :tm_p // MOBA_BLOCK].reshape(bp, nb, BR_WIDTH)
        kmean = jnp.pad(kmean, ((0, 0), (0, -nb % 8), (0, 0)))
        branch = _prompt_attention(att, kmean, chunks.reshape(2, -1, chunks.shape[-1]), small, diff, cmpw, bp, t)
        j = li // 2
        if li % 2 == 0:
            ffn = ("dense", ffn_w_gate[j], ffn_w_up[j], ffn_w_down[j])
        else:
            ffn = ("moe", moe_router[j], moe_router_b[j], moe_w_gate[j], moe_w_up[j], moe_w_down[j])
        xp = _trunk_tail(xp, hb, branch, small, modp[:, 2], modp[:, 4], modp[:, 3], modp[:, 5], norm_ffn[li],
                         w_branch[li], w_gate[li], b_gate[li], w_out[li], ffn, rows_per_mod=t)
        logf_p.append(small[:, :N_HEADS].reshape(bp, t, N_HEADS))
        mods = mod[bp:bp + bs].reshape(bs, 6, d)
        hb, att, kva, kvb, kvc, kvdc, kvds, kvdw, small = _inproj(
            xs, mods[:, 1], mods[:, 0], norm_attn[li], wm, ws, bsm, gains, pos_s, rows_per_mod=1, tm=bs)
        diff_s = (lam, 1.0 - lam_init, jnp.tile(diff_subnorm[li].astype(F32), N_HEADS))
        branch, win_s = _sample_attention(att, (kva, kvb, kvc, kvds, kvdw), small, caches_t, logf_t, win_t, cmp_t,
                                          page_table, li, diff_s, cmpw, past_len, win_s)
        xs = _trunk_tail(xs, hb, branch, small, mods[:, 2], mods[:, 4], mods[:, 3], mods[:, 5], norm_ffn[li],
                         w_branch[li], w_gate[li], b_gate[li], w_out[li], ffn, rows_per_mod=1)
        kv6 = lambda a: a.reshape(bs, 1, 2, N_HEADS, HEAD_DIM)
        for lst, val in zip(st_s, (kv6(kva), small[:, :N_HEADS].reshape(bs, 1, N_HEADS), kv6(kvb), kv6(kvc),
                                   kv6(kvdc), kv6(kvds))):
            lst.append(val)
    untr = lambda a: jnp.transpose(a.reshape(a.shape[:3] + (N_HEADS, HEAD_DIM, a.shape[-1])), (0, 1, 5, 2, 3, 4))
    win = min(NSA_WINDOW, t)
    kva_p, kvb_p, kvc_p, kvdc_p, kvds_p, kvdw_p = kv_p
    out_p = (untr(kva_p), jnp.stack(logf_p, axis=0), untr(kvb_p), untr(kvc_p), untr(kvdc_p), untr(kvds_p),
             untr(kvdw_p[..., t - win:]))
    out_s = tuple(jnp.stack(lst, axis=0) for lst in st_s) + (untr(win_s),)
    return (xp.reshape(bp, t, d), xs.reshape(bs, 1, d)) + out_p + out_s
```

```python
import functools
import math

import jax
import jax.numpy as jnp
import numpy as np
from jax import lax
from jax.experimental import pallas as pl
from jax.experimental.pallas import tpu as pltpu

F32 = jnp.float32
BF16 = jnp.bfloat16

N_HEADS = 4
HEAD_DIM = 64
BR_WIDTH = N_HEADS * HEAD_DIM
PAIR_LANES = 2 * HEAD_DIM
DIFF_DIM = HEAD_DIM // 2
ROPE_THETA = 500000.0
MOBA_BLOCK = 256
MOBA_TOPK = 3
NSA_CMP_LEN = 32
NSA_CMP_STRIDE = 16
NSA_SLC_BLOCK = 64
NSA_TOPK = 4
NSA_WINDOW = 512
N_EXPERTS = 8
EPS = 1e-6
NEG = -1e30
LOG2E = math.log2(math.e)
DECODE_SAMPLES_PER_STEP = 2
CHUNK_PITCH = 24
V7X_VMEM_BYTES = 64 * 1024 * 1024
VMEM_LIMIT = V7X_VMEM_BYTES - 16 * 1024 * 1024

(G_QA, G_KA, G_VA, G_QB, G_KB, G_VB, G_QC, G_KC, G_VC,
 G_QDN, G_QD, G_KDS, G_VDS, G_KDW, G_VDW) = range(15)
N_ATT_GROUPS = 15


def _cparams(sem):
    return pltpu.CompilerParams(dimension_semantics=sem, vmem_limit_bytes=VMEM_LIMIT)


def _const_spec(shape):
    nd = len(shape)
    return pl.BlockSpec(shape, lambda *_: (0,) * nd)


def _rowmm_kernel(x_ref, w_ref, b_ref, o_ref, *, silu_in):
    x = x_ref[...]
    if silu_in:
        x = x * jax.nn.sigmoid(x)
    o_ref[...] = jnp.dot(x.astype(BF16), w_ref[...].astype(BF16),
                         preferred_element_type=F32) + b_ref[...]


def _rowmm(x, w, b, *, silu_in=False, tn=1024):
    m, k = x.shape
    n = w.shape[1]
    tn = math.gcd(tn, n)
    return pl.pallas_call(
        functools.partial(_rowmm_kernel, silu_in=silu_in),
        grid=(n // tn,),
        in_specs=[pl.BlockSpec((m, k), lambda j: (0, 0)),
                  pl.BlockSpec((k, tn), lambda j: (0, j)),
                  pl.BlockSpec((1, tn), lambda j: (0, j))],
        out_specs=pl.BlockSpec((m, tn), lambda j: (0, j)),
        out_shape=jax.ShapeDtypeStruct((m, n), F32),
        compiler_params=_cparams(("parallel",)),
        name="rowmm",
    )(x, w, b.reshape(1, n))


def _rope_tables(pos, group):
    r = group // 4
    half = r // 2
    inv = ROPE_THETA ** (-(np.arange(half, dtype=np.float32) / half))
    lane = np.arange(BR_WIDTH)
    j = lane % group
    ang = pos.astype(F32)[:, None] * jnp.asarray(inv[j % half], F32)[None, :]
    cos = jnp.where(j[None, :] < r, jnp.cos(ang), 1.0)
    sin = jnp.sin(ang)
    sin_a = jnp.where(j[None, :] < half, -sin, 0.0)
    sin_b = jnp.where((j[None, :] >= half) & (j[None, :] < r), sin, 0.0)
    return cos.astype(F32), sin_a.astype(F32), sin_b.astype(F32)


def _group_mean_matrix(width, group):
    i = np.arange(width)
    return jnp.asarray((i[:, None] // group == i[None, :] // group).astype(np.float32) / group, BF16)


def _group_rms(a, bd, gain):
    ms = jnp.dot((a * a).astype(BF16), bd, preferred_element_type=F32)
    return a * lax.rsqrt(ms + EPS) * gain


def _rope(a, cos, sin_a, sin_b, half):
    w = a.shape[-1]
    return a * cos + pltpu.roll(a, w - half, 1) * sin_a + pltpu.roll(a, half, 1) * sin_b


def _log_sigmoid(x):
    return jnp.minimum(x, 0.0) - jnp.log1p(jnp.exp(-jnp.abs(x)))


def _inproj_kernel(x_ref, sc_ref, sh_ref, gn_ref, w_ref, ws_ref, bs_ref, gains_ref,
                   bd64_ref, bd32_ref, c64_ref, sa64_ref, sb64_ref, c32_ref, sa32_ref, sb32_ref,
                   *rest, transposed, n_alias):
    rest = rest[n_alias:]
    h_ref, att_ref, kva_ref, kvb_ref, kvc_ref, kvdc_ref, kvds_ref, kvdw_ref, small_ref = rest[:9]

    def put_kv(ref, k, v):
        if transposed:
            stage_ref = rest[-1]
            stage_ref[0] = k
            stage_ref[1] = v
            ref[0] = jnp.transpose(stage_ref[0])
            ref[1] = jnp.transpose(stage_ref[1])
        else:
            ref[:, :BR_WIDTH] = k
            ref[:, BR_WIDTH:] = v

    x = x_ref[...]
    ms = jnp.mean(x * x, axis=-1, keepdims=True)
    h = (x * lax.rsqrt(ms + EPS) * gn_ref[...]) * (1.0 + sc_ref[...]) + sh_ref[...]
    hb = h.astype(BF16)
    h_ref[...] = hb
    bd64 = bd64_ref[...]
    bd32 = bd32_ref[...]

    def proj(g):
        return jnp.dot(hb, w_ref[:, g * BR_WIDTH:(g + 1) * BR_WIDTH], preferred_element_type=F32)

    def gain(i):
        return gains_ref[i:i + 1, :]

    def rope64(a):
        return _rope(a, c64_ref[...], sa64_ref[...], sb64_ref[...], 8)

    def rope32(a):
        return _rope(a, c32_ref[...], sa32_ref[...], sb32_ref[...], 4)

    def att(g, a):
        att_ref[:, g * BR_WIDTH:(g + 1) * BR_WIDTH] = a.astype(BF16)

    sm_scale = HEAD_DIM ** -0.5 * LOG2E
    att(G_QA, _group_rms(proj(0), bd64, gain(0)) * sm_scale)
    k = _group_rms(proj(1), bd64, gain(1))
    v = proj(2)
    put_kv(kva_ref, k, v)
    att(G_KA, k)
    att(G_VA, v)
    att(G_QB, rope32(_group_rms(proj(3), bd32, gain(2))) * (DIFF_DIM ** -0.5 * LOG2E))
    k = rope32(_group_rms(proj(4), bd32, gain(3)))
    v = proj(5)
    put_kv(kvb_ref, k, v)
    att(G_KB, k)
    att(G_VB, v)
    att(G_QC, rope64(_group_rms(proj(6), bd64, gain(4))) * sm_scale)
    k = rope64(_group_rms(proj(7), bd64, gain(5)))
    v = proj(8)
    put_kv(kvc_ref, k, v)
    if transposed:
        km_ref = rest[9]
        nb = k.shape[0] // MOBA_BLOCK
        km_ref[...] = jnp.concatenate(
            [jnp.mean(k[b * MOBA_BLOCK:(b + 1) * MOBA_BLOCK], axis=0, keepdims=True) for b in range(nb)]
            + [jnp.zeros((8 - nb, BR_WIDTH), F32)], axis=0)
    att(G_KC, k)
    att(G_VC, v)
    qn = _group_rms(proj(9), bd64, gain(6))
    att(G_QDN, qn * sm_scale)
    att(G_QD, rope64(qn) * sm_scale)
    put_kv(kvdc_ref, proj(10), proj(11))
    if transposed:
        chunk_ref, pair_ref, stage_ref = rest[10], rest[-2], rest[-1]
        n_ch = x.shape[0] // NSA_CMP_STRIDE
        lane_c = lax.broadcasted_iota(jnp.int32, (n_ch, PAIR_LANES), 1)
        for kv in range(2):
            for p in range(2):
                pair_ref[kv, p] = stage_ref[kv, :, p * PAIR_LANES:(p + 1) * PAIR_LANES]
                for i in range(NSA_CMP_STRIDE // 2):
                    x0 = pair_ref[kv, p, pl.ds(2 * i, n_ch, stride=NSA_CMP_STRIDE), :]
                    x1 = pair_ref[kv, p, pl.ds(2 * i + 1, n_ch, stride=NSA_CMP_STRIDE), :]
                    even = jnp.where(lane_c < HEAD_DIM, x0, pltpu.roll(x1, HEAD_DIM, 1))
                    odd = jnp.where(lane_c < HEAD_DIM, pltpu.roll(x0, HEAD_DIM, 1), x1)
                    cols = slice(i * PAIR_LANES, (i + 1) * PAIR_LANES)
                    chunk_ref[kv, 2 * p, :, cols] = even.astype(BF16)
                    chunk_ref[kv, 2 * p + 1, :, cols] = odd.astype(BF16)
    k = rope64(_group_rms(proj(12), bd64, gain(7)))
    v = proj(13)
    put_kv(kvds_ref, k, v)
    att(G_KDS, k)
    att(G_VDS, v)
    k = rope64(_group_rms(proj(14), bd64, gain(8)))
    v = proj(15)
    put_kv(kvdw_ref, k, v)
    att(G_KDW, k)
    att(G_VDW, v)
    z = jnp.dot(hb, ws_ref[...], preferred_element_type=F32) + bs_ref[...]
    lane = lax.broadcasted_iota(jnp.int32, z.shape, 1)
    small_ref[...] = jnp.where(lane < N_HEADS, _log_sigmoid(z), jax.nn.sigmoid(z))


def _split_w_in(w_in, fox_fbias):
    d = w_in.shape[0]
    c0 = 3 * BR_WIDTH
    main = jnp.concatenate([w_in[:, :c0], w_in[:, c0 + N_HEADS:c0 + N_HEADS + 13 * BR_WIDTH]], axis=1)
    small = jnp.concatenate([w_in[:, c0:c0 + N_HEADS], w_in[:, c0 + N_HEADS + 13 * BR_WIDTH:],
                             jnp.zeros((d, 128 - 4 * N_HEADS), w_in.dtype)], axis=1)
    bias = jnp.concatenate([fox_fbias.astype(F32), jnp.zeros((128 - N_HEADS,), F32)]).reshape(1, 128)
    return main.astype(BF16), small.astype(BF16), bias


def _inproj(x2d, sc, sh, gn, w_main, w_small, b_small, gains, pos, *, rows_per_mod, tm, stacked=None):
    n, d = x2d.shape
    tm = min(tm, n)
    assert n % tm == 0
    per_tok = rows_per_mod == 1
    if not per_tok:
        assert rows_per_mod % tm == 0
    tabs = _rope_tables(pos, HEAD_DIM) + _rope_tables(pos, DIFF_DIM)
    single_pos = pos.shape[0] == 1
    tiles_per_seq = 1 if single_pos else pos.shape[0] // tm

    if per_tok:
        mod_spec = pl.BlockSpec((tm, d), lambda i: (i, 0))
    else:
        tpm = rows_per_mod // tm
        mod_spec = pl.BlockSpec((None, 1, d), lambda i: (i // tpm, 0, 0))
        sc, sh = sc.reshape(-1, 1, d), sh.reshape(-1, 1, d)
    if single_pos:
        tab_spec = pl.BlockSpec((1, BR_WIDTH), lambda i: (0, 0))
    else:
        tab_spec = pl.BlockSpec((tm, BR_WIDTH), lambda i: (i % tiles_per_seq, 0))

    row = lambda w: pl.BlockSpec((tm, w), lambda i: (i, 0))
    in_specs = [row(d), mod_spec, mod_spec, _const_spec((1, d)),
                _const_spec(w_main.shape), _const_spec(w_small.shape), _const_spec((1, 128)),
                _const_spec(gains.shape),
                _const_spec((BR_WIDTH, BR_WIDTH)), _const_spec((BR_WIDTH, BR_WIDTH))] + [tab_spec] * 6
    args = [x2d, sc, sh, gn.reshape(1, d), w_main, w_small, b_small, gains,
            _group_mean_matrix(BR_WIDTH, HEAD_DIM), _group_mean_matrix(BR_WIDTH, DIFF_DIM), *tabs]
    aliases = {}
    if stacked is None:
        kv_shapes = (jax.ShapeDtypeStruct((n, 2 * BR_WIDTH), F32),) * 6
        kv_specs = (row(2 * BR_WIDTH),) * 6
        extra_shapes, extra_specs = (), ()
    else:
        li, depth, prev = stacked
        seq = pos.shape[0]
        batch = n // seq
        assert tm % MOBA_BLOCK == 0
        kv_shapes = (jax.ShapeDtypeStruct((depth, batch, 2, BR_WIDTH, seq), F32),) * 6
        kv_specs = (pl.BlockSpec((None, None, 2, BR_WIDTH, tm),
                                 lambda i: (li, i // tiles_per_seq, 0, 0, i % tiles_per_seq)),) * 6
        n_ch = tm // NSA_CMP_STRIDE
        extra_shapes = (jax.ShapeDtypeStruct((n // tm, 8, BR_WIDTH), F32),
                        jax.ShapeDtypeStruct((2, batch, N_HEADS, seq // NSA_CMP_STRIDE, NSA_CMP_STRIDE * HEAD_DIM), BF16))
        extra_specs = (pl.BlockSpec((None, 8, BR_WIDTH), lambda i: (i, 0, 0)),
                       pl.BlockSpec((2, None, N_HEADS, n_ch, NSA_CMP_STRIDE * HEAD_DIM),
                                    lambda i: (0, i // tiles_per_seq, 0, i % tiles_per_seq, 0)))
        if prev is not None:
            aliases = {len(args) + k: 2 + k for k in range(6)}
            in_specs += [pl.BlockSpec(memory_space=pl.ANY)] * 6
            args += list(prev)
    out_shape = ((jax.ShapeDtypeStruct((n, d), BF16), jax.ShapeDtypeStruct((n, N_ATT_GROUPS * BR_WIDTH), BF16))
                 + kv_shapes + (jax.ShapeDtypeStruct((n, 128), F32),) + extra_shapes)
    out_specs = (row(d), row(N_ATT_GROUPS * BR_WIDTH)) + kv_specs + (row(128),) + extra_specs
    return pl.pallas_call(
        functools.partial(_inproj_kernel, transposed=stacked is not None, n_alias=len(aliases)),
        grid=(n // tm,),
        in_specs=in_specs,
        out_specs=out_specs,
        out_shape=out_shape,
        scratch_shapes=([pltpu.VMEM((2, 2, tm, PAIR_LANES), F32), pltpu.VMEM((2, tm, BR_WIDTH), F32)]
                        if stacked is not None else []),
        input_output_aliases=aliases,
        compiler_params=_cparams(("parallel",)),
        name="inproj",
    )(*args)


def _tile_gain(g, group):
    return jnp.tile(g.astype(F32), BR_WIDTH // group)


def _cumsum_kernel(x_ref, tri_ref, ft_ref, carry_ref):
    @pl.when(pl.program_id(1) == 0)
    def _():
        carry_ref[...] = jnp.zeros_like(carry_ref)

    c = jnp.dot(tri_ref[...], x_ref[...], precision=lax.Precision.HIGHEST,
                preferred_element_type=F32) + carry_ref[...]
    carry_ref[...] = c[-1:, :]
    ft_ref[...] = jnp.transpose(c)[:8, :]


def _cumsum_t(small, batch, seq, *, tc=256):
    tc = min(tc, seq)
    nt = seq // tc
    tri = jnp.asarray(np.tril(np.ones((tc, tc), np.float32)))
    return pl.pallas_call(
        _cumsum_kernel,
        grid=(batch, nt),
        in_specs=[pl.BlockSpec((tc, 128), lambda b, t: (b * nt + t, 0)), _const_spec((tc, tc))],
        out_specs=pl.BlockSpec((None, 8, tc), lambda b, t: (b, 0, t)),
        out_shape=jax.ShapeDtypeStruct((batch, 8, seq), F32),
        scratch_shapes=[pltpu.VMEM((1, 128), F32)],
        compiler_params=_cparams(("parallel", "arbitrary")),
        name="cumsum_t",
    )(small, tri)


def _flash_steps(seq, tq, tk, window):
    assert tk % tq == 0 and seq % tk == 0
    qi_l, kj_l, fl_l = [], [], []
    for qi in range(seq // tq):
        t0, t1 = qi * tq, (qi + 1) * tq - 1
        js = []
        for j in range(t0 // tk, -1, -1):
            s0, s1 = j * tk, (j + 1) * tk - 1
            if window is not None and s1 <= t0 - window:
                break
            partial = s1 > t0 or (window is not None and s0 <= t1 - window)
            js.append((j, partial))
        for n, (j, partial) in enumerate(js):
            qi_l.append(qi)
            kj_l.append(j)
            fl_l.append((1 if n == 0 else 0) | (2 if n == len(js) - 1 else 0) | (4 if partial else 0))
    return (jnp.asarray(qi_l, jnp.int32), jnp.asarray(kj_l, jnp.int32), jnp.asarray(fl_l, jnp.int32))


def _dot_nt(a, b):
    return lax.dot_general(a, b, (((1,), (1,)), ((), ())), preferred_element_type=F32)


def _flash_kernel(qi_ref, kj_ref, fl_ref, *refs, aug, groups, tq, tk, fox, window, combine, lam_scale):
    refs = list(refs)
    q_ref, k_ref, v_ref = refs[:3]
    refs = refs[3:]
    mask_ref = None if aug else refs.pop(0)
    ft_ref = refs.pop(0) if fox else None
    if combine == "diff":
        lam_ref, bd_ref, gsub_ref = refs[:3]
        refs = refs[3:]
    o_ref, m_ref, acc_ref, qs_ref = refs

    step = pl.program_id(1)
    fl = fl_ref[step]
    qi = qi_ref[step]
    kj = kj_ref[step]
    pair = lambda ref, p: ref[:, p * PAIR_LANES:(p + 1) * PAIR_LANES]

    @pl.when((fl & 1) != 0)
    def _():
        m_ref[...] = jnp.full(m_ref.shape, NEG, F32)
        acc_ref[...] = jnp.zeros(acc_ref.shape, F32)
        if not aug:
            for p in range(2):
                q = pair(q_ref, p)
                for r in range(groups):
                    qs_ref[p * groups + r] = q * mask_ref[r:r + 1, :]

    def body(apply_mask):
        if apply_mask:
            row = qi * tq + lax.broadcasted_iota(jnp.int32, (tq, tk), 0)
            col = kj * tk + lax.broadcasted_iota(jnp.int32, (tq, tk), 1)
            valid = col <= row
            if window is not None:
                valid = jnp.logical_and(valid, col > row - window)
        if fox:
            ft = ft_ref[...] * LOG2E
        for p in range(2):
            v1 = jnp.concatenate([pair(v_ref, p), jnp.ones((tk, PAIR_LANES), BF16)], axis=1)
            for r in range(groups):
                g = p * groups + r
                if aug:
                    s = _dot_nt(q_ref[g], k_ref[g])
                else:
                    s = _dot_nt(qs_ref[g], pair(k_ref, p))
                if fox:
                    s = s - ft[g:g + 1, :]
                if apply_mask:
                    s = jnp.where(valid, s, NEG)
                m_prev = m_ref[g]
                m_new = jnp.maximum(m_prev, jnp.max(s, axis=-1, keepdims=True))
                alpha = jnp.exp2(m_prev - m_new)
                pexp = jnp.exp2(s - m_new)
                acc_ref[g] = alpha * acc_ref[g] + jnp.dot(pexp.astype(BF16), v1, preferred_element_type=F32)
                m_ref[g] = m_new

    masked = (fl & 4) != 0
    pl.when(masked)(lambda: body(True))
    pl.when(jnp.logical_not(masked))(lambda: body(False))

    @pl.when((fl & 2) != 0)
    def _():
        lane = lax.broadcasted_iota(jnp.int32, (tq, PAIR_LANES), 1)
        for p in range(2):
            outs = []
            for r in range(groups):
                a = acc_ref[p * groups + r]
                outs.append(a[:, :PAIR_LANES] / a[:, PAIR_LANES:])
            if combine == "diff":
                lam = lam_ref[0]
                o = jnp.where(lane < HEAD_DIM, outs[0] - lam * outs[1], outs[2] - lam * outs[3])
                o = _group_rms(o, bd_ref[...], gsub_ref[...]) * lam_scale
            else:
                o = jnp.where(lane < HEAD_DIM, outs[0], outs[1])
            o_ref[:, p * PAIR_LANES:(p + 1) * PAIR_LANES] = o.astype(o_ref.dtype)


def _lane_masks(bounds):
    lane = np.arange(PAIR_LANES)
    return jnp.asarray(np.stack([(lane >= a) & (lane < b) for a, b in bounds]).astype(np.float32), BF16)


PAIR_MASKS = ((0, 64), (64, 128))
DIFF_MASKS = ((0, 32), (32, 64), (64, 96), (96, 128))


def _flash(q_src, k_src, v_src, *, batch, seq, tq, tk, q_col=None, k_col=None, v_col, aug=False,
           fox_t=None, window=None, diff=None, out_dtype=BF16):
    tq, tk = min(tq, seq), min(tk, seq)
    nq, nk = seq // tq, seq // tk
    qi_t, kj_t, fl_t = _flash_steps(seq, tq, tk, window)
    n_steps = int(qi_t.shape[0]) // 1
    groups = 4 if diff is not None else 2
    combine = "diff" if diff is not None else "pair"

    def tok(use_q, col):
        tile, n_t = (tq, nq) if use_q else (tk, nk)
        return pl.BlockSpec((tile, BR_WIDTH),
                            lambda b, s, qi, kj, fl: (b * n_t + (qi if use_q else kj)[s], col))

    in_specs, args = [], []
    if aug:
        in_specs += [pl.BlockSpec((None, N_HEADS, tq, PAIR_LANES), lambda b, s, qi, kj, fl: (b, 0, qi[s], 0)),
                     pl.BlockSpec((None, N_HEADS, tk, PAIR_LANES), lambda b, s, qi, kj, fl: (b, 0, kj[s], 0))]
    else:
        in_specs += [tok(True, q_col), tok(False, k_col)]
    in_specs.append(tok(False, v_col))
    args += [q_src, k_src, v_src]
    if not aug:
        in_specs.append(_const_spec((groups, PAIR_LANES)))
        args.append(_lane_masks(DIFF_MASKS if diff is not None else PAIR_MASKS))
    if fox_t is not None:
        in_specs.append(pl.BlockSpec((None, 8, tk), lambda b, s, qi, kj, fl: (b, 0, kj[s])))
        args.append(fox_t)
    lam_scale = 1.0
    if diff is not None:
        lam, lam_scale, gsub = diff
        in_specs += [pl.BlockSpec(memory_space=pltpu.SMEM), _const_spec((PAIR_LANES, PAIR_LANES)),
                     _const_spec((1, PAIR_LANES))]
        args += [lam.reshape(1).astype(F32), _group_mean_matrix(PAIR_LANES, HEAD_DIM), gsub.reshape(1, PAIR_LANES)]

    kern = functools.partial(_flash_kernel, aug=aug, groups=groups, tq=tq, tk=tk, fox=fox_t is not None,
                             window=window, combine=combine, lam_scale=lam_scale)
    return pl.pallas_call(
        kern,
        grid_spec=pltpu.PrefetchScalarGridSpec(
            num_scalar_prefetch=3,
            grid=(batch, n_steps),
            in_specs=in_specs,
            out_specs=pl.BlockSpec((tq, BR_WIDTH), lambda b, s, qi, kj, fl: (b * nq + qi[s], 0)),
            scratch_shapes=[pltpu.VMEM((2 * groups, tq, 1), F32),
                            pltpu.VMEM((2 * groups, tq, 2 * PAIR_LANES), F32),
                            pltpu.VMEM((2 * groups, tq, PAIR_LANES), BF16)]),
        out_shape=jax.ShapeDtypeStruct((batch * seq, BR_WIDTH), out_dtype),
        compiler_params=_cparams(("parallel", "arbitrary")),
        name="flash_" + ("aug" if aug else combine) + ("_fox" if fox_t is not None else "")
             + ("_win" if window is not None else ""),
    )(qi_t, kj_t, fl_t, *args)


def _select_bias(rel, j, own, topk):
    big = jnp.int32(1 << 20)
    val = jnp.where(j < own, jnp.where(j >= 0, rel, NEG), NEG)
    bias = jnp.where(j == own, 0.0, NEG)
    for _ in range(topk):
        mx = jnp.max(val, axis=-1, keepdims=True)
        idx = jnp.min(jnp.where(val == mx, j, big), axis=-1, keepdims=True)
        pick = j == jnp.where(mx > 0.5 * NEG, idx, big)
        bias = jnp.where(pick, 0.0, bias)
        val = jnp.where(pick, NEG, val)
    return bias


def _blockmean_kernel(x_ref, o_ref):
    j = pl.program_id(1)
    o_ref[pl.ds(j, 1), :] = jnp.mean(x_ref[...], axis=0, keepdims=True)


def _blockmean(kv, batch, seq, blk, nb_pad):
    nb = seq // blk
    out = pl.pallas_call(
        _blockmean_kernel,
        grid=(batch, nb),
        in_specs=[pl.BlockSpec((blk, BR_WIDTH), lambda b, j: (b * nb + j, 0))],
        out_specs=pl.BlockSpec((None, nb, BR_WIDTH), lambda b, j: (b, 0, 0)),
        out_shape=jax.ShapeDtypeStruct((batch, nb, BR_WIDTH), F32),
        compiler_params=_cparams(("parallel", "arbitrary")),
        name="blockmean",
    )(kv)
    return jnp.pad(out, ((0, 0), (0, nb_pad - nb), (0, 0)))


def _augment(x, bias_or_onehot, r):
    lane = lax.broadcasted_iota(jnp.int32, x.shape, 1)
    mine = (lane < HEAD_DIM) if r == 0 else (lane >= HEAD_DIM)
    return jnp.where(mine, x.astype(F32), bias_or_onehot).astype(BF16)


def _payload_block_id(shape, r):
    lane = lax.broadcasted_iota(jnp.int32, shape, 1)
    return (lane - HEAD_DIM) if r == 0 else jnp.where(lane < HEAD_DIM, lane, -1)


def _moba_router_kernel(q_ref, k_ref, km_ref, mask_ref, qa_ref, ka_ref, *, tq, blk, nb_pad):
    qi = pl.program_id(2)
    q = q_ref[...]
    k = k_ref[...]
    row = qi * tq + lax.broadcasted_iota(jnp.int32, (tq, 1), 0)
    own = row // blk
    for r in range(2):
        km = (km_ref[...] * mask_ref[r:r + 1, :].astype(F32)).astype(BF16)
        lo = HEAD_DIM if r == 0 else 0
        pieces = [km, jnp.zeros((PAIR_LANES - nb_pad, PAIR_LANES), BF16)]
        if lo:
            pieces = [jnp.zeros((lo, PAIR_LANES), BF16), km, jnp.zeros((PAIR_LANES - lo - nb_pad, PAIR_LANES), BF16)]
        rel = _dot_nt(q * mask_ref[r:r + 1, :], jnp.concatenate(pieces, axis=0))
        j = _payload_block_id((tq, PAIR_LANES), r)
        bias = _select_bias(rel, j, own, MOBA_TOPK)
        qa_ref[r] = _augment(q, bias, r)
        ka_ref[r] = _augment(k, (j == own).astype(F32), r)


def _moba_router(att, kmean, batch, seq, *, tq):
    tq = min(tq, seq)
    nq = seq // tq
    nb_pad = kmean.shape[1]
    out = jax.ShapeDtypeStruct((batch, N_HEADS, seq, PAIR_LANES), BF16)
    aug_spec = pl.BlockSpec((None, 2, tq, PAIR_LANES), lambda b, p, i: (b, p, i, 0))
    return pl.pallas_call(
        functools.partial(_moba_router_kernel, tq=tq, blk=MOBA_BLOCK, nb_pad=nb_pad),
        grid=(batch, 2, nq),
        in_specs=[pl.BlockSpec((tq, PAIR_LANES), lambda b, p, i: (b * nq + i, G_QC * 2 + p)),
                  pl.BlockSpec((tq, PAIR_LANES), lambda b, p, i: (b * nq + i, G_KC * 2 + p)),
                  pl.BlockSpec((None, nb_pad, PAIR_LANES), lambda b, p, i: (b, 0, p)),
                  _const_spec((2, PAIR_LANES))],
        out_specs=(aug_spec, aug_spec),
        out_shape=(out, out),
        compiler_params=_cparams(("parallel", "parallel", "parallel")),
        name="moba_router",
    )(att, att, kmean, _lane_masks(PAIR_MASKS))


def _compress_kernel(c_ref, w1_ref, b1_ref, w2_ref, b2_ref, gain_ref, o_ref, *, rows):
    kv = pl.program_id(0)
    uv = jnp.dot(c_ref[...].astype(BF16), w1_ref[...], preferred_element_type=F32)
    hid = uv.shape[1] // 2
    pre = uv[:, :hid] + pltpu.roll(uv[:, hid:], rows - 1, 0) + b1_ref[...]
    y = jnp.dot(jax.nn.gelu(pre, approximate=True).astype(BF16), w2_ref[...],
                preferred_element_type=F32) + b2_ref[...]
    ms = jnp.mean(y * y, axis=-1, keepdims=True)
    yn = y * lax.rsqrt(ms + EPS) * gain_ref[...]
    o_ref[...] = jnp.where(kv == 0, yn, y)


def _compress(chunks, pos_emb, w1, b1, w2, b2, k_gain, *, chunks_per_seq, seqs_per_tile):
    _, r, cw = chunks.shape
    hid = w1.shape[-1]
    rows = chunks_per_seq * seqs_per_tile
    assert r % rows == 0
    w1ab, bias1, w2, b2, gain = _compress_weights(pos_emb, w1, b1, w2, b2, k_gain)
    return pl.pallas_call(
        functools.partial(_compress_kernel, rows=rows),
        grid=(2, r // rows),
        in_specs=[pl.BlockSpec((None, rows, cw), lambda kv, i: (kv, i, 0)),
                  pl.BlockSpec((None, cw, 2 * hid), lambda kv, i: (kv, 0, 0)),
                  pl.BlockSpec((None, 1, hid), lambda kv, i: (kv, 0, 0)),
                  pl.BlockSpec((None, hid, HEAD_DIM), lambda kv, i: (kv, 0, 0)),
                  pl.BlockSpec((None, 1, HEAD_DIM), lambda kv, i: (kv, 0, 0)),
                  _const_spec((1, HEAD_DIM))],
        out_specs=pl.BlockSpec((None, rows, HEAD_DIM), lambda kv, i: (kv, i, 0)),
        out_shape=jax.ShapeDtypeStruct((2, r, HEAD_DIM), F32),
        compiler_params=_cparams(("parallel", "parallel")),
        name="nsa_compress",
    )(chunks, w1ab, bias1, w2, b2, gain)


def _compress_weights(pos_emb, w1, b1, w2, b2, k_gain):
    cw = w1.shape[1] // 2
    hid = w1.shape[-1]
    w1ab = jnp.concatenate([w1[:, :cw], w1[:, cw:]], axis=-1).astype(BF16)
    bias1 = (jnp.einsum('kf,kfh->kh', pos_emb.reshape(2, -1).astype(F32), w1.astype(F32),
                        precision=lax.Precision.HIGHEST) + b1.astype(F32)).reshape(2, 1, hid)
    return (w1ab, bias1, w2.astype(BF16), b2.astype(F32).reshape(2, 1, HEAD_DIM),
            k_gain.astype(F32).reshape(1, HEAD_DIM))


def _cmp_to_slc_map(n_cmp_pad, n_cmp):
    start = np.arange(n_cmp_pad)[:, None] * NSA_CMP_STRIDE
    blk = np.arange(HEAD_DIM)[None, :] * NSA_SLC_BLOCK
    ov = np.minimum(start + NSA_CMP_LEN, blk + NSA_SLC_BLOCK) - np.maximum(start, blk)
    m = np.maximum(ov, 0).astype(np.float32) / NSA_CMP_LEN
    m[n_cmp:] = 0.0
    z = np.zeros_like(m)
    return jnp.asarray(np.stack([np.concatenate([z, m], 1), np.concatenate([m, z], 1)]), BF16)


def _nsa_cmp_kernel(qn_ref, q_ref, k_ref, kc_ref, vc_ref, map_ref, mask_ref,
                    oc_ref, qa_ref, ka_ref, *, tq, n_pad):
    qi = pl.program_id(2)
    qn = qn_ref[...]
    q = q_ref[...]
    k = k_ref[...]
    kc = kc_ref[...].astype(BF16)
    vc = vc_ref[...].astype(BF16)
    row = qi * tq + lax.broadcasted_iota(jnp.int32, (tq, 1), 0)
    own = row // NSA_SLC_BLOCK
    cmp_end = lax.broadcasted_iota(jnp.int32, (tq, n_pad), 1) * NSA_CMP_STRIDE + (NSA_CMP_LEN - 1)
    visible = cmp_end <= row
    outs = []
    for r in range(2):
        s = jnp.where(visible, _dot_nt(qn * mask_ref[r:r + 1, :], kc), NEG)
        e = jnp.where(visible, jnp.exp2(s - jnp.max(s, axis=-1, keepdims=True)), 0.0)
        p = e / jnp.maximum(jnp.sum(e, axis=-1, keepdims=True), 1e-30)
        p_hi = p.astype(BF16)
        p_lo = (p - p_hi.astype(F32)).astype(BF16)
        outs.append(jnp.dot(p_hi, vc, preferred_element_type=F32))
        rel = (jnp.dot(p_hi, map_ref[r], preferred_element_type=F32)
               + jnp.dot(p_lo, map_ref[r], preferred_element_type=F32))
        j = _payload_block_id((tq, PAIR_LANES), r)
        bias = _select_bias(rel, j, own, NSA_TOPK)
        qa_ref[r] = _augment(q, bias, r)
        ka_ref[r] = _augment(k, (j == own).astype(F32), r)
    lane = lax.broadcasted_iota(jnp.int32, (tq, PAIR_LANES), 1)
    oc_ref[...] = jnp.where(lane < HEAD_DIM, outs[0], outs[1])


def _nsa_cmp(att, cmp_kv, batch, seq, n_cmp, *, tq):
    tq = min(tq, seq)
    nq = seq // tq
    n_pad = cmp_kv.shape[1]
    aug = jax.ShapeDtypeStruct((batch, N_HEADS, seq, PAIR_LANES), BF16)
    aug_spec = pl.BlockSpec((None, 2, tq, PAIR_LANES), lambda b, p, i: (b, p, i, 0))
    tok = lambda g: pl.BlockSpec((tq, PAIR_LANES), lambda b, p, i: (b * nq + i, g * 2 + p))
    return pl.pallas_call(
        functools.partial(_nsa_cmp_kernel, tq=tq, n_pad=n_pad),
        grid=(batch, 2, nq),
        in_specs=[tok(G_QDN), tok(G_QD), tok(G_KDS),
                  pl.BlockSpec((None, n_pad, PAIR_LANES), lambda b, p, i: (b, 0, p)),
                  pl.BlockSpec((None, n_pad, PAIR_LANES), lambda b, p, i: (b, 0, 2 + p)),
                  _const_spec((2, n_pad, PAIR_LANES)), _const_spec((2, PAIR_LANES))],
        out_specs=(pl.BlockSpec((tq, PAIR_LANES), lambda b, p, i: (b * nq + i, p)), aug_spec, aug_spec),
        out_shape=(jax.ShapeDtypeStruct((batch * seq, BR_WIDTH), F32), aug, aug),
        compiler_params=_cparams(("parallel", "parallel", "parallel")),
        name="nsa_cmp",
    )(att, att, att, cmp_kv, cmp_kv, _cmp_to_slc_map(n_pad, n_cmp), _lane_masks(PAIR_MASKS))


def _resident_spec(shape):
    nd = len(shape)
    return pl.BlockSpec(shape, lambda *_: (0,) * nd, pipeline_mode=pl.Buffered(1))


def _gate_expand_matrix():
    m = np.zeros((3, 128, BR_WIDTH), np.float32)
    for c in range(3):
        for h in range(N_HEADS):
            m[c, N_HEADS + 3 * h + c, h * HEAD_DIM:(h + 1) * HEAD_DIM] = 1.0
    return jnp.asarray(m, BF16)


def _merge_kernel(x_ref, hb_ref, oa_ref, ob_ref, oc_ref, odc_ref, ods_ref, odw_ref, small_ref,
                  g1_ref, sc2_ref, sh2_ref, gn2_ref, ex_ref, wbr_ref, wg_ref, bg_ref, wo_ref,
                  xo_ref, h2_ref):
    d = x_ref.shape[1]
    sm = small_ref[...]
    sm_hi = sm.astype(BF16)
    sm_lo = (sm - sm_hi.astype(F32)).astype(BF16)
    o_d = jnp.zeros(odc_ref.shape, F32)
    for c, ref in enumerate((odc_ref, ods_ref, odw_ref)):
        gate = (jnp.dot(sm_hi, ex_ref[c], preferred_element_type=F32)
                + jnp.dot(sm_lo, ex_ref[c], preferred_element_type=F32))
        o_d = o_d + gate * ref[...]
    hb = hb_ref[...]
    branches = (oa_ref[...], ob_ref[...], oc_ref[...], o_d.astype(BF16))
    merged = jnp.zeros((x_ref.shape[0], d), F32)
    for g, o in enumerate(branches):
        br = jnp.dot(o, wbr_ref[g], preferred_element_type=F32)
        z = jnp.dot(hb, wg_ref[:, g * d:(g + 1) * d], preferred_element_type=F32) + bg_ref[:, g * d:(g + 1) * d]
        merged = merged + jax.nn.sigmoid(z) * br
    y = jnp.dot(merged.astype(BF16), wo_ref[...], preferred_element_type=F32)
    x = x_ref[...] + g1_ref[...] * y
    xo_ref[...] = x
    ms = jnp.mean(x * x, axis=-1, keepdims=True)
    h2 = (x * lax.rsqrt(ms + EPS) * gn2_ref[...]) * (1.0 + sc2_ref[...]) + sh2_ref[...]
    h2_ref[...] = h2.astype(BF16)


def _mod_spec(mod, d, tm, rows_per_mod):
    if rows_per_mod == 1:
        return pl.BlockSpec((tm, d), lambda i: (i, 0)), mod
    tpm = rows_per_mod // tm
    return pl.BlockSpec((None, 1, d), lambda i: (i // tpm, 0, 0)), mod.reshape(-1, 1, d)


def _merge(x2d, hb, o_a, o_b, o_c, o_dc, o_ds, o_dw, small, g1, sc2, sh2, gn2,
           w_branch, w_gate, b_gate, w_out, *, rows_per_mod, tm):
    n, d = x2d.shape
    tm = min(tm, n)
    row = lambda w: pl.BlockSpec((tm, w), lambda i: (i, 0))
    mods = [_mod_spec(m, d, tm, rows_per_mod) for m in (g1, sc2, sh2)]
    return pl.pallas_call(
        _merge_kernel,
        grid=(n // tm,),
        in_specs=[row(d), row(d)] + [row(BR_WIDTH)] * 6 + [row(128)] + [m[0] for m in mods]
                 + [_const_spec((1, d)), _const_spec((3, 128, BR_WIDTH)),
                    _resident_spec(w_branch.shape), _resident_spec(w_gate.shape),
                    _const_spec((1, 4 * d)), _resident_spec(w_out.shape)],
        out_specs=(row(d), row(d)),
        out_shape=(jax.ShapeDtypeStruct((n, d), F32), jax.ShapeDtypeStruct((n, d), BF16)),
        compiler_params=_cparams(("parallel",)),
        name="merge",
    )(x2d, hb, o_a, o_b, o_c, o_dc, o_ds, o_dw, small, *[m[1] for m in mods],
      gn2.reshape(1, d).astype(F32), _gate_expand_matrix(),
      w_branch.astype(BF16), w_gate.astype(BF16), b_gate.reshape(1, 4 * d).astype(F32), w_out.astype(BF16))


def _swiglu_acc(h, wg_ref, wu_ref, wd_ref, chunk):
    ff = wg_ref.shape[-1]
    acc = None
    for c0 in range(0, ff, chunk):
        g = jnp.dot(h, wg_ref[:, c0:c0 + chunk], preferred_element_type=F32)
        u = jnp.dot(h, wu_ref[:, c0:c0 + chunk], preferred_element_type=F32)
        part = jnp.dot((g * jax.nn.sigmoid(g) * u).astype(BF16), wd_ref[c0:c0 + chunk, :],
                       preferred_element_type=F32)
        acc = part if acc is None else acc + part
    return acc


def _ffn_kernel(x_ref, h_ref, g2_ref, wg_ref, wu_ref, wd_ref, o_ref, *, chunk):
    o_ref[...] = x_ref[...] + g2_ref[...] * _swiglu_acc(h_ref[...], wg_ref, wu_ref, wd_ref, chunk)


def _ff_chunk(ff):
    for c in (1408, 1024, 896, 512, 256, 128):
        if ff % c == 0:
            return c
    return ff


def _ffn(x2d, h2, g2, wg, wu, wd, *, rows_per_mod, tm):
    n, d = x2d.shape
    tm = min(tm, n)
    row = lambda w: pl.BlockSpec((tm, w), lambda i: (i, 0))
    g2_spec, g2 = _mod_spec(g2, d, tm, rows_per_mod)
    return pl.pallas_call(
        functools.partial(_ffn_kernel, chunk=_ff_chunk(wg.shape[1])),
        grid=(n // tm,),
        in_specs=[row(d), row(d), g2_spec, _resident_spec(wg.shape), _resident_spec(wu.shape),
                  _resident_spec(wd.shape)],
        out_specs=row(d),
        out_shape=jax.ShapeDtypeStruct((n, d), F32),
        compiler_params=_cparams(("parallel",)),
        name="ffn",
    )(x2d, h2, g2, wg.astype(BF16), wu.astype(BF16), wd.astype(BF16))


def _route_top2(logits, n_exp):
    lane = lax.broadcasted_iota(jnp.int32, logits.shape, 1)
    real = lane < n_exp
    logits = jnp.where(real, logits, NEG)
    ex = jnp.exp(logits - jnp.max(logits, axis=-1, keepdims=True))
    prob = ex / jnp.sum(ex, axis=-1, keepdims=True)
    avail = real
    comb = jnp.zeros(prob.shape, F32)
    for _ in range(2):
        val = jnp.where(avail, prob, -1.0)
        mx = jnp.max(val, axis=-1, keepdims=True)
        idx = jnp.min(jnp.where(jnp.logical_and(avail, val == mx), lane, 1 << 20), axis=-1, keepdims=True)
        pick = lane == idx
        comb = jnp.where(pick, prob, comb)
        avail = jnp.logical_and(avail, jnp.logical_not(pick))
    return comb / jnp.sum(comb, axis=-1, keepdims=True)


def _moe_sparse_kernel(h_ref, wr_ref, br_ref, ltri_ref, wg_ref, wu_ref, wd_ref, o_ref,
                       comb_ref, rank_ref, rank_t_ref, comb_t_ref, *, chunk, n_exp, cap):
    e = pl.program_id(1)
    tm = h_ref.shape[0]
    h = h_ref[...]

    @pl.when(e == 0)
    def _():
        comb = _route_top2(jnp.dot(h, wr_ref[...], preferred_element_type=F32) + br_ref[...], n_exp)
        routed = jnp.where(comb > 0.0, 1.0, 0.0)
        rank = jnp.dot(ltri_ref[...], routed.astype(BF16), preferred_element_type=F32)
        comb_ref[...] = comb
        rank_ref[...] = rank
        blk = min(tm, 128)
        for i in range(tm // blk):
            rows = slice(i * blk, (i + 1) * blk)
            rank_t_ref[:, rows] = jnp.transpose(rank[rows, :])
            comb_t_ref[:, rows] = jnp.transpose(comb[rows, :])
        o_ref[...] = jnp.zeros(o_ref.shape, F32)

    lane = lax.broadcasted_iota(jnp.int32, (tm, 128), 1)
    mine = lane == e
    w_col = jnp.sum(jnp.where(mine, comb_ref[...], 0.0), axis=-1, keepdims=True)
    key_col = jnp.where(w_col > 0.0, jnp.sum(jnp.where(mine, rank_ref[...], 0.0), axis=-1, keepdims=True), -1.0)
    key_row = jnp.where(comb_t_ref[pl.ds(e, 1), :] > 0.0, rank_t_ref[pl.ds(e, 1), :], -1.0)
    count = jnp.sum(jnp.where(w_col > 0.0, 1.0, 0.0))

    for c in range(-(-tm // cap)):
        @pl.when(count > c * cap)
        def _(c=c):
            slot_r = (lax.broadcasted_iota(jnp.int32, (cap, tm), 0) + c * cap).astype(F32)
            gather = jnp.where(key_row == slot_r, 1.0, 0.0).astype(BF16)
            hc = jnp.dot(gather, h, preferred_element_type=F32).astype(BF16)
            y = _swiglu_acc(hc, wg_ref, wu_ref, wd_ref, chunk)
            slot_c = (lax.broadcasted_iota(jnp.int32, (tm, cap), 1) + c * cap).astype(F32)
            scatter = jnp.where(key_col == slot_c, 1.0, 0.0).astype(BF16)
            o_ref[...] += w_col * jnp.dot(scatter, y.astype(BF16), preferred_element_type=F32)


def _moe_sparse(h2, router, router_b, wg, wu, wd, *, tm, cap):
    n, d = h2.shape
    n_exp = wg.shape[0]
    tm = min(tm, n)
    cap = min(cap, tm)
    assert n % tm == 0 and tm % min(tm, 128) == 0
    wr = jnp.pad(router.astype(BF16), ((0, 0), (0, 128 - n_exp)))
    br = jnp.pad(router_b.astype(F32), (0, 128 - n_exp)).reshape(1, 128)
    ltri = jnp.asarray(np.tril(np.ones((tm, tm), np.float32), -1), BF16)
    exp_spec = lambda a: pl.BlockSpec((None,) + a.shape[1:], lambda i, e: (e, 0, 0))
    cst = lambda shape: pl.BlockSpec(shape, lambda i, e: (0,) * len(shape))
    return pl.pallas_call(
        functools.partial(_moe_sparse_kernel, chunk=_ff_chunk(wg.shape[2]), n_exp=n_exp, cap=cap),
        grid=(n // tm, n_exp),
        in_specs=[pl.BlockSpec((tm, d), lambda i, e: (i, 0)), cst((d, 128)), cst((1, 128)), cst((tm, tm)),
                  exp_spec(wg), exp_spec(wu), exp_spec(wd)],
        out_specs=pl.BlockSpec((tm, d), lambda i, e: (i, 0)),
        out_shape=jax.ShapeDtypeStruct((n, d), F32),
        scratch_shapes=[pltpu.VMEM((tm, 128), F32), pltpu.VMEM((tm, 128), F32),
                        pltpu.VMEM((128, tm), F32), pltpu.VMEM((128, tm), F32)],
        compiler_params=_cparams(("parallel", "arbitrary")),
        name="moe_sparse",
    )(h2, wr, br, ltri, wg.astype(BF16), wu.astype(BF16), wd.astype(BF16))


def _residual_kernel(x_ref, y_ref, g_ref, o_ref):
    o_ref[...] = x_ref[...] + g_ref[...] * y_ref[...]


def _residual(x2d, y, g, *, rows_per_mod, tm):
    n, d = x2d.shape
    tm = min(tm, n)
    row = pl.BlockSpec((tm, d), lambda i: (i, 0))
    g_spec, g = _mod_spec(g, d, tm, rows_per_mod)
    return pl.pallas_call(
        _residual_kernel, grid=(n // tm,), in_specs=[row, row, g_spec], out_specs=row,
        out_shape=jax.ShapeDtypeStruct((n, d), F32), compiler_params=_cparams(("parallel",)),
        name="residual",
    )(x2d, y, g)


def _moe_kernel(x_ref, h_ref, g2_ref, wr_ref, br_ref, wg_ref, wu_ref, wd_ref, o_ref,
                comb_ref, acc_ref, *, chunk, n_exp):
    e = pl.program_id(1)
    h = h_ref[...]

    @pl.when(e == 0)
    def _():
        logits = jnp.dot(h, wr_ref[...], preferred_element_type=F32) + br_ref[...]
        lane = lax.broadcasted_iota(jnp.int32, logits.shape, 1)
        real = lane < n_exp
        logits = jnp.where(real, logits, NEG)
        ex = jnp.exp(logits - jnp.max(logits, axis=-1, keepdims=True))
        prob = ex / jnp.sum(ex, axis=-1, keepdims=True)
        avail = real
        comb = jnp.zeros(prob.shape, F32)
        for _ in range(2):
            val = jnp.where(avail, prob, -1.0)
            mx = jnp.max(val, axis=-1, keepdims=True)
            idx = jnp.min(jnp.where(jnp.logical_and(avail, val == mx), lane, 1 << 20), axis=-1, keepdims=True)
            pick = lane == idx
            comb = jnp.where(pick, prob, comb)
            avail = jnp.logical_and(avail, jnp.logical_not(pick))
        comb_ref[...] = comb / jnp.sum(comb, axis=-1, keepdims=True)
        acc_ref[...] = jnp.zeros(acc_ref.shape, F32)

    lane = lax.broadcasted_iota(jnp.int32, comb_ref.shape, 1)
    w_e = jnp.sum(jnp.where(lane == e, comb_ref[...], 0.0), axis=-1, keepdims=True)

    @pl.when(jnp.max(w_e) > 0.0)
    def _():
        acc_ref[...] += w_e * _swiglu_acc(h, wg_ref, wu_ref, wd_ref, chunk)

    @pl.when(e == n_exp - 1)
    def _():
        o_ref[...] = x_ref[...] + g2_ref[...] * acc_ref[...]


def _moe(x2d, h2, g2, router, router_b, wg, wu, wd, *, rows_per_mod, tm):
    n, d = x2d.shape
    n_exp = wg.shape[0]
    tm = min(tm, n)
    row = lambda w: pl.BlockSpec((tm, w), lambda i, e: (i, 0))
    if rows_per_mod == 1:
        g2_spec = row(d)
    else:
        tpm = rows_per_mod // tm
        g2_spec = pl.BlockSpec((None, 1, d), lambda i, e: (i // tpm, 0, 0))
        g2 = g2.reshape(-1, 1, d)
    wr = jnp.pad(router.astype(BF16), ((0, 0), (0, 128 - n_exp)))
    br = jnp.pad(router_b.astype(F32), (0, 128 - n_exp)).reshape(1, 128)
    exp_spec = lambda a: pl.BlockSpec((None,) + a.shape[1:], lambda i, e: (e, 0, 0))
    return pl.pallas_call(
        functools.partial(_moe_kernel, chunk=_ff_chunk(wg.shape[2]), n_exp=n_exp),
        grid=(n // tm, n_exp),
        in_specs=[row(d), row(d), g2_spec,
                  pl.BlockSpec((d, 128), lambda i, e: (0, 0)), pl.BlockSpec((1, 128), lambda i, e: (0, 0)),
                  exp_spec(wg), exp_spec(wu), exp_spec(wd)],
        out_specs=row(d),
        out_shape=jax.ShapeDtypeStruct((n, d), F32),
        scratch_shapes=[pltpu.VMEM((tm, 128), F32), pltpu.VMEM((tm, d), F32)],
        compiler_params=_cparams(("parallel", "arbitrary")),
        name="moe",
    )(x2d, h2, g2, wr, br, wg.astype(BF16), wu.astype(BF16), wd.astype(BF16))


def _decode_row_masks(diff):
    lane = np.arange(BR_WIDTH)
    qm = np.zeros((8, BR_WIDTH), np.float32)
    om = np.zeros((8, BR_WIDTH), np.float32)
    for r in range(8 if diff else N_HEADS):
        h = r // 2 if diff else r
        lo, hi = (h * HEAD_DIM + (r % 2) * DIFF_DIM, h * HEAD_DIM + (r % 2 + 1) * DIFF_DIM) if diff \
            else (h * HEAD_DIM, (h + 1) * HEAD_DIM)
        qm[r] = (lane >= lo) & (lane < hi)
        om[r] = (lane >= h * HEAD_DIM) & (lane < (h + 1) * HEAD_DIM)
    return jnp.asarray(qm, BF16), jnp.asarray(om, F32)


def _segment_matrices(n_keys, blk):
    seg = (np.arange(n_keys)[:, None] // blk == np.arange(128)[None, :]).astype(np.float32)
    return jnp.asarray(seg / blk, F32), jnp.asarray(seg.T, BF16)


def _decode_kernel(pt_ref, *refs, n_pg, spb, dense, mode, fox, diff, lam_scale, skip_first, shift_out, n_alias):
    del pt_ref
    refs = list(refs)
    q_ref, kvn_ref = refs[:2]
    n_in = 1 if dense else spb * n_pg
    pages = refs[2:2 + n_in]
    dense_ref = pages[0] if dense else None
    refs = refs[2 + n_in:]
    if fox:
        lf_pages = refs[:n_in]
        smalln_ref, diag_ref = refs[n_in:n_in + 2]
        refs = refs[n_in + 2:]
    if mode == "sel":
        sel_ref = refs.pop(0)
    qm_ref, om_ref = refs[:2]
    refs = refs[2:]
    if mode == "sel":
        segt_ref = refs.pop(0)
    if diff:
        lam_ref, bd_ref, gsub_ref = refs[:3]
        refs = refs[3:]
    if shift_out:
        newt_ref = refs.pop(0)
        refs = refs[n_alias:]
        o_ref, win_ref = refs
    else:
        (o_ref,) = refs

    for i in range(spb):
        sample_pages = [dense_ref.at[i]] if dense else pages[i * n_pg:(i + 1) * n_pg]
        rows = sample_pages[0].shape[-1]
        if shift_out:
            lane_s = lax.broadcasted_iota(jnp.int32, newt_ref.shape, 1)
            col = jnp.sum(jnp.where(lane_s == pl.program_id(0) * spb + i, newt_ref[...], 0.0),
                          axis=-1, keepdims=True)
            lane = lax.broadcasted_iota(jnp.int32, (BR_WIDTH, rows), 1)
            for kv in range(2):
                win_ref[i, kv] = jnp.where(lane == rows - 1, col[kv * BR_WIDTH:(kv + 1) * BR_WIDTH],
                                           pltpu.roll(sample_pages[0][kv], rows - 1, 1))

        qrows = q_ref[i:i + 1, :] * qm_ref[...]
        kn = kvn_ref[i:i + 1, :BR_WIDTH].astype(BF16).astype(F32)
        vn = kvn_ref[i:i + 1, BR_WIDTH:].astype(BF16).astype(F32)
        s_new = jnp.sum(qrows.astype(F32) * kn, axis=-1, keepdims=True)
        s = jnp.concatenate([jnp.dot(qrows, pg[0].astype(BF16), preferred_element_type=F32)
                             for pg in sample_pages], axis=1)
        n_keys = s.shape[1]
        if fox:
            carry = jnp.sum(diag_ref[...] * smalln_ref[i:i + 1, :], axis=-1, keepdims=True)
            lane_p = lax.broadcasted_iota(jnp.int32, (8, rows), 1)
            parts = []
            for lf_ref in reversed(lf_pages[i * n_pg:(i + 1) * n_pg]):
                lf = jnp.concatenate([lf_ref[...], jnp.zeros((8 - N_HEADS, rows), F32)], axis=0)
                suf = lf
                k = 1
                while k < rows:
                    suf = suf + jnp.where(lane_p < rows - k, pltpu.roll(suf, rows - k, 1), 0.0)
                    k *= 2
                parts.append(suf - lf + carry)
                carry = carry + jnp.sum(lf, axis=-1, keepdims=True)
            s = s + jnp.concatenate(parts[::-1], axis=1) * LOG2E
        keep = None
        if mode == "moba":
            nb = n_keys // MOBA_BLOCK
            lane = lax.broadcasted_iota(jnp.int32, (8, 128), 1)
            rel = jnp.zeros((8, 128), F32)
            for b in range(nb):
                tot = jnp.sum(s[:, b * MOBA_BLOCK:(b + 1) * MOBA_BLOCK], axis=-1, keepdims=True)
                rel = jnp.where(lane == b, tot, rel)
            picked = _select_bias(rel, jnp.where(lane < nb, lane, -1), jnp.full((8, 1), nb, jnp.int32), MOBA_TOPK)
            chosen = jnp.where(jnp.logical_and(picked == 0.0, lane < nb), 1.0, 0.0)
            s = jnp.concatenate(
                [jnp.where(jnp.sum(jnp.where(lane == b, chosen, 0.0), axis=-1, keepdims=True) > 0.5,
                           s[:, b * MOBA_BLOCK:(b + 1) * MOBA_BLOCK], NEG) for b in range(nb)], axis=1)
        elif mode == "sel":
            keep = jnp.dot(sel_ref[i].astype(BF16), segt_ref[...], preferred_element_type=F32) > 0.5
        if skip_first:
            fresh = lax.broadcasted_iota(jnp.int32, s.shape, 1) >= skip_first
            keep = fresh if keep is None else jnp.logical_and(keep, fresh)
        if keep is not None:
            s = jnp.where(keep, s, NEG)
        m = jnp.maximum(jnp.max(s, axis=-1, keepdims=True), s_new)
        e = jnp.exp2(s - m)
        e_new = jnp.exp2(s_new - m)
        denom = jnp.sum(e, axis=-1, keepdims=True) + e_new
        eb = e.astype(BF16)
        o = e_new.astype(BF16).astype(F32) * vn
        for j, pg in enumerate(sample_pages):
            o = o + _dot_nt(eb[:, j * rows:(j + 1) * rows], pg[1].astype(BF16))
        o = o / denom
        if diff:
            row = lax.broadcasted_iota(jnp.int32, (8, 1), 0)
            o = o * jnp.where(row % 2 == 0, 1.0, -lam_ref[0])
        out = jnp.sum(o * om_ref[...], axis=0, keepdims=True)
        if diff:
            out = _group_rms(out, bd_ref[...], gsub_ref[...]) * lam_scale
        o_ref[i:i + 1, :] = out


def _decode_attn(att3, q_col, kvn3, cache_t, li, page_table, *, mode="plain", fox=None, diff=None,
                 sel=None, blk=None, dense=False, skip_first=0, shift_out=None):
    s_n = att3.shape[0]
    spb = DECODE_SAMPLES_PER_STEP if s_n % DECODE_SAMPLES_PER_STEP == 0 else 1
    grp = lambda a: a.reshape((s_n // spb, spb) + a.shape[2:])
    rows = cache_t.shape[-1]
    n_pg = 1 if dense else page_table.shape[1]
    n_keys = n_pg * rows
    pt = page_table.reshape(-1).astype(jnp.int32)
    slots = [(i, j) for i in range(spb) for j in range(n_pg)]
    if dense:
        page_specs = [pl.BlockSpec((None, spb, 2, BR_WIDTH, rows), lambda b, pt: (li, b, 0, 0, 0))]
    else:
        page_specs = [pl.BlockSpec((None, None, 2, BR_WIDTH, rows),
                                   lambda b, pt, i=i, j=j: (li, pt[(b * spb + i) * n_pg + j], 0, 0, 0))
                      for i, j in slots]
    in_specs = [pl.BlockSpec((None, spb, BR_WIDTH), lambda b, pt: (b, 0, q_col)),
                pl.BlockSpec((None, spb, 2 * BR_WIDTH), lambda b, pt: (b, 0, 0))] + page_specs
    args = [grp(att3), grp(kvn3)] + [cache_t] * len(page_specs)
    cst = lambda a: (pl.BlockSpec(a.shape, lambda b, pt: (0,) * a.ndim), a)
    consts = []
    if fox is not None:
        logf_t, small3 = fox
        in_specs += [pl.BlockSpec((None, None, N_HEADS, rows),
                                  lambda b, pt, i=i, j=j: (li, pt[(b * spb + i) * n_pg + j], 0, 0)) for i, j in slots]
        args += [logf_t] * len(slots)
        in_specs.append(pl.BlockSpec((None, spb, 128), lambda b, pt: (b, 0, 0)))
        args.append(grp(small3))
        consts += [cst(jnp.asarray(np.eye(8, 128, dtype=np.float32) * (np.arange(8)[:, None] < N_HEADS)))]
    qm, om = _decode_row_masks(diff is not None)
    consts += [cst(qm), cst(om)]
    if mode == "sel":
        in_specs.append(pl.BlockSpec((spb, 8, 128), lambda b, pt: (b, 0, 0)))
        args.append(sel)
        consts.append(cst(_segment_matrices(n_keys, blk)[1]))
    in_specs += [c[0] for c in consts]
    args += [c[1] for c in consts]
    lam_scale = 1.0
    if diff is not None:
        lam, lam_scale, gsub = diff
        bd = _group_mean_matrix(BR_WIDTH, HEAD_DIM)
        in_specs += [pl.BlockSpec(memory_space=pltpu.SMEM), pl.BlockSpec(bd.shape, lambda b, pt: (0, 0)),
                     pl.BlockSpec((1, BR_WIDTH), lambda b, pt: (0, 0))]
        args += [lam.reshape(1).astype(F32), bd, gsub.reshape(1, BR_WIDTH)]
    out_specs = pl.BlockSpec((None, spb, BR_WIDTH), lambda b, pt: (b, 0, 0))
    out_shape = jax.ShapeDtypeStruct((s_n // spb, spb, BR_WIDTH), F32)
    aliases = {}
    if shift_out is not None:
        new_t, prev, depth = shift_out
        in_specs.append(pl.BlockSpec(new_t.shape, lambda b, pt: (0, 0)))
        args.append(new_t)
        if prev is not None:
            aliases = {len(args) + 1: 1}
            in_specs.append(pl.BlockSpec(memory_space=pl.ANY))
            args.append(prev)
        out_specs = (out_specs, pl.BlockSpec((None, spb, 2, BR_WIDTH, rows), lambda b, pt: (li, b, 0, 0, 0)))
        out_shape = (out_shape, jax.ShapeDtypeStruct((depth, s_n, 2, BR_WIDTH, rows), F32))
    kern = functools.partial(_decode_kernel, n_pg=n_pg, spb=spb, dense=dense, mode=mode, fox=fox is not None,
                             diff=diff is not None, lam_scale=lam_scale, skip_first=skip_first,
                             shift_out=shift_out is not None, n_alias=len(aliases))
    res = pl.pallas_call(
        kern,
        grid_spec=pltpu.PrefetchScalarGridSpec(
            num_scalar_prefetch=1, grid=(s_n // spb,), in_specs=in_specs, out_specs=out_specs),
        out_shape=out_shape,
        input_output_aliases=aliases,
        compiler_params=_cparams(("parallel",)),
        name="decode_" + mode + ("_fox" if fox is not None else "") + ("_diff" if diff is not None else "")
             + ("_dense" if dense else ""),
    )(pt, *args)
    if shift_out is not None:
        return res[0].reshape(s_n, 1, BR_WIDTH), res[1]
    return res.reshape(s_n, 1, BR_WIDTH)


def _decode_cmp_kernel(q_ref, kc_ref, vc_ref, map_ref, qm_ref, om_ref, oc_ref, sel_ref, *, n_cmp, own):
    qrows = q_ref[...] * qm_ref[...]
    s = _dot_nt(qrows, kc_ref[...].astype(BF16))
    visible = lax.broadcasted_iota(jnp.int32, s.shape, 1) < n_cmp
    s = jnp.where(visible, s, NEG)
    e = jnp.where(visible, jnp.exp2(s - jnp.max(s, axis=-1, keepdims=True)), 0.0)
    p = e / jnp.maximum(jnp.sum(e, axis=-1, keepdims=True), 1e-30)
    p_hi = p.astype(BF16)
    p_lo = (p - p_hi.astype(F32)).astype(BF16)
    o = jnp.dot(p_hi, vc_ref[...].astype(BF16), preferred_element_type=F32)
    oc_ref[...] = jnp.sum(o * om_ref[...], axis=0, keepdims=True)
    rel = (jnp.dot(p_hi, map_ref[...], preferred_element_type=F32)
           + jnp.dot(p_lo, map_ref[...], preferred_element_type=F32))
    lane = lax.broadcasted_iota(jnp.int32, rel.shape, 1)
    picked = _select_bias(rel, lane, jnp.full((8, 1), own, jnp.int32), NSA_TOPK)
    sel_ref[...] = jnp.where(jnp.logical_and(picked == 0.0, lane < own), 1.0, 0.0)


def _decode_cmp(att3, cmp_kv, n_cmp, own):
    s_n, n_pad, _ = cmp_kv.shape
    assert own <= 128
    start = np.arange(n_pad)[:, None] * NSA_CMP_STRIDE
    blk = np.arange(128)[None, :] * NSA_SLC_BLOCK
    ov = np.minimum(start + NSA_CMP_LEN, blk + NSA_SLC_BLOCK) - np.maximum(start, blk)
    cmap = np.maximum(ov, 0).astype(np.float32) / NSA_CMP_LEN
    cmap[n_cmp:] = 0.0
    qm, om = _decode_row_masks(False)
    return pl.pallas_call(
        functools.partial(_decode_cmp_kernel, n_cmp=n_cmp, own=own),
        grid=(s_n,),
        in_specs=[pl.BlockSpec((None, 1, BR_WIDTH), lambda b: (b, 0, G_QDN)),
                  pl.BlockSpec((None, n_pad, BR_WIDTH), lambda b: (b, 0, 0)),
                  pl.BlockSpec((None, n_pad, BR_WIDTH), lambda b: (b, 0, 1)),
                  _const_spec((n_pad, 128)), _const_spec((8, BR_WIDTH)), _const_spec((8, BR_WIDTH))],
        out_specs=(pl.BlockSpec((None, 1, BR_WIDTH), lambda b: (b, 0, 0)),
                   pl.BlockSpec((None, 8, 128), lambda b: (b, 0, 0))),
        out_shape=(jax.ShapeDtypeStruct((s_n, 1, BR_WIDTH), F32), jax.ShapeDtypeStruct((s_n, 8, 128), F32)),
        compiler_params=_cparams(("parallel",)),
        name="decode_cmp",
    )(att3, cmp_kv, cmp_kv, jnp.asarray(cmap, BF16), qm, om)


def _decode_nsa_cmp_kernel(pt_ref, *refs, n_pg, n_cmp, own):
    del pt_ref
    q_ref = refs[0]
    pages = refs[1:1 + n_pg]
    (w1_ref, b1_ref, w2_ref, b2_ref, gain_ref, map_ref, oc_ref, sel_ref, xs_ref, cst_ref) = refs[1 + n_pg:]
    rows = pages[0].shape[-1]
    n_ch = n_pg * rows // NSA_CMP_STRIDE
    for j, pg in enumerate(pages):
        for kv in range(2):
            for p in range(2):
                t = jnp.transpose(pg[kv, p * PAIR_LANES:(p + 1) * PAIR_LANES, :])
                for c in range(rows // NSA_CMP_STRIDE):
                    r0 = (j * (rows // NSA_CMP_STRIDE) + c) * CHUNK_PITCH
                    xs_ref[kv, p, r0:r0 + NSA_CMP_STRIDE, :] = t[c * NSA_CMP_STRIDE:(c + 1) * NSA_CMP_STRIDE, :]
    lane = lax.broadcasted_iota(jnp.int32, (n_ch, PAIR_LANES), 1)
    for kv in range(2):
        for p in range(2):
            for i in range(NSA_CMP_STRIDE // 2):
                x0 = xs_ref[kv, p, pl.ds(2 * i, n_ch, stride=CHUNK_PITCH), :]
                x1 = xs_ref[kv, p, pl.ds(2 * i + 1, n_ch, stride=CHUNK_PITCH), :]
                even = jnp.where(lane < HEAD_DIM, x0, pltpu.roll(x1, HEAD_DIM, 1))
                odd = jnp.where(lane < HEAD_DIM, pltpu.roll(x0, HEAD_DIM, 1), x1)
                cols = slice(i * PAIR_LANES, (i + 1) * PAIR_LANES)
                cst_ref[kv, (2 * p) * n_ch:(2 * p + 1) * n_ch, cols] = even.astype(BF16)
                cst_ref[kv, (2 * p + 1) * n_ch:(2 * p + 2) * n_ch, cols] = odd.astype(BF16)
    toks = []
    for kv in range(2):
        uv = jnp.dot(cst_ref[kv], w1_ref[kv], preferred_element_type=F32)
        hid = uv.shape[1] // 2
        pre = uv[:, :hid] + pltpu.roll(uv[:, hid:], N_HEADS * n_ch - 1, 0) + b1_ref[kv]
        y = jnp.dot(jax.nn.gelu(pre, approximate=True).astype(BF16), w2_ref[kv],
                    preferred_element_type=F32) + b2_ref[kv]
        if kv == 0:
            y = y * lax.rsqrt(jnp.mean(y * y, axis=-1, keepdims=True) + EPS) * gain_ref[...]
        toks.append(y.astype(BF16))
    yk, yv = toks
    q = q_ref[...].astype(F32)
    q8 = jnp.concatenate([q[:, h * HEAD_DIM:(h + 1) * HEAD_DIM] for h in range(N_HEADS)]
                         + [jnp.zeros((8 - N_HEADS, HEAD_DIM), F32)], axis=0).astype(BF16)
    s_all = _dot_nt(q8, yk)
    row = lax.broadcasted_iota(jnp.int32, (8, n_ch), 0)
    s = jnp.zeros((8, n_ch), F32)
    for h in range(N_HEADS):
        s = s + jnp.where(row == h, s_all[:, h * n_ch:(h + 1) * n_ch], 0.0)
    visible = lax.broadcasted_iota(jnp.int32, (8, n_ch), 1) < n_cmp
    s = jnp.where(visible, s, NEG)
    e = jnp.where(visible, jnp.exp2(s - jnp.max(s, axis=-1, keepdims=True)), 0.0)
    p = e / jnp.maximum(jnp.sum(e, axis=-1, keepdims=True), 1e-30)
    p_hi = p.astype(BF16)
    p_lo = (p - p_hi.astype(F32)).astype(BF16)
    p_all = jnp.concatenate([jnp.where(row == h, p_hi, jnp.zeros_like(p_hi)) for h in range(N_HEADS)], axis=1)
    o4 = jnp.dot(p_all, yv, preferred_element_type=F32)
    oc_ref[...] = jnp.concatenate([o4[h:h + 1, :] for h in range(N_HEADS)], axis=1)
    rel = (jnp.dot(p_hi, map_ref[...], preferred_element_type=F32)
           + jnp.dot(p_lo, map_ref[...], preferred_element_type=F32))
    lane_b = lax.broadcasted_iota(jnp.int32, rel.shape, 1)
    picked = _select_bias(rel, lane_b, jnp.full((8, 1), own, jnp.int32), NSA_TOPK)
    sel_ref[...] = jnp.where(jnp.logical_and(picked == 0.0, lane_b < own), 1.0, 0.0)


def _decode_nsa_cmp(att3, cache_t, li, page_table, cmpw, n_cmp, own):
    s_n = att3.shape[0]
    rows = cache_t.shape[-1]
    n_pg = page_table.shape[1]
    n_ch = n_pg * rows // NSA_CMP_STRIDE
    assert own <= 128
    w1ab, bias1, w2, b2, gain = _compress_weights(*cmpw)
    start = np.arange(n_ch)[:, None] * NSA_CMP_STRIDE
    blk = np.arange(128)[None, :] * NSA_SLC_BLOCK
    ov = np.minimum(start + NSA_CMP_LEN, blk + NSA_SLC_BLOCK) - np.maximum(start, blk)
    cmap = np.maximum(ov, 0).astype(np.float32) / NSA_CMP_LEN
    cmap[n_cmp:] = 0.0
    pt = page_table.reshape(-1).astype(jnp.int32)
    cst = lambda a: pl.BlockSpec(a.shape, lambda b, pt: (0,) * a.ndim)
    consts = [w1ab, bias1, w2, b2, gain, jnp.asarray(cmap, BF16)]
    return pl.pallas_call(
        functools.partial(_decode_nsa_cmp_kernel, n_pg=n_pg, n_cmp=n_cmp, own=own),
        grid_spec=pltpu.PrefetchScalarGridSpec(
            num_scalar_prefetch=1, grid=(s_n,),
            in_specs=[pl.BlockSpec((None, 1, BR_WIDTH), lambda b, pt: (b, 0, G_QDN))]
                     + [pl.BlockSpec((None, None, 2, BR_WIDTH, rows),
                                     lambda b, pt, j=j: (li, pt[b * n_pg + j], 0, 0, 0)) for j in range(n_pg)]
                     + [cst(a) for a in consts],
            out_specs=(pl.BlockSpec((None, 1, BR_WIDTH), lambda b, pt: (b, 0, 0)),
                       pl.BlockSpec((None, 8, 128), lambda b, pt: (b, 0, 0))),
            scratch_shapes=[pltpu.VMEM((2, 2, n_ch * CHUNK_PITCH, PAIR_LANES), F32),
                            pltpu.VMEM((2, N_HEADS * n_ch, NSA_CMP_STRIDE * HEAD_DIM), BF16)]),
        out_shape=(jax.ShapeDtypeStruct((s_n, 1, BR_WIDTH), F32), jax.ShapeDtypeStruct((s_n, 8, 128), F32)),
        compiler_params=_cparams(("parallel",)),
        name="decode_nsa_cmp",
    )(pt, att3, *([cache_t] * n_pg), *consts)


def _sample_attention(att, kvs, small, caches_t, logf_t, win_t, cmp_t, page_table, li, diff, cmpw, past_len,
                      win_prev):
    s_n = att.shape[0]
    kva, kvb, kvc, kvds, kvdw = (a.reshape(s_n, 1, 2 * BR_WIDTH) for a in kvs)
    att3 = att.reshape(s_n, 1, -1)
    ca, cb, cc, cds = caches_t
    dec = functools.partial(_decode_attn, att3, li=li, page_table=page_table)
    o_a = dec(G_QA, kva, ca, fox=(logf_t, small.reshape(s_n, 1, 128)))
    o_b = dec(G_QB, kvb, cb, diff=diff)
    o_c = dec(G_QC, kvc, cc, mode="moba")
    n_cmp = (past_len + 1 - NSA_CMP_LEN) // NSA_CMP_STRIDE + 1
    o_dc, sel = _decode_nsa_cmp(att3, cmp_t, li, page_table, cmpw, n_cmp, past_len // NSA_SLC_BLOCK)
    o_ds = dec(G_QD, kvds, cds, mode="sel", sel=sel, blk=NSA_SLC_BLOCK)
    buf_len = win_t.shape[-1]
    o_dw, win_next = dec(G_QD, kvdw, win_t, dense=True, skip_first=max(0, buf_len - NSA_WINDOW + 1),
                         shift_out=(jnp.transpose(kvs[4]), win_prev, win_t.shape[0]))
    flat = lambda a, dt: a.reshape(s_n, BR_WIDTH).astype(dt)
    return (flat(o_a, BF16), flat(o_b, BF16), flat(o_c, BF16), flat(o_dc, F32), flat(o_ds, F32),
            flat(o_dw, F32)), win_next


def _layer_gains(fox_qnorm, fox_knorm, diff_qnorm, diff_knorm, moba_qnorm, moba_knorm, nsa_qnorm, nsa_knorm):
    return jnp.stack([_tile_gain(fox_qnorm, HEAD_DIM), _tile_gain(fox_knorm, HEAD_DIM),
                      _tile_gain(diff_qnorm, DIFF_DIM), _tile_gain(diff_knorm, DIFF_DIM),
                      _tile_gain(moba_qnorm, HEAD_DIM), _tile_gain(moba_knorm, HEAD_DIM),
                      _tile_gain(nsa_qnorm, HEAD_DIM), _tile_gain(nsa_knorm[1], HEAD_DIM),
                      _tile_gain(nsa_knorm[2], HEAD_DIM)])


def _unchunk_tokens(tok, b, m):
    return jnp.transpose(tok.reshape(2, b, N_HEADS, m, HEAD_DIM), (1, 3, 0, 2, 4)).reshape(b, m, 2 * BR_WIDTH)


def _prompt_attention(att, kmean, chunks, small, diff, cmpw, batch, seq, *, tq=512, tk=1024):
    fl = functools.partial(_flash, batch=batch, seq=seq, tq=tq, tk=tk)
    ft8 = _cumsum_t(small, batch, seq)
    o_a = fl(att, att, att, q_col=G_QA, k_col=G_KA, v_col=G_VA, fox_t=ft8)
    o_b = fl(att, att, att, q_col=G_QB, k_col=G_KB, v_col=G_VB, diff=diff)
    qa_c, ka_c = _moba_router(att, kmean, batch, seq, tq=tq)
    o_c = fl(qa_c, ka_c, att, v_col=G_VC, aug=True)
    m = seq // NSA_CMP_STRIDE
    pos_emb, w1, b1, w2, b2, k_gain = cmpw
    tok = _compress(chunks, pos_emb, w1, b1, w2, b2, k_gain,
                    chunks_per_seq=m, seqs_per_tile=max(1, 512 // m))
    cmp_kv = _unchunk_tokens(tok, batch, m)
    o_dc, qa_d, ka_d = _nsa_cmp(att, cmp_kv, batch, seq, m - 1, tq=tq)
    o_ds = fl(qa_d, ka_d, att, v_col=G_VDS, aug=True, out_dtype=F32)
    o_dw = fl(att, att, att, q_col=G_QD, k_col=G_KDW, v_col=G_VDW, window=NSA_WINDOW, out_dtype=F32, tk=tq)
    return o_a, o_b, o_c, o_dc, o_ds, o_dw


def _trunk_tail(x2d, hb, branch, small, g1, sc2, sh2, g2, gn2, w_branch, w_gate, b_gate, w_out, ffn,
                *, rows_per_mod, tm=512):
    x2d, h2 = _merge(x2d, hb, *branch, small, g1, sc2, sh2, gn2, w_branch, w_gate, b_gate, w_out,
                     rows_per_mod=rows_per_mod, tm=tm)
    if ffn[0] == "dense":
        return _ffn(x2d, h2, g2, *ffn[1:], rows_per_mod=rows_per_mod, tm=tm)
    y = _moe_sparse(h2, *ffn[1:], tm=2 * tm, cap=288)
    return _residual(x2d, y, g2, rows_per_mod=rows_per_mod, tm=tm)


def kernel(x_prompt, x_sample, c_prompt, c_sample, cache_a_kv, cache_a_logf, cache_b_kv, cache_c_kv, cache_d_cmp_kv, cache_d_slc_kv, state_d_win_kv, page_table, ada_w, ada_b, norm_attn, norm_ffn, w_in, fox_fbias, fox_qnorm, fox_knorm, diff_qnorm, diff_knorm, diff_lambda, diff_subnorm, moba_qnorm, moba_knorm, nsa_qnorm, nsa_knorm, nsa_cmp_pos, nsa_cmp_w1, nsa_cmp_b1, nsa_cmp_w2, nsa_cmp_b2, w_branch, w_gate, b_gate, w_out, ffn_w_gate, ffn_w_up, ffn_w_down, moe_router, moe_router_b, moe_w_gate, moe_w_up, moe_w_down):
    bp, t, d = x_prompt.shape
    bs = x_sample.shape[0]
    depth = w_in.shape[0]
    n_pages = page_table.shape[1]
    past_len = n_pages * cache_a_kv.shape[2]
    pos_p = jnp.arange(t, dtype=jnp.int32)
    pos_s = jnp.full((1,), past_len, jnp.int32)
    xp = x_prompt.reshape(bp * t, d)
    xs = x_sample.reshape(bs, d)
    c_all = jnp.concatenate([c_prompt, c_sample, jnp.zeros((-(bp + bs) % 8, d), F32)], axis=0)
    page_t = lambda c: jnp.transpose(c, (0, 1, 3, 4, 5, 2)).reshape(c.shape[:2] + (2, BR_WIDTH, c.shape[2]))
    caches_t = tuple(page_t(c) for c in (cache_a_kv, cache_b_kv, cache_c_kv, cache_d_slc_kv))
    logf_t = jnp.swapaxes(cache_a_logf, 2, 3)
    win_t = page_t(state_d_win_kv)
    cmp_t = page_t(cache_d_cmp_kv)
    kv_p = None
    logf_p = []
    st_s = [[] for _ in range(6)]
    win_s = None
    for li in range(depth):
        lam_init = 0.8 - 0.6 * math.exp(-0.3 * li)
        lq = diff_lambda[li].astype(F32)
        lam = jnp.exp(jnp.sum(lq[0] * lq[1])) - jnp.exp(jnp.sum(lq[2] * lq[3])) + lam_init
        diff = (lam, 1.0 - lam_init, jnp.tile(diff_subnorm[li].astype(F32), 2))
        cmpw = (nsa_cmp_pos[li], nsa_cmp_w1[li], nsa_cmp_b1[li], nsa_cmp_w2[li], nsa_cmp_b2[li], nsa_knorm[li][0])
        mod = _rowmm(c_all, ada_w[li], ada_b[li], silu_in=True)
        modp = mod[:bp].reshape(bp, 6, d)
        wm, ws, bsm = _split_w_in(w_in[li], fox_fbias[li])
        gains = _layer_gains(fox_qnorm[li], fox_knorm[li], diff_qnorm[li], diff_knorm[li], moba_qnorm[li],
                             moba_knorm[li], nsa_qnorm[li], nsa_knorm[li])
        tm_p = 512
        hb, att, *kv_p, small, km, chunks = _inproj(
            xp, modp[:, 1], modp[:, 0], norm_attn[li], wm, ws, bsm, gains, pos_p, rows_per_mod=t, tm=tm_p,
            stacked=(li, depth, kv_p))
        nb = t // MOBA_BLOCK
        kmean = km[:, :tm_p // MOBA_BLOCK].reshape(bp, nb, BR_WIDTH)
        kmean = jnp.pad(kmean, ((0, 0), (0, -nb % 8), (0, 0)))
        branch = _prompt_attention(att, kmean, chunks.reshape(2, -1, chunks.shape[-1]), small, diff, cmpw, bp, t)
        j = li // 2
        if li % 2 == 0:
            ffn = ("dense", ffn_w_gate[j], ffn_w_up[j], ffn_w_down[j])
        else:
            ffn = ("moe", moe_router[j], moe_router_b[j], moe_w_gate[j], moe_w_up[j], moe_w_down[j])
        xp = _trunk_tail(xp, hb, branch, small, modp[:, 2], modp[:, 4], modp[:, 3], modp[:, 5], norm_ffn[li],
                         w_branch[li], w_gate[li], b_gate[li], w_out[li], ffn, rows_per_mod=t)
        logf_p.append(small[:, :N_HEADS].reshape(bp, t, N_HEADS))
        mods = mod[bp:bp + bs].reshape(bs, 6, d)
        hb, att, kva, kvb, kvc, kvdc, kvds, kvdw, small = _inproj(
            xs, mods[:, 1], mods[:, 0], norm_attn[li], wm, ws, bsm, gains, pos_s, rows_per_mod=1, tm=bs)
        diff_s = (lam, 1.0 - lam_init, jnp.tile(diff_subnorm[li].astype(F32), N_HEADS))
        branch, win_s = _sample_attention(att, (kva, kvb, kvc, kvds, kvdw), small, caches_t, logf_t, win_t, cmp_t,
                                          page_table, li, diff_s, cmpw, past_len, win_s)
        xs = _trunk_tail(xs, hb, branch, small, mods[:, 2], mods[:, 4], mods[:, 3], mods[:, 5], norm_ffn[li],
                         w_branch[li], w_gate[li], b_gate[li], w_out[li], ffn, rows_per_mod=1)
        kv6 = lambda a: a.reshape(bs, 1, 2, N_HEADS, HEAD_DIM)
        for lst, val in zip(st_s, (kv6(kva), small[:, :N_HEADS].reshape(bs, 1, N_HEADS), kv6(kvb), kv6(kvc),
                                   kv6(kvdc), kv6(kvds))):
            lst.append(val)
    untr = lambda a: jnp.transpose(a.reshape(a.shape[:3] + (N_HEADS, HEAD_DIM, a.shape[-1])), (0, 1, 5, 2, 3, 4))
    win = min(NSA_WINDOW, t)
    kva_p, kvb_p, kvc_p, kvdc_p, kvds_p, kvdw_p = kv_p
    out_p = (untr(kva_p), jnp.stack(logf_p, axis=0), untr(kvb_p), untr(kvc_p), untr(kvdc_p), untr(kvds_p),
             untr(kvdw_p[..., t - win:]))
    out_s = tuple(jnp.stack(lst, axis=0) for lst in st_s) + (untr(win_s),)
    return (xp.reshape(bp, t, d), xs.reshape(bs, 1, d)) + out_p + out_s
```
